```python
import math
import jax, jax.numpy as jnp
from jax import lax
import numpy as np

D_MODEL = 2048
BATCH = 4
SEQ = 8192
DEPTH = 4
DEC_BATCH = 8
DEC_SEQ = 2048
PAST_LEN = 128

N_MIXERS = 2
N_ATTN_LAYERS = (DEPTH + 1) // 2
N_HYENA_LAYERS = DEPTH // 2
HEAD_DIM = 128
N_SLOTS = D_MODEL // HEAD_DIM
DILATED_GROUPS = ((128, 1), (512, 4), (2048, 16))
N_GROUPS = len(DILATED_GROUPS)
ATTN_WIDTH = N_SLOTS * HEAD_DIM
QKV_WIDTH = 3 * N_GROUPS * ATTN_WIDTH
BAND_BLOCK = 64
NEG_INF = -1e30
D_FF = 5632
HYENA_ORDER = 2
SHORT_CONV = 3
FILTER_BANDS = 16
FILTER_EMB = 2 * FILTER_BANDS + 1
FILTER_HIDDEN = 64
DECAY_TARGET = 1e-2
DECAY_FAST = 0.3
DECAY_SLOW = 1.5
MAX_DECAY = math.log(DECAY_TARGET) / DECAY_FAST
MIN_DECAY = math.log(DECAY_TARGET) / DECAY_SLOW
DN_ALPHA = (2.0 * DEPTH) ** 0.25
DN_BETA = (8.0 * DEPTH) ** -0.25
LN_EPS = 1e-5

kernel_name = 'hybrid_dilated_attn_hyena_encoder'


def _layer_norm(x, g, b):
    x32 = x.astype(jnp.float32)
    mu = jnp.mean(x32, axis=-1, keepdims=True)
    var = jnp.mean(jnp.square(x32 - mu), axis=-1, keepdims=True)
    y = (x32 - mu) * lax.rsqrt(var + LN_EPS) * g.astype(jnp.float32) + b.astype(jnp.float32)
    return y.astype(x.dtype)


def _swiglu(x, w_in, w_out):
    gate, up = jnp.split(x @ w_in, 2, axis=-1)
    return (jax.nn.silu(gate) * up) @ w_out


def _alibi_slopes():
    return jnp.exp2(-8.0 * jnp.arange(1, N_SLOTS + 1, dtype=jnp.float32) / N_SLOTS)


def _banded_dilated_attention(q, k, v, dilation, half, slopes):
    B, S, H, dh = q.shape
    d = dilation
    ls = S // d
    nb = -(-ls // BAND_BLOCK)
    lp = nb * BAND_BLOCK
    z = B * d

    def to_residue(t):
        t = t.reshape(B, ls, d, H, dh).transpose(0, 2, 1, 3, 4).reshape(z, ls, H, dh)
        return jnp.pad(t, ((0, 0), (0, lp - ls), (0, 0), (0, 0)))

    def band(t):
        t = jnp.pad(t, ((0, 0), (BAND_BLOCK, BAND_BLOCK), (0, 0), (0, 0))).reshape(z, nb + 2, BAND_BLOCK, H, dh)
        return jnp.concatenate([t[:, :-2], t[:, 1:-1], t[:, 2:]], axis=2)

    qb = to_residue(q).reshape(z, nb, BAND_BLOCK, H, dh)
    kb = band(to_residue(k))
    vb = band(to_residue(v))
    scores = jnp.einsum('znqhd,znkhd->znhqk', qb, kb, preferred_element_type=jnp.float32) * (dh ** -0.5)
    qi = jnp.arange(BAND_BLOCK)
    ki = jnp.arange(3 * BAND_BLOCK)
    rel = ki[None, :] - BAND_BLOCK - qi[:, None]
    key_pos = jnp.arange(nb)[:, None] * BAND_BLOCK - BAND_BLOCK + ki[None, :]
    valid = (jnp.abs(rel) <= half)[None] & ((key_pos >= 0) & (key_pos < ls))[:, None, :]
    alibi = -slopes[:, None, None] * (jnp.abs(rel) * d).astype(jnp.float32)[None]
    scores = jnp.where(valid[None, :, None], scores + alibi[None, None], NEG_INF)
    mx = jnp.max(scores, axis=-1)
    p = jnp.exp(scores - mx[..., None])
    den = jnp.sum(p, axis=-1)
    o = jnp.einsum('znhqk,znkhd->znqhd', p.astype(vb.dtype), vb, preferred_element_type=jnp.float32)
    mx = jnp.moveaxis(mx, 2, 3)
    den = jnp.moveaxis(den, 2, 3)
    o = o / den[..., None]

    def from_residue(t):
        rest = t.shape[4:]
        t = t.reshape((z, lp, H) + rest)[:, :ls]
        t = jnp.swapaxes(t.reshape((B, d, ls, H) + rest), 1, 2)
        return t.reshape((B, S, H) + rest)

    return from_residue(o), from_residue(mx), from_residue(den)


def _dilated_attention(x, w_qkv, w_o):
    B, S, _ = x.shape
    qkv = (x @ w_qkv).reshape(B, S, 3, N_GROUPS, N_SLOTS, HEAD_DIM)
    slopes = _alibi_slopes()
    outs, maxes, dens = [], [], []
    for g, (window, dilation) in enumerate(DILATED_GROUPS):
        o, mx, den = _banded_dilated_attention(qkv[:, :, 0, g], qkv[:, :, 1, g], qkv[:, :, 2, g],
                                               dilation, window // (2 * dilation), slopes)
        outs.append(o)
        maxes.append(mx)
        dens.append(den)
    o = jnp.stack(outs)
    mx = jnp.stack(maxes)
    den = jnp.stack(dens)
    wts = den * jnp.exp(mx - jnp.max(mx, axis=0, keepdims=True))
    out = jnp.sum(wts[..., None] * o, axis=0) / jnp.sum(wts, axis=0)[..., None]
    return out.reshape(B, S, ATTN_WIDTH).astype(x.dtype) @ w_o


def _short_conv(u, conv_w, conv_b):
    L = u.shape[1]
    pad = SHORT_CONV // 2
    up = jnp.pad(u, ((0, 0), (pad, SHORT_CONV - 1 - pad), (0, 0)))
    return sum(up[:, j:j + L] * conv_w[j] for j in range(SHORT_CONV)) + conv_b


def _hyena_filters(L, f_w1, f_b1, f_w2, f_b2, f_freq, f_w3):
    f32 = jnp.float32
    t = jnp.linspace(0.0, 1.0, L, dtype=f32)[:, None]
    w = 2.0 * math.pi * jnp.arange(L, dtype=f32)[:, None] / L
    bands = jnp.linspace(1e-4, FILTER_BANDS - 1, FILTER_BANDS, dtype=f32)[None, :]
    pos = jnp.concatenate([t, jnp.cos(bands * w), -jnp.sin(bands * w)], axis=-1)
    freq = f_freq.astype(f32)
    h = jnp.sin(freq * (pos @ f_w1.astype(f32) + f_b1.astype(f32)))
    h = jnp.sin(freq * (h @ f_w2.astype(f32) + f_b2.astype(f32)))
    h = (h @ f_w3.astype(f32)).reshape(L, HYENA_ORDER, 2, D_MODEL)
    deltas = jnp.abs(jnp.linspace(MIN_DECAY, MAX_DECAY, D_MODEL, dtype=f32))
    h = h * jnp.exp(-t * deltas)[:, None, None, :]
    return h / jnp.sum(jnp.abs(h), axis=(0, 2), keepdims=True)


def _bidir_fft_conv(u, h_fwd, h_bwd, skip):
    L = u.shape[1]
    c = jnp.concatenate([h_fwd.at[0].add(h_bwd[0]), jnp.zeros_like(h_fwd[:1]), h_bwd[:0:-1]], axis=0)
    u32 = u.astype(jnp.float32)
    uf = jnp.fft.rfft(u32, n=2 * L, axis=1)
    y = jnp.fft.irfft(uf * jnp.fft.rfft(c, axis=0)[None], n=2 * L, axis=1)[:, :L]
    return y + u32 * skip.astype(jnp.float32)


def _hyena_mixer(x, w_in, b_in, conv_w, conv_b, f_w1, f_b1, f_w2, f_b2, f_freq, f_w3, skip, w_out, b_out):
    L = x.shape[1]
    u = _short_conv(x @ w_in + b_in, conv_w, conv_b)
    v, g1, g2 = jnp.split(u, 3, axis=-1)
    h = _hyena_filters(L, f_w1, f_b1, f_w2, f_b2, f_freq, f_w3)
    z = g1.astype(jnp.float32) * _bidir_fft_conv(v, h[:, 0, 0], h[:, 0, 1], skip[0])
    z = g2.astype(jnp.float32) * _bidir_fft_conv(z, h[:, 1, 0], h[:, 1, 1], skip[1])
    return z.astype(x.dtype) @ w_out + b_out


def _trunk(x, ln_g, ln_b, ffn_w_in, ffn_w_out, attn_w_qkv, attn_w_o,
           hy_w_in, hy_b_in, hy_conv_w, hy_conv_b, hy_f_w1, hy_f_b1, hy_f_w2,
           hy_f_b2, hy_f_freq, hy_f_w3, hy_skip, hy_w_out, hy_b_out):
    for i in range(DEPTH):
        x = _layer_norm(DN_ALPHA * x + 0.5 * _swiglu(x, ffn_w_in[i, 0], ffn_w_out[i, 0]), ln_g[i, 0], ln_b[i, 0])
        j = i // N_MIXERS
        if i % N_MIXERS == 0:
            mix = _dilated_attention(x, attn_w_qkv[j], attn_w_o[j])
        else:
            mix = _hyena_mixer(x, hy_w_in[j], hy_b_in[j], hy_conv_w[j], hy_conv_b[j], hy_f_w1[j], hy_f_b1[j],
                               hy_f_w2[j], hy_f_b2[j], hy_f_freq[j], hy_f_w3[j], hy_skip[j], hy_w_out[j], hy_b_out[j])
        x = _layer_norm(DN_ALPHA * x + mix, ln_g[i, 1], ln_b[i, 1])
        x = _layer_norm(DN_ALPHA * x + 0.5 * _swiglu(x, ffn_w_in[i, 1], ffn_w_out[i, 1]), ln_g[i, 2], ln_b[i, 2])
    return x


def setup_inputs(seed: int = 0) -> dict:
    key = jax.random.key(seed)
    ks = jax.random.split(key, 24)
    f32 = jnp.float32

    def nrm(k, shape, scale):
        return jax.random.normal(k, shape, f32) * scale

    qkv_scale = jnp.array([1.0, 1.0, DN_BETA], f32)[:, None]
    attn_w_qkv = (nrm(ks[6], (N_ATTN_LAYERS, D_MODEL, 3, N_GROUPS * ATTN_WIDTH), D_MODEL ** -0.5)
                  * qkv_scale).reshape(N_ATTN_LAYERS, D_MODEL, QKV_WIDTH)
    return {
        'x_prompt': nrm(ks[0], (BATCH, SEQ, D_MODEL), 1.0),
        'x_sample': nrm(ks[1], (DEC_BATCH, DEC_SEQ, D_MODEL), 1.0),
        'ln_g': 1.0 + nrm(ks[2], (DEPTH, 3, D_MODEL), 0.02),
        'ln_b': nrm(ks[3], (DEPTH, 3, D_MODEL), 0.02),
        'ffn_w_in': nrm(ks[4], (DEPTH, 2, D_MODEL, 2 * D_FF), D_MODEL ** -0.5),
        'ffn_w_out': nrm(ks[5], (DEPTH, 2, D_FF, D_MODEL), D_FF ** -0.5 * DN_BETA),
        'attn_w_qkv': attn_w_qkv,
        'attn_w_o': nrm(ks[7], (N_ATTN_LAYERS, ATTN_WIDTH, D_MODEL), ATTN_WIDTH ** -0.5 * DN_BETA),
        'hy_w_in': nrm(ks[8], (N_HYENA_LAYERS, D_MODEL, 3 * D_MODEL), D_MODEL ** -0.5),
        'hy_b_in': nrm(ks[9], (N_HYENA_LAYERS, 3 * D_MODEL), 0.02),
        'hy_conv_w': nrm(ks[10], (N_HYENA_LAYERS, SHORT_CONV, 3 * D_MODEL), SHORT_CONV ** -0.5),
        'hy_conv_b': nrm(ks[11], (N_HYENA_LAYERS, 3 * D_MODEL), 0.02),
        'hy_f_w1': nrm(ks[12], (N_HYENA_LAYERS, FILTER_EMB, FILTER_HIDDEN), FILTER_EMB ** -0.5),
        'hy_f_b1': nrm(ks[13], (N_HYENA_LAYERS, FILTER_HIDDEN), 0.02),
        'hy_f_w2': nrm(ks[14], (N_HYENA_LAYERS, FILTER_HIDDEN, FILTER_HIDDEN), FILTER_HIDDEN ** -0.5),
        'hy_f_b2': nrm(ks[15], (N_HYENA_LAYERS, FILTER_HIDDEN), 0.02),
        'hy_f_freq': 1.0 + nrm(ks[16], (N_HYENA_LAYERS, FILTER_HIDDEN), 0.02),
        'hy_f_w3': nrm(ks[17], (N_HYENA_LAYERS, FILTER_HIDDEN, HYENA_ORDER * 2 * D_MODEL), FILTER_HIDDEN ** -0.5),
        'hy_skip': nrm(ks[18], (N_HYENA_LAYERS, HYENA_ORDER, D_MODEL), 1.0),
        'hy_w_out': nrm(ks[19], (N_HYENA_LAYERS, D_MODEL, D_MODEL), D_MODEL ** -0.5 * DN_BETA),
        'hy_b_out': nrm(ks[20], (N_HYENA_LAYERS, D_MODEL), 0.02),
    }


def reference(x_prompt, x_sample, ln_g, ln_b, ffn_w_in, ffn_w_out, attn_w_qkv, attn_w_o,
              hy_w_in, hy_b_in, hy_conv_w, hy_conv_b, hy_f_w1, hy_f_b1, hy_f_w2, hy_f_b2,
              hy_f_freq, hy_f_w3, hy_skip, hy_w_out, hy_b_out):
    y_prompt = _trunk(x_prompt, ln_g, ln_b, ffn_w_in, ffn_w_out, attn_w_qkv, attn_w_o,
                      hy_w_in, hy_b_in, hy_conv_w, hy_conv_b, hy_f_w1, hy_f_b1, hy_f_w2, hy_f_b2,
                      hy_f_freq, hy_f_w3, hy_skip, hy_w_out, hy_b_out)
    y_sample = _trunk(x_sample, ln_g, ln_b, ffn_w_in, ffn_w_out, attn_w_qkv, attn_w_o,
                      hy_w_in, hy_b_in, hy_conv_w, hy_conv_b, hy_f_w1, hy_f_b1, hy_f_w2, hy_f_b2,
                      hy_f_freq, hy_f_w3, hy_skip, hy_w_out, hy_b_out)
    return (y_prompt, y_sample)
```

```python
import functools
import math

import jax
import jax.numpy as jnp
from jax import lax
from jax.experimental import pallas as pl
from jax.experimental.pallas import tpu as pltpu

D_MODEL = 2048
DEPTH = 4
HEAD_DIM = 128
N_SLOTS = D_MODEL // HEAD_DIM
DILATED_GROUPS = ((128, 1), (512, 4), (2048, 16))
N_GROUPS = len(DILATED_GROUPS)
ATTN_WIDTH = N_SLOTS * HEAD_DIM
QKV_WIDTH = 3 * N_GROUPS * ATTN_WIDTH
HALF_WINDOW = 64
NEG_INF = -1e30
D_FF = 5632
HYENA_ORDER = 2
SHORT_CONV = 3
FILTER_BANDS = 16
DECAY_TARGET = 1e-2
DECAY_FAST = 0.3
DECAY_SLOW = 1.5
MAX_DECAY = math.log(DECAY_TARGET) / DECAY_FAST
MIN_DECAY = math.log(DECAY_TARGET) / DECAY_SLOW
DN_ALPHA = (2.0 * DEPTH) ** 0.25
LN_EPS = 1e-5

VMEM_LIMIT_BYTES = 48 * 1024 * 1024
BF16 = jnp.bfloat16
F32 = jnp.float32


def _layer_norm_rows(y, g, b):
    mu = jnp.mean(y, axis=-1, keepdims=True)
    c = y - mu
    var = jnp.mean(c * c, axis=-1, keepdims=True)
    return c * lax.rsqrt(var + LN_EPS) * g + b


def _ffn_kernel(x_ref, wg_ref, wu_ref, wo_ref, g_ref, b_ref, o_ref, xb_ref, *, nk):
    k = pl.program_id(1)

    @pl.when(k == 0)
    def _():
        xb_ref[...] = x_ref[...].astype(BF16)
        o_ref[...] = jnp.zeros_like(o_ref)

    xb = xb_ref[...]
    gate = jnp.dot(xb, wg_ref[...], preferred_element_type=F32)
    up = jnp.dot(xb, wu_ref[...], preferred_element_type=F32)
    act = (gate * jax.nn.sigmoid(gate) * up).astype(BF16)
    o_ref[...] += jnp.dot(act, wo_ref[...], preferred_element_type=F32)

    @pl.when(k == nk - 1)
    def _():
        y = DN_ALPHA * x_ref[...] + 0.5 * o_ref[...]
        o_ref[...] = _layer_norm_rows(y, g_ref[...], b_ref[...])


def _ffn_layer(x, w_in, w_out, g, b, *, tm=512, tf=512):
    t, dm = x.shape
    dff = w_out.shape[0]
    nk = dff // tf
    return pl.pallas_call(
        functools.partial(_ffn_kernel, nk=nk),
        grid=(t // tm, nk),
        in_specs=[
            pl.BlockSpec((tm, dm), lambda i, k: (i, 0)),
            pl.BlockSpec((dm, tf), lambda i, k: (0, k)),
            pl.BlockSpec((dm, tf), lambda i, k: (0, k + nk)),
            pl.BlockSpec((tf, dm), lambda i, k: (k, 0)),
            pl.BlockSpec((1, dm), lambda i, k: (0, 0)),
            pl.BlockSpec((1, dm), lambda i, k: (0, 0)),
        ],
        out_specs=pl.BlockSpec((tm, dm), lambda i, k: (i, 0)),
        out_shape=jax.ShapeDtypeStruct((t, dm), F32),
        scratch_shapes=[pltpu.VMEM((tm, dm), BF16)],
        compiler_params=pltpu.CompilerParams(
            dimension_semantics=("parallel", "arbitrary"),
            vmem_limit_bytes=VMEM_LIMIT_BYTES),
        name="ffn_ln",
    )(x, w_in, w_in, w_out, g.reshape(1, dm), b.reshape(1, dm))


def _proj_kernel(x_ref, w_ref, b_ref, o_ref, xb_ref):
    @pl.when(pl.program_id(1) == 0)
    def _():
        xb_ref[...] = x_ref[...].astype(BF16)

    acc = jnp.dot(xb_ref[...], w_ref[...], preferred_element_type=F32)
    o_ref[...] = (acc + b_ref[...]).astype(o_ref.dtype)


def _projection(x, w, bias, out_dtype, *, tm=1024, tn=1024):
    t, dm = x.shape
    n = w.shape[1]
    return pl.pallas_call(
        _proj_kernel,
        grid=(t // tm, n // tn),
        in_specs=[
            pl.BlockSpec((tm, dm), lambda i, j: (i, 0)),
            pl.BlockSpec((dm, tn), lambda i, j: (0, j)),
            pl.BlockSpec((1, tn), lambda i, j: (0, j)),
        ],
        out_specs=pl.BlockSpec((tm, tn), lambda i, j: (i, j)),
        out_shape=jax.ShapeDtypeStruct((t, n), out_dtype),
        scratch_shapes=[pltpu.VMEM((tm, dm), BF16)],
        compiler_params=pltpu.CompilerParams(
            dimension_semantics=("parallel", "arbitrary"),
            vmem_limit_bytes=VMEM_LIMIT_BYTES),
        name="projection",
    )(x, w, bias.reshape(1, n))


def _proj_ln_kernel(x_ref, z_ref, w_ref, bias_ref, g_ref, b_ref, o_ref):
    mix = jnp.dot(z_ref[...].astype(BF16), w_ref[...], preferred_element_type=F32)
    y = DN_ALPHA * x_ref[...] + (mix + bias_ref[...])
    o_ref[...] = _layer_norm_rows(y, g_ref[...], b_ref[...])


def _projection_ln(x, z, w, bias, g, b, *, tm=512):
    t, dm = x.shape
    kdim = z.shape[1]
    return pl.pallas_call(
        _proj_ln_kernel,
        grid=(t // tm,),
        in_specs=[
            pl.BlockSpec((tm, dm), lambda i: (i, 0)),
            pl.BlockSpec((tm, kdim), lambda i: (i, 0)),
            pl.BlockSpec((kdim, dm), lambda i: (0, 0)),
            pl.BlockSpec((1, dm), lambda i: (0, 0)),
            pl.BlockSpec((1, dm), lambda i: (0, 0)),
            pl.BlockSpec((1, dm), lambda i: (0, 0)),
        ],
        out_specs=pl.BlockSpec((tm, dm), lambda i: (i, 0)),
        out_shape=jax.ShapeDtypeStruct((t, dm), F32),
        compiler_params=pltpu.CompilerParams(
            dimension_semantics=("parallel",),
            vmem_limit_bytes=VMEM_LIMIT_BYTES),
        name="projection_ln",
    )(x, z, w, bias.reshape(1, dm), g.reshape(1, dm), b.reshape(1, dm))


def _attn_kernel(slopes_ref, *refs, tq, hg, seq_tokens):
    n_in = 7 * N_GROUPS
    in_refs = refs[:n_in]
    o_ref = refs[n_in]
    qf_ref, kf_ref, vf_ref, og_ref, lg_ref = refs[n_in + 1:]
    n_prompt, s_prompt, s_sample = seq_tokens

    t0 = pl.program_id(0) * tq
    is_prompt = t0 < n_prompt
    seq_len = jnp.where(is_prompt, s_prompt, s_sample)
    pos0 = lax.rem(t0, seq_len)
    scale = HEAD_DIM ** -0.5

    for hh in range(hg):
        cols = slice(hh * HEAD_DIM, (hh + 1) * HEAD_DIM)
        slope = slopes_ref[pl.program_id(1) * hg + hh]
        for g, (_, d) in enumerate(DILATED_GROUPS):
            q_ref, kp_ref, kc_ref, kn_ref, vp_ref, vc_ref, vn_ref = in_refs[7 * g:7 * g + 7]
            halo = HALF_WINDOW * d
            nq = tq // d
            sb = min(nq, 128)
            nkeys = sb + 2 * HALF_WINDOW
            chain_len = seq_len // d
            cpos0 = pos0 // d

            qf_ref[0:tq, :] = q_ref[:, cols].astype(F32)
            kf_ref[0:halo, :] = kp_ref[:, cols].astype(F32)
            kf_ref[halo:halo + tq, :] = kc_ref[:, cols].astype(F32)
            kf_ref[halo + tq:2 * halo + tq, :] = kn_ref[:, cols].astype(F32)
            vf_ref[0:halo, :] = vp_ref[:, cols].astype(F32)
            vf_ref[halo:halo + tq, :] = vc_ref[:, cols].astype(F32)
            vf_ref[halo + tq:2 * halo + tq, :] = vn_ref[:, cols].astype(F32)

            qi = lax.broadcasted_iota(jnp.int32, (sb, nkeys), 0)
            ki = lax.broadcasted_iota(jnp.int32, (sb, nkeys), 1)
            rel = ki - HALF_WINDOW - qi
            in_band = jnp.abs(rel) <= HALF_WINDOW
            alibi = (-slope) * (jnp.abs(rel) * d).astype(F32)

            def chain_block(idx, carry, d=d, sb=sb, nkeys=nkeys, nq=nq, g=g,
                            in_band=in_band, alibi=alibi, ki=ki,
                            chain_len=chain_len, cpos0=cpos0):
                r = idx // (nq // sb)
                s = idx % (nq // sb)
                start = r + s * (sb * d)
                if d == 1:
                    q_rows = pl.ds(start, sb)
                    k_rows = pl.ds(start, nkeys)
                else:
                    q_rows = pl.ds(start, sb, stride=d)
                    k_rows = pl.ds(start, nkeys, stride=d)
                qc = qf_ref[q_rows, :].astype(BF16)
                kc = kf_ref[k_rows, :].astype(BF16)
                vc = vf_ref[k_rows, :].astype(BF16)
                sc = lax.dot_general(qc, kc, (((1,), (1,)), ((), ())),
                                     preferred_element_type=F32) * scale
                kpos0 = cpos0 + s * sb - HALF_WINDOW
                valid = in_band & (ki >= -kpos0) & (ki < chain_len - kpos0)
                sc = jnp.where(valid, sc + alibi, NEG_INF)
                mx = jnp.max(sc, axis=-1, keepdims=True)
                p = jnp.exp(sc - mx)
                den = jnp.sum(p, axis=-1, keepdims=True)
                o = jnp.dot(p.astype(BF16), vc, preferred_element_type=F32) / den
                og_ref[g, q_rows, :] = o
                lg_ref[g, q_rows, :] = jnp.broadcast_to(mx + jnp.log(den), (sb, HEAD_DIM))
                return carry

            lax.fori_loop(0, d * (nq // sb), chain_block, 0)

        l0, l1, l2 = lg_ref[0], lg_ref[1], lg_ref[2]
        lmax = jnp.maximum(jnp.maximum(l0, l1), l2)
        w0, w1, w2 = jnp.exp(l0 - lmax), jnp.exp(l1 - lmax), jnp.exp(l2 - lmax)
        mixed = (w0 * og_ref[0] + w1 * og_ref[1] + w2 * og_ref[2]) / (w0 + w1 + w2)
        o_ref[:, cols] = mixed.astype(o_ref.dtype)


def _dilated_attention(qkv, slopes, seq_tokens, *, tq=1024, hg=4):
    t = qkv.shape[0]
    wcols = hg * HEAD_DIM
    blocks_per_part = ATTN_WIDTH // wcols
    in_specs = []
    args = []
    max_halo = HALF_WINDOW * max(d for _, d in DILATED_GROUPS)
    for g, (_, d) in enumerate(DILATED_GROUPS):
        halo = HALF_WINDOW * d
        ratio = tq // halo
        n_halo_blocks = t // halo

        def col(part, g=g):
            return lambda i, h, *_: (part * N_GROUPS + g) * blocks_per_part + h

        def cur(part, g=g):
            c = col(part, g)
            return pl.BlockSpec((tq, wcols), lambda i, h, *_: (i, c(i, h)))

        def prev(part, g=g, ratio=ratio):
            c = col(part, g)
            return pl.BlockSpec((halo, wcols),
                                lambda i, h, *_: (jnp.maximum(i * ratio - 1, 0), c(i, h)))

        def nxt(part, g=g, ratio=ratio, n_halo_blocks=n_halo_blocks):
            c = col(part, g)
            return pl.BlockSpec((halo, wcols),
                                lambda i, h, *_: (jnp.minimum((i + 1) * ratio, n_halo_blocks - 1), c(i, h)))

        in_specs += [cur(0), prev(1), cur(1), nxt(1), prev(2), cur(2), nxt(2)]
        args += [qkv] * 7

    grid_spec = pltpu.PrefetchScalarGridSpec(
        num_scalar_prefetch=1,
        grid=(t // tq, ATTN_WIDTH // wcols),
        in_specs=in_specs,
        out_specs=pl.BlockSpec((tq, wcols), lambda i, h, *_: (i, h)),
        scratch_shapes=[
            pltpu.VMEM((tq, HEAD_DIM), F32),
            pltpu.VMEM((tq + 2 * max_halo, HEAD_DIM), F32),
            pltpu.VMEM((tq + 2 * max_halo, HEAD_DIM), F32),
            pltpu.VMEM((N_GROUPS, tq, HEAD_DIM), F32),
            pltpu.VMEM((N_GROUPS, tq, HEAD_DIM), F32),
        ],
    )
    return pl.pallas_call(
        functools.partial(_attn_kernel, tq=tq, hg=hg, seq_tokens=seq_tokens),
        grid_spec=grid_spec,
        out_shape=jax.ShapeDtypeStruct((t, ATTN_WIDTH), BF16),
        compiler_params=pltpu.CompilerParams(
            dimension_semantics=("parallel", "arbitrary"),
            vmem_limit_bytes=VMEM_LIMIT_BYTES),
        name="dilated_attention",
    )(slopes, *args)


def _short_conv(u, conv_w, conv_b):
    length = u.shape[1]
    pad = SHORT_CONV // 2
    up = jnp.pad(u, ((0, 0), (pad, SHORT_CONV - 1 - pad), (0, 0)))
    return sum(up[:, j:j + length] * conv_w[j] for j in range(SHORT_CONV)) + conv_b


def _hyena_filters(length, f_w1, f_b1, f_w2, f_b2, f_freq, f_w3):
    hp = lax.Precision.HIGHEST
    t = jnp.linspace(0.0, 1.0, length, dtype=F32)[:, None]
    w = 2.0 * math.pi * jnp.arange(length, dtype=F32)[:, None] / length
    bands = jnp.linspace(1e-4, FILTER_BANDS - 1, FILTER_BANDS, dtype=F32)[None, :]
    pos = jnp.concatenate([t, jnp.cos(bands * w), -jnp.sin(bands * w)], axis=-1)
    h = jnp.sin(f_freq * (jnp.dot(pos, f_w1, precision=hp) + f_b1))
    h = jnp.sin(f_freq * (jnp.dot(h, f_w2, precision=hp) + f_b2))
    h = jnp.dot(h, f_w3, precision=hp).reshape(length, HYENA_ORDER, 2, D_MODEL)
    deltas = jnp.abs(jnp.linspace(MIN_DECAY, MAX_DECAY, D_MODEL, dtype=F32))
    h = h * jnp.exp(-t * deltas)[:, None, None, :]
    return h / jnp.sum(jnp.abs(h), axis=(0, 2), keepdims=True)


def _bidir_fft_conv(u, h_fwd, h_bwd, skip):
    length = u.shape[1]
    c = jnp.concatenate([h_fwd.at[0].add(h_bwd[0]), jnp.zeros_like(h_fwd[:1]), h_bwd[:0:-1]], axis=0)
    uf = jnp.fft.rfft(u, n=2 * length, axis=1)
    y = jnp.fft.irfft(uf * jnp.fft.rfft(c, axis=0)[None], n=2 * length, axis=1)[:, :length]
    return y + u * skip


def _hyena_inner(u, conv_w, conv_b, filt_params, skip):
    length = u.shape[1]
    u = _short_conv(u, conv_w, conv_b)
    v, g1, g2 = jnp.split(u, 3, axis=-1)
    h = _hyena_filters(length, *filt_params)
    z = g1 * _bidir_fft_conv(v, h[:, 0, 0], h[:, 0, 1], skip[0])
    z = g2 * _bidir_fft_conv(z, h[:, 1, 0], h[:, 1, 1], skip[1])
    return z


def kernel(x_prompt, x_sample, ln_g, ln_b, ffn_w_in, ffn_w_out, attn_w_qkv, attn_w_o, hy_w_in, hy_b_in,
           hy_conv_w, hy_conv_b, hy_f_w1, hy_f_b1, hy_f_w2, hy_f_b2, hy_f_freq, hy_f_w3, hy_skip,
           hy_w_out, hy_b_out):
    bp, sp, dm = x_prompt.shape
    bs, ss, _ = x_sample.shape
    n_prompt = bp * sp
    n_sample = bs * ss
    seq_tokens = (n_prompt, sp, ss)
    x = jnp.concatenate([x_prompt.reshape(n_prompt, dm), x_sample.reshape(n_sample, dm)], axis=0)
    slopes = jnp.exp2(-8.0 * jnp.arange(1, N_SLOTS + 1, dtype=F32) / N_SLOTS)
    zero_bias_qkv = jnp.zeros((QKV_WIDTH,), F32)
    zero_bias_dm = jnp.zeros((dm,), F32)

    for i in range(DEPTH):
        x = _ffn_layer(x, ffn_w_in[i, 0].astype(BF16), ffn_w_out[i, 0].astype(BF16), ln_g[i, 0], ln_b[i, 0])
        j = i // 2
        if i % 2 == 0:
            qkv = _projection(x, attn_w_qkv[j].astype(BF16), zero_bias_qkv, BF16)
            att = _dilated_attention(qkv, slopes, seq_tokens)
            x = _projection_ln(x, att, attn_w_o[j].astype(BF16), zero_bias_dm, ln_g[i, 1], ln_b[i, 1])
        else:
            u = _projection(x, hy_w_in[j].astype(BF16), hy_b_in[j], F32)
            filt = (hy_f_w1[j], hy_f_b1[j], hy_f_w2[j], hy_f_b2[j], hy_f_freq[j], hy_f_w3[j])
            zp = _hyena_inner(u[:n_prompt].reshape(bp, sp, 3 * dm), hy_conv_w[j], hy_conv_b[j], filt, hy_skip[j])
            zs = _hyena_inner(u[n_prompt:].reshape(bs, ss, 3 * dm), hy_conv_w[j], hy_conv_b[j], filt, hy_skip[j])
            z = jnp.concatenate([zp.reshape(n_prompt, dm), zs.reshape(n_sample, dm)], axis=0)
            x = _projection_ln(x, z, hy_w_out[j].astype(BF16), hy_b_out[j], ln_g[i, 1], ln_b[i, 1])
        x = _ffn_layer(x, ffn_w_in[i, 1].astype(BF16), ffn_w_out[i, 1].astype(BF16), ln_g[i, 2], ln_b[i, 2])

    return (x[:n_prompt].reshape(bp, sp, dm), x[n_prompt:].reshape(bs, ss, dm))
```

```python
import functools
import math

import jax
import jax.numpy as jnp
import ml_dtypes
import numpy as np
from jax import lax
from jax.experimental import pallas as pl
from jax.experimental.pallas import tpu as pltpu

D_MODEL = 2048
DEPTH = 4
HEAD_DIM = 128
N_SLOTS = D_MODEL // HEAD_DIM
DILATED_GROUPS = ((128, 1), (512, 4), (2048, 16))
N_GROUPS = len(DILATED_GROUPS)
ATTN_WIDTH = N_SLOTS * HEAD_DIM
QKV_WIDTH = 3 * N_GROUPS * ATTN_WIDTH
HALF_WINDOW = 64
NEG_INF = -1e30
D_FF = 5632
HYENA_ORDER = 2
SHORT_CONV = 3
FILTER_BANDS = 16
DECAY_TARGET = 1e-2
DECAY_FAST = 0.3
DECAY_SLOW = 1.5
MAX_DECAY = math.log(DECAY_TARGET) / DECAY_FAST
MIN_DECAY = math.log(DECAY_TARGET) / DECAY_SLOW
DN_ALPHA = (2.0 * DEPTH) ** 0.25
LN_EPS = 1e-5

LANES = 128
VMEM_LIMIT_BYTES = 48 * 1024 * 1024
BF16 = jnp.bfloat16
F32 = jnp.float32


def _layer_norm_rows(y, g, b):
    mu = jnp.mean(y, axis=-1, keepdims=True)
    c = y - mu
    var = jnp.mean(c * c, axis=-1, keepdims=True)
    return c * lax.rsqrt(var + LN_EPS) * g + b


def _seq_position(t0, seq_tokens):
    n_prompt, s_prompt, s_sample = seq_tokens
    seq_len = jnp.where(t0 < n_prompt, s_prompt, s_sample)
    return lax.rem(t0, seq_len), seq_len


def _ffn_kernel(x_ref, wg_ref, wu_ref, wo_ref, g_ref, b_ref, o_ref, xb_ref, *, nk):
    k = pl.program_id(1)

    @pl.when(k == 0)
    def _():
        xb_ref[...] = x_ref[...].astype(BF16)
        o_ref[...] = jnp.zeros_like(o_ref)

    xb = xb_ref[...]
    gate = jnp.dot(xb, wg_ref[...], preferred_element_type=F32)
    up = jnp.dot(xb, wu_ref[...], preferred_element_type=F32)
    act = (gate * jax.nn.sigmoid(gate) * up).astype(BF16)
    o_ref[...] += jnp.dot(act, wo_ref[...], preferred_element_type=F32)

    @pl.when(k == nk - 1)
    def _():
        y = DN_ALPHA * x_ref[...] + 0.5 * o_ref[...]
        o_ref[...] = _layer_norm_rows(y, g_ref[...], b_ref[...])


def _ffn_layer(x, w_in, w_out, g, b, *, tm=512, tf=512):
    t, dm = x.shape
    dff = w_out.shape[0]
    nk = dff // tf
    return pl.pallas_call(
        functools.partial(_ffn_kernel, nk=nk),
        grid=(t // tm, nk),
        in_specs=[
            pl.BlockSpec((tm, dm), lambda i, k: (i, 0)),
            pl.BlockSpec((dm, tf), lambda i, k: (0, k)),
            pl.BlockSpec((dm, tf), lambda i, k: (0, k + nk)),
            pl.BlockSpec((tf, dm), lambda i, k: (k, 0)),
            pl.BlockSpec((1, dm), lambda i, k: (0, 0)),
            pl.BlockSpec((1, dm), lambda i, k: (0, 0)),
        ],
        out_specs=pl.BlockSpec((tm, dm), lambda i, k: (i, 0)),
        out_shape=jax.ShapeDtypeStruct((t, dm), F32),
        scratch_shapes=[pltpu.VMEM((tm, dm), BF16)],
        compiler_params=pltpu.CompilerParams(
            dimension_semantics=("parallel", "arbitrary"),
            vmem_limit_bytes=VMEM_LIMIT_BYTES),
        name="ffn_ln",
    )(x, w_in, w_in, w_out, g.reshape(1, dm), b.reshape(1, dm))


def _proj_kernel(x_ref, w_ref, b_ref, o_ref, xb_ref):
    @pl.when(pl.program_id(1) == 0)
    def _():
        xb_ref[...] = x_ref[...].astype(BF16)

    acc = jnp.dot(xb_ref[...], w_ref[...], preferred_element_type=F32)
    o_ref[...] = (acc + b_ref[...]).astype(o_ref.dtype)


def _projection(x, w, bias, out_dtype, *, tm=1024, tn=1024):
    t, dm = x.shape
    n = w.shape[1]
    return pl.pallas_call(
        _proj_kernel,
        grid=(t // tm, n // tn),
        in_specs=[
            pl.BlockSpec((tm, dm), lambda i, j: (i, 0)),
            pl.BlockSpec((dm, tn), lambda i, j: (0, j)),
            pl.BlockSpec((1, tn), lambda i, j: (0, j)),
        ],
        out_specs=pl.BlockSpec((tm, tn), lambda i, j: (i, j)),
        out_shape=jax.ShapeDtypeStruct((t, n), out_dtype),
        scratch_shapes=[pltpu.VMEM((tm, dm), BF16)],
        compiler_params=pltpu.CompilerParams(
            dimension_semantics=("parallel", "arbitrary"),
            vmem_limit_bytes=VMEM_LIMIT_BYTES),
        name="projection",
    )(x, w, bias.reshape(1, n))


HALO_ROWS = 16


def _proj_conv_kernel(xp_ref, x_ref, xn_ref, w_ref, b_ref, cw_ref, cb_ref, o_ref, xb_ref, *, tm, seq_tokens):
    @pl.when(pl.program_id(1) == 0)
    def _():
        xb_ref[0:HALO_ROWS, :] = xp_ref[...].astype(BF16)
        xb_ref[HALO_ROWS:HALO_ROWS + tm, :] = x_ref[...].astype(BF16)
        xb_ref[HALO_ROWS + tm:, :] = xn_ref[...].astype(BF16)

    rows = tm + 2 * HALO_ROWS
    acc = jnp.dot(xb_ref[...], w_ref[...], preferred_element_type=F32) + b_ref[...]
    pos0, seq_len = _seq_position(pl.program_id(0) * tm, seq_tokens)
    tile_row = lax.broadcasted_iota(jnp.int32, (tm, 1), 0)
    at_seq_start = (tile_row == 0) & (pos0 == 0)
    at_seq_end = (tile_row == tm - 1) & (pos0 + tm == seq_len)
    cur = acc[HALO_ROWS:HALO_ROWS + tm]
    prev = pltpu.roll(acc, 1, axis=0)[HALO_ROWS:HALO_ROWS + tm]
    nxt = pltpu.roll(acc, rows - 1, axis=0)[HALO_ROWS:HALO_ROWS + tm]
    prev = jnp.where(at_seq_start, 0.0, prev)
    nxt = jnp.where(at_seq_end, 0.0, nxt)
    o_ref[...] = prev * cw_ref[0:1, :] + cur * cw_ref[1:2, :] + nxt * cw_ref[2:3, :] + cb_ref[...]


def _projection_short_conv(x, w, bias, conv_w, conv_b, seq_tokens, *, tm=1024, tn=1024):
    t, dm = x.shape
    n = w.shape[1]
    ratio = tm // HALO_ROWS
    n_halo_blocks = t // HALO_ROWS
    return pl.pallas_call(
        functools.partial(_proj_conv_kernel, tm=tm, seq_tokens=seq_tokens),
        grid=(t // tm, n // tn),
        in_specs=[
            pl.BlockSpec((HALO_ROWS, dm), lambda i, j: (jnp.maximum(i * ratio - 1, 0), 0)),
            pl.BlockSpec((tm, dm), lambda i, j: (i, 0)),
            pl.BlockSpec((HALO_ROWS, dm), lambda i, j: (jnp.minimum((i + 1) * ratio, n_halo_blocks - 1), 0)),
            pl.BlockSpec((dm, tn), lambda i, j: (0, j)),
            pl.BlockSpec((1, tn), lambda i, j: (0, j)),
            pl.BlockSpec((SHORT_CONV, tn), lambda i, j: (0, j)),
            pl.BlockSpec((1, tn), lambda i, j: (0, j)),
        ],
        out_specs=pl.BlockSpec((tm, tn), lambda i, j: (i, j)),
        out_shape=jax.ShapeDtypeStruct((t, n), F32),
        scratch_shapes=[pltpu.VMEM((tm + 2 * HALO_ROWS, dm), BF16)],
        compiler_params=pltpu.CompilerParams(
            dimension_semantics=("parallel", "arbitrary"),
            vmem_limit_bytes=VMEM_LIMIT_BYTES),
        name="projection_short_conv",
    )(x, x, x, w, bias.reshape(1, n), conv_w, conv_b.reshape(1, n))


def _proj_ln_kernel(x_ref, z_ref, w_ref, bias_ref, g_ref, b_ref, o_ref):
    mix = jnp.dot(z_ref[...].astype(BF16), w_ref[...], preferred_element_type=F32)
    y = DN_ALPHA * x_ref[...] + (mix + bias_ref[...])
    o_ref[...] = _layer_norm_rows(y, g_ref[...], b_ref[...])


def _projection_ln(x, z, w, bias, g, b, *, tm=512):
    t, dm = x.shape
    kdim = z.shape[1]
    return pl.pallas_call(
        _proj_ln_kernel,
        grid=(t // tm,),
        in_specs=[
            pl.BlockSpec((tm, dm), lambda i: (i, 0)),
            pl.BlockSpec((tm, kdim), lambda i: (i, 0)),
            pl.BlockSpec((kdim, dm), lambda i: (0, 0)),
            pl.BlockSpec((1, dm), lambda i: (0, 0)),
            pl.BlockSpec((1, dm), lambda i: (0, 0)),
            pl.BlockSpec((1, dm), lambda i: (0, 0)),
        ],
        out_specs=pl.BlockSpec((tm, dm), lambda i: (i, 0)),
        out_shape=jax.ShapeDtypeStruct((t, dm), F32),
        compiler_params=pltpu.CompilerParams(
            dimension_semantics=("parallel",),
            vmem_limit_bytes=VMEM_LIMIT_BYTES),
        name="projection_ln",
    )(x, z, w, bias.reshape(1, dm), g.reshape(1, dm), b.reshape(1, dm))


def _attn_kernel(slopes_ref, *refs, tq, hg, seq_tokens):
    n_in = 7 * N_GROUPS
    in_refs = refs[:n_in]
    o_ref = refs[n_in]
    qf_ref, kf_ref, vf_ref, og_ref, lg_ref = refs[n_in + 1:]

    pos0, seq_len = _seq_position(pl.program_id(0) * tq, seq_tokens)
    scale = HEAD_DIM ** -0.5

    for hh in range(hg):
        cols = slice(hh * HEAD_DIM, (hh + 1) * HEAD_DIM)
        slope = slopes_ref[pl.program_id(1) * hg + hh]
        for g, (_, d) in enumerate(DILATED_GROUPS):
            q_ref, kp_ref, kc_ref, kn_ref, vp_ref, vc_ref, vn_ref = in_refs[7 * g:7 * g + 7]
            halo = HALF_WINDOW * d
            nq = tq // d
            sb = min(nq, 128)
            nkeys = sb + 2 * HALF_WINDOW
            chain_len = seq_len // d
            cpos0 = pos0 // d

            qf_ref[0:tq, :] = q_ref[:, cols].astype(F32)
            kf_ref[0:halo, :] = kp_ref[:, cols].astype(F32)
            kf_ref[halo:halo + tq, :] = kc_ref[:, cols].astype(F32)
            kf_ref[halo + tq:2 * halo + tq, :] = kn_ref[:, cols].astype(F32)
            vf_ref[0:halo, :] = vp_ref[:, cols].astype(F32)
            vf_ref[halo:halo + tq, :] = vc_ref[:, cols].astype(F32)
            vf_ref[halo + tq:2 * halo + tq, :] = vn_ref[:, cols].astype(F32)

            qi = lax.broadcasted_iota(jnp.int32, (sb, nkeys), 0)
            ki = lax.broadcasted_iota(jnp.int32, (sb, nkeys), 1)
            rel = ki - HALF_WINDOW - qi
            in_band = jnp.abs(rel) <= HALF_WINDOW
            alibi = (-slope) * (jnp.abs(rel) * d).astype(F32)

            def chain_block(idx, carry, d=d, sb=sb, nkeys=nkeys, nq=nq, g=g,
                            in_band=in_band, alibi=alibi, ki=ki,
                            chain_len=chain_len, cpos0=cpos0):
                r = idx // (nq // sb)
                s = idx % (nq // sb)
                start = r + s * (sb * d)
                if d == 1:
                    q_rows = pl.ds(start, sb)
                    k_rows = pl.ds(start, nkeys)
                else:
                    q_rows = pl.ds(start, sb, stride=d)
                    k_rows = pl.ds(start, nkeys, stride=d)
                qc = qf_ref[q_rows, :].astype(BF16)
                kc = kf_ref[k_rows, :].astype(BF16)
                vc = vf_ref[k_rows, :].astype(BF16)
                sc = lax.dot_general(qc, kc, (((1,), (1,)), ((), ())),
                                     preferred_element_type=F32) * scale
                kpos0 = cpos0 + s * sb - HALF_WINDOW
                valid = in_band & (ki >= -kpos0) & (ki < chain_len - kpos0)
                sc = jnp.where(valid, sc + alibi, NEG_INF)
                mx = jnp.max(sc, axis=-1, keepdims=True)
                p = jnp.exp(sc - mx)
                den = jnp.sum(p, axis=-1, keepdims=True)
                o = jnp.dot(p.astype(BF16), vc, preferred_element_type=F32) / den
                og_ref[g, q_rows, :] = o
                lg_ref[g, q_rows, :] = jnp.broadcast_to(mx + jnp.log(den), (sb, HEAD_DIM))
                return carry

            lax.fori_loop(0, d * (nq // sb), chain_block, 0)

        l0, l1, l2 = lg_ref[0], lg_ref[1], lg_ref[2]
        lmax = jnp.maximum(jnp.maximum(l0, l1), l2)
        w0, w1, w2 = jnp.exp(l0 - lmax), jnp.exp(l1 - lmax), jnp.exp(l2 - lmax)
        mixed = (w0 * og_ref[0] + w1 * og_ref[1] + w2 * og_ref[2]) / (w0 + w1 + w2)
        o_ref[:, cols] = mixed.astype(o_ref.dtype)


def _dilated_attention(qkv, slopes, seq_tokens, *, tq=1024, hg=4):
    t = qkv.shape[0]
    wcols = hg * HEAD_DIM
    blocks_per_part = ATTN_WIDTH // wcols
    in_specs = []
    args = []
    max_halo = HALF_WINDOW * max(d for _, d in DILATED_GROUPS)
    for g, (_, d) in enumerate(DILATED_GROUPS):
        halo = HALF_WINDOW * d
        ratio = tq // halo
        n_halo_blocks = t // halo

        def col(part, g=g):
            return lambda i, h, *_: (part * N_GROUPS + g) * blocks_per_part + h

        def cur(part, g=g):
            c = col(part, g)
            return pl.BlockSpec((tq, wcols), lambda i, h, *_: (i, c(i, h)))

        def prev(part, g=g, ratio=ratio):
            c = col(part, g)
            return pl.BlockSpec((halo, wcols),
                                lambda i, h, *_: (jnp.maximum(i * ratio - 1, 0), c(i, h)))

        def nxt(part, g=g, ratio=ratio, n_halo_blocks=n_halo_blocks):
            c = col(part, g)
            return pl.BlockSpec((halo, wcols),
                                lambda i, h, *_: (jnp.minimum((i + 1) * ratio, n_halo_blocks - 1), c(i, h)))

        in_specs += [cur(0), prev(1), cur(1), nxt(1), prev(2), cur(2), nxt(2)]
        args += [qkv] * 7

    grid_spec = pltpu.PrefetchScalarGridSpec(
        num_scalar_prefetch=1,
        grid=(t // tq, ATTN_WIDTH // wcols),
        in_specs=in_specs,
        out_specs=pl.BlockSpec((tq, wcols), lambda i, h, *_: (i, h)),
        scratch_shapes=[
            pltpu.VMEM((tq, HEAD_DIM), F32),
            pltpu.VMEM((tq + 2 * max_halo, HEAD_DIM), F32),
            pltpu.VMEM((tq + 2 * max_halo, HEAD_DIM), F32),
            pltpu.VMEM((N_GROUPS, tq, HEAD_DIM), F32),
            pltpu.VMEM((N_GROUPS, tq, HEAD_DIM), F32),
        ],
    )
    return pl.pallas_call(
        functools.partial(_attn_kernel, tq=tq, hg=hg, seq_tokens=seq_tokens),
        grid_spec=grid_spec,
        out_shape=jax.ShapeDtypeStruct((t, ATTN_WIDTH), BF16),
        compiler_params=pltpu.CompilerParams(
            dimension_semantics=("parallel", "arbitrary"),
            vmem_limit_bytes=VMEM_LIMIT_BYTES),
        name="dilated_attention",
    )(slopes, *args)


TWIDDLE_RADIX = 16
PITCH_PAD = 8


def _split_hi_lo_np(m):
    hi = m.astype(ml_dtypes.bfloat16)
    lo = (m - hi.astype(np.float64)).astype(ml_dtypes.bfloat16)
    return np.stack([hi, lo])


@functools.lru_cache(maxsize=None)
def _dft_constants(r):
    h = r // 2
    idx = np.arange(r, dtype=np.float64)
    ang = 2.0 * np.pi * np.outer(idx, idx) / r
    c, s = np.cos(ang), np.sin(ang)
    f1 = np.block([[c[:, :h], s[:, :h]], [-s[:, :h], c[:, :h]]])
    f1_real = np.concatenate([c, -s], axis=0)
    f2 = np.block([[c, s], [-s, c]])
    f2_inv = np.block([[c, -s], [s, c]])
    f1_inv = np.block([[c[:h, :], -s[:h, :]], [s[:h, :], c[:h, :]]])
    n = r * r
    coarse = np.arange(r // TWIDDLE_RADIX, dtype=np.float64)[:, None] * TWIDDLE_RADIX
    fine = np.arange(TWIDDLE_RADIX, dtype=np.float64)[:, None]

    def table(mult):
        a = 2.0 * np.pi * mult * idx[None, :] / n
        t = np.stack([np.cos(a), np.sin(a)], axis=1)
        return np.ascontiguousarray(np.broadcast_to(t[..., None], t.shape + (LANES,))).astype(np.float32)

    return dict(f1=_split_hi_lo_np(f1), f1_real=_split_hi_lo_np(f1_real), f2=_split_hi_lo_np(f2),
                f2_inv=_split_hi_lo_np(f2_inv), f1_inv=_split_hi_lo_np(f1_inv),
                t1=table(coarse), t2=table(fine))


def _dot3(m_ref, data):
    d_hi = data.astype(BF16)
    d_lo = (data - d_hi.astype(F32)).astype(BF16)
    m_hi, m_lo = m_ref[0], m_ref[1]
    return (jnp.dot(m_hi, d_hi, preferred_element_type=F32)
            + jnp.dot(m_hi, d_lo, preferred_element_type=F32)
            + jnp.dot(m_lo, d_hi, preferred_element_type=F32))


def _twiddle(t1_ref, t2_ref, idx):
    a = idx // TWIDDLE_RADIX
    b = idx % TWIDDLE_RADIX
    c1, s1 = t1_ref[a, 0], t1_ref[a, 1]
    c2, s2 = t2_ref[b, 0], t2_ref[b, 1]
    return c1 * c2 - s1 * s2, s1 * c2 + c1 * s2


def _lanes(x, half):
    return x[:, half * LANES:(half + 1) * LANES]


def _blocks_to_rows(src_ref, dst_ref, r):
    chunk = src_ref.shape[1]

    def body(g, carry):
        dst_ref[pl.ds(pl.multiple_of(g * chunk, 8), chunk), :] = src_ref[g]
        return carry

    lax.fori_loop(0, r, body, 0)


def _rows_to_blocks(src_ref, dst_ref, r):
    chunk = dst_ref.shape[1]

    def body(g, carry):
        dst_ref[g] = src_ref[pl.ds(pl.multiple_of(g * chunk, 8), chunk), :]
        return carry

    lax.fori_loop(0, r, body, 0)


def _stage_a(x_ref, x2d, chunk, first_n2, f1_ref, t1_ref, t2_ref, sre_ref, sim_ref, r):
    pitch = r + PITCH_PAD
    _blocks_to_rows(x_ref, x2d, r)

    def body(jj, carry):
        n2l = 2 * jj
        rhs = jnp.concatenate([x2d[pl.ds(n2l, r, stride=chunk), :],
                               x2d[pl.ds(n2l + 1, r, stride=chunk), :]], axis=1)
        out = _dot3(f1_ref, rhs)
        for half in range(2):
            re, im = _lanes(out[:r], half), _lanes(out[r:], half)
            n2 = first_n2 + n2l + half
            c, s = _twiddle(t1_ref, t2_ref, n2)
            rows = pl.ds(n2, r, stride=pitch)
            sre_ref[rows, :] = re * c + im * s
            sim_ref[rows, :] = im * c - re * s
        return carry

    lax.fori_loop(0, chunk // 2, body, 0)


def _load_k1_pair(sre_ref, sim_ref, k1, r):
    pitch = r + PITCH_PAD
    r0 = pl.multiple_of(k1 * pitch, 8)
    r1 = pl.multiple_of((k1 + 1) * pitch, 8)
    rows = (pl.ds(r0, r), pl.ds(r1, r))
    rhs = jnp.concatenate([
        jnp.concatenate([sre_ref[rows[0], :], sre_ref[rows[1], :]], axis=1),
        jnp.concatenate([sim_ref[rows[0], :], sim_ref[rows[1], :]], axis=1)], axis=0)
    return rhs, rows


def _fft_conv_kernel(x_ref, v_ref, gate_ref, cs_ref, skip_ref, f1_ref, f1i_ref, f2_ref, f2i_ref, t1_ref, t2_ref,
                     o_ref, sre_ref, sim_ref, x2d, v2d, g2d, o2d, *, r, ca, cb, cd):
    pitch = r + PITCH_PAD
    n_a, n_bc = r // ca, r // cb
    step = pl.program_id(2)

    @pl.when(step < n_a)
    def _():
        _stage_a(x_ref, x2d, ca, step * ca, f1_ref, t1_ref, t2_ref, sre_ref, sim_ref, r)

    @pl.when((step >= n_a) & (step < n_a + n_bc))
    def _():
        first_k1 = (step - n_a) * cb

        def body(jj, carry):
            k1l = 2 * jj
            k1 = first_k1 + k1l
            rhs, rows = _load_k1_pair(sre_ref, sim_ref, k1, r)
            x = _dot3(f2_ref, rhs)
            p_re, p_im = [], []
            for half in range(2):
                xr, xi = _lanes(x[:r], half), _lanes(x[r:], half)
                crow = pl.ds(pl.multiple_of((k1l + half) * r, 8), r)
                cr, ci = cs_ref[0, crow, :], cs_ref[1, crow, :]
                p_re.append(xr * cr - xi * ci)
                p_im.append(xr * ci + xi * cr)
            prod = jnp.concatenate([jnp.concatenate(p_re, axis=1), jnp.concatenate(p_im, axis=1)], axis=0)
            y = _dot3(f2i_ref, prod)
            for half in range(2):
                re, im = _lanes(y[:r], half), _lanes(y[r:], half)
                c, s = _twiddle(t1_ref, t2_ref, k1 + half)
                sre_ref[rows[half], :] = re * c - im * s
                sim_ref[rows[half], :] = im * c + re * s
            return carry

        lax.fori_loop(0, cb // 2, body, 0)

    @pl.when(step >= n_a + n_bc)
    def _():
        first_n2 = (step - n_a - n_bc) * cd
        _blocks_to_rows(v_ref, v2d, r)
        _blocks_to_rows(gate_ref, g2d, r)
        skip = skip_ref[...]

        def body(jj, carry):
            n2l = 2 * jj
            n2 = first_n2 + n2l
            rows0 = pl.ds(n2, r, stride=pitch)
            rows1 = pl.ds(n2 + 1, r, stride=pitch)
            rhs = jnp.concatenate([
                jnp.concatenate([sre_ref[rows0, :], sre_ref[rows1, :]], axis=1),
                jnp.concatenate([sim_ref[rows0, :], sim_ref[rows1, :]], axis=1)], axis=0)
            y = _dot3(f1i_ref, rhs)
            for half in range(2):
                tok = pl.ds(n2l + half, r, stride=cd)
                o2d[tok, :] = g2d[tok, :] * (_lanes(y, half) + v2d[tok, :] * skip)
            return carry

        lax.fori_loop(0, cd // 2, body, 0)
        _rows_to_blocks(o2d, o_ref, r)


def _fft_conv(x, x_group0, x_col0, gate, gate_group0, gate_col0, spectrum, skip, r, n_pairs, dm,
              *, ca=16, cb=16, cd=16):
    n_a, n_bc, n_d = r // ca, r // cb, r // cd
    n_slabs = dm // LANES
    consts = _dft_constants(r)
    xv = x.reshape(x.shape[0] // r, r, x.shape[1])
    gv = gate.reshape(gate.shape[0] // r, r, gate.shape[1])
    x_blk0, g_blk0 = x_group0 // r, gate_group0 // r

    def d_step(st):
        return jnp.clip(st - n_a - n_bc, 0, n_d - 1)

    def full(a):
        return pl.BlockSpec(a.shape, lambda s, p, st, nd=a.ndim: (0,) * nd)

    const_args = [consts["f1"], consts["f1_inv"], consts["f2"], consts["f2_inv"], consts["t1"], consts["t2"]]
    out = pl.pallas_call(
        functools.partial(_fft_conv_kernel, r=r, ca=ca, cb=cb, cd=cd),
        grid=(n_slabs, n_pairs, n_a + n_bc + n_d),
        in_specs=[
            pl.BlockSpec((r, ca, LANES), lambda s, p, st: (x_blk0 + p, jnp.minimum(st, n_a - 1), x_col0 + s)),
            pl.BlockSpec((r, cd, LANES), lambda s, p, st: (x_blk0 + p, d_step(st), x_col0 + s)),
            pl.BlockSpec((r, cd, LANES), lambda s, p, st: (g_blk0 + p, d_step(st), gate_col0 + s)),
            pl.BlockSpec((2, cb * r, LANES), lambda s, p, st: (0, jnp.clip(st - n_a, 0, n_bc - 1), s)),
            pl.BlockSpec((1, LANES), lambda s, p, st: (0, s)),
        ] + [full(a) for a in const_args],
        out_specs=pl.BlockSpec((r, cd, LANES), lambda s, p, st: (p, d_step(st), s)),
        out_shape=jax.ShapeDtypeStruct((n_pairs * r, r, dm), F32),
        scratch_shapes=[pltpu.VMEM((r * (r + PITCH_PAD), LANES), F32),
                        pltpu.VMEM((r * (r + PITCH_PAD), LANES), F32),
                        pltpu.VMEM((r * ca, LANES), F32),
                        pltpu.VMEM((r * cd, LANES), F32),
                        pltpu.VMEM((r * cd, LANES), F32),
                        pltpu.VMEM((r * cd, LANES), F32)],
        compiler_params=pltpu.CompilerParams(
            dimension_semantics=("parallel", "arbitrary", "arbitrary"),
            vmem_limit_bytes=VMEM_LIMIT_BYTES),
        name="fft_conv",
    )(xv, xv, gv, spectrum, skip.reshape(1, dm), *const_args)
    return out.reshape(n_pairs * r * r, dm)


def _fft_spectrum_kernel(c_ref, f1_ref, f2_ref, t1_ref, t2_ref, o_ref, sre_ref, sim_ref, c2d, *, r, ca, cb, scale):
    n_a = r // ca
    step = pl.program_id(2)

    @pl.when(step < n_a)
    def _():
        _stage_a(c_ref, c2d, ca, step * ca, f1_ref, t1_ref, t2_ref, sre_ref, sim_ref, r)

    @pl.when(step >= n_a)
    def _():
        first_k1 = (step - n_a) * cb

        def body(jj, carry):
            k1l = 2 * jj
            rhs, _ = _load_k1_pair(sre_ref, sim_ref, first_k1 + k1l, r)
            x = _dot3(f2_ref, rhs) * scale
            for half in range(2):
                orow = pl.ds(pl.multiple_of((k1l + half) * r, 8), r)
                o_ref[0, 0, orow, :] = _lanes(x[:r], half)
                o_ref[0, 1, orow, :] = _lanes(x[r:], half)
            return carry

        lax.fori_loop(0, cb // 2, body, 0)


def _fft_spectrum(c, r, *, ca=16, cb=16):
    orders, n, dm = c.shape
    n_a, n_bc = r // ca, r // cb
    consts = _dft_constants(r)
    cv = c.reshape(orders * r, r, dm)

    def full(a):
        return pl.BlockSpec(a.shape, lambda s, o, st, nd=a.ndim: (0,) * nd)

    const_args = [consts["f1_real"], consts["f2"], consts["t1"], consts["t2"]]
    return pl.pallas_call(
        functools.partial(_fft_spectrum_kernel, r=r, ca=ca, cb=cb, scale=1.0 / n),
        grid=(dm // LANES, orders, n_a + n_bc),
        in_specs=[pl.BlockSpec((r, ca, LANES), lambda s, o, st: (o, jnp.minimum(st, n_a - 1), s))]
        + [full(a) for a in const_args],
        out_specs=pl.BlockSpec((1, 2, cb * r, LANES), lambda s, o, st: (o, 0, jnp.maximum(st - n_a, 0), s)),
        out_shape=jax.ShapeDtypeStruct((orders, 2, n, dm), F32),
        scratch_shapes=[pltpu.VMEM((r * (r + PITCH_PAD), LANES), F32),
                        pltpu.VMEM((r * (r + PITCH_PAD), LANES), F32),
                        pltpu.VMEM((r * ca, LANES), F32)],
        compiler_params=pltpu.CompilerParams(
            dimension_semantics=("parallel", "arbitrary", "arbitrary"),
            vmem_limit_bytes=VMEM_LIMIT_BYTES),
        name="fft_spectrum",
    )(cv, *const_args)


def _hyena_filters(length, f_w1, f_b1, f_w2, f_b2, f_freq, f_w3):
    hp = lax.Precision.HIGHEST
    t = jnp.linspace(0.0, 1.0, length, dtype=F32)[:, None]
    w = 2.0 * math.pi * jnp.arange(length, dtype=F32)[:, None] / length
    bands = jnp.linspace(1e-4, FILTER_BANDS - 1, FILTER_BANDS, dtype=F32)[None, :]
    pos = jnp.concatenate([t, jnp.cos(bands * w), -jnp.sin(bands * w)], axis=-1)
    h = jnp.sin(f_freq * (jnp.dot(pos, f_w1, precision=hp) + f_b1))
    h = jnp.sin(f_freq * (jnp.dot(h, f_w2, precision=hp) + f_b2))
    h = jnp.dot(h, f_w3, precision=hp).reshape(length, HYENA_ORDER, 2, D_MODEL)
    deltas = jnp.abs(jnp.linspace(MIN_DECAY, MAX_DECAY, D_MODEL, dtype=F32))
    h = h * jnp.exp(-t * deltas)[:, None, None, :]
    return h / jnp.sum(jnp.abs(h), axis=(0, 2), keepdims=True)


def _filter_spectra(length, r, filt_params):
    h = _hyena_filters(length, *filt_params)
    h_fwd = jnp.moveaxis(h[:, :, 0], 1, 0)
    h_bwd = jnp.moveaxis(h[:, :, 1], 1, 0)
    c = jnp.concatenate([h_fwd.at[:, 0].add(h_bwd[:, 0]), jnp.zeros_like(h_fwd[:, :1]),
                         h_bwd[:, :0:-1]], axis=1)
    return _fft_spectrum(c, r)


def _hyena_mixer(u, filt_params, skip, seq_tokens, batches):
    n_prompt, s_prompt, s_sample = seq_tokens
    dm = u.shape[1] // 3
    slabs = dm // LANES
    parts = []
    for seq_len, n_batch, row0 in ((s_prompt, batches[0], 0), (s_sample, batches[1], n_prompt)):
        r = math.isqrt(2 * seq_len)
        assert r * r == 2 * seq_len and n_batch % 2 == 0
        spec = _filter_spectra(seq_len, r, filt_params)
        z1 = _fft_conv(u, row0 // r, 0, u, row0 // r, slabs, spec[0], skip[0], r, n_batch // 2, dm)
        z2 = _fft_conv(z1, 0, 0, u, row0 // r, 2 * slabs, spec[1], skip[1], r, n_batch // 2, dm)
        parts.append(z2)
    return jnp.concatenate(parts, axis=0)


def kernel(x_prompt, x_sample, ln_g, ln_b, ffn_w_in, ffn_w_out, attn_w_qkv, attn_w_o, hy_w_in, hy_b_in,
           hy_conv_w, hy_conv_b, hy_f_w1, hy_f_b1, hy_f_w2, hy_f_b2, hy_f_freq, hy_f_w3, hy_skip,
           hy_w_out, hy_b_out):
    bp, sp, dm = x_prompt.shape
    bs, ss, _ = x_sample.shape
    n_prompt = bp * sp
    n_sample = bs * ss
    seq_tokens = (n_prompt, sp, ss)
    x = jnp.concatenate([x_prompt.reshape(n_prompt, dm), x_sample.reshape(n_sample, dm)], axis=0)
    slopes = jnp.exp2(-8.0 * jnp.arange(1, N_SLOTS + 1, dtype=F32) / N_SLOTS)
    zero_bias_qkv = jnp.zeros((QKV_WIDTH,), F32)
    zero_bias_dm = jnp.zeros((dm,), F32)

    for i in range(DEPTH):
        x = _ffn_layer(x, ffn_w_in[i, 0].astype(BF16), ffn_w_out[i, 0].astype(BF16), ln_g[i, 0], ln_b[i, 0])
        j = i // 2
        if i % 2 == 0:
            qkv = _projection(x, attn_w_qkv[j].astype(BF16), zero_bias_qkv, BF16)
            att = _dilated_attention(qkv, slopes, seq_tokens)
            x = _projection_ln(x, att, attn_w_o[j].astype(BF16), zero_bias_dm, ln_g[i, 1], ln_b[i, 1])
        else:
            u = _projection_short_conv(x, hy_w_in[j].astype(BF16), hy_b_in[j], hy_conv_w[j], hy_conv_b[j],
                                       seq_tokens)
            filt = (hy_f_w1[j], hy_f_b1[j], hy_f_w2[j], hy_f_b2[j], hy_f_freq[j], hy_f_w3[j])
            z = _hyena_mixer(u, filt, hy_skip[j], seq_tokens, (bp, bs))
            x = _projection_ln(x, z, hy_w_out[j].astype(BF16), hy_b_out[j], ln_g[i, 1], ln_b[i, 1])
        x = _ffn_layer(x, ffn_w_in[i, 1].astype(BF16), ffn_w_out[i, 1].astype(BF16), ln_g[i, 2], ln_b[i, 2])

    return (x[:n_prompt].reshape(bp, sp, dm), x[n_prompt:].reshape(bs, ss, dm))
```

```python
import functools
import math

import jax
import jax.numpy as jnp
import ml_dtypes
import numpy as np
from jax import lax
from jax.experimental import pallas as pl
from jax.experimental.pallas import tpu as pltpu

D_MODEL = 2048
DEPTH = 4
HEAD_DIM = 128
N_SLOTS = D_MODEL // HEAD_DIM
DILATED_GROUPS = ((128, 1), (512, 4), (2048, 16))
N_GROUPS = len(DILATED_GROUPS)
ATTN_WIDTH = N_SLOTS * HEAD_DIM
QKV_WIDTH = 3 * N_GROUPS * ATTN_WIDTH
HALF_WINDOW = 64
NEG_INF = -1e30
D_FF = 5632
HYENA_ORDER = 2
SHORT_CONV = 3
FILTER_BANDS = 16
DECAY_TARGET = 1e-2
DECAY_FAST = 0.3
DECAY_SLOW = 1.5
MAX_DECAY = math.log(DECAY_TARGET) / DECAY_FAST
MIN_DECAY = math.log(DECAY_TARGET) / DECAY_SLOW
DN_ALPHA = (2.0 * DEPTH) ** 0.25
LN_EPS = 1e-5

LANES = 128
ATTN_BLOCKS = 4
FFT_BLOCKS = 4
COPY_UNROLL = 8
VMEM_LIMIT_BYTES = 48 * 1024 * 1024
BF16 = jnp.bfloat16
F32 = jnp.float32


def _layer_norm_rows(y, g, b):
    mu = jnp.mean(y, axis=-1, keepdims=True)
    c = y - mu
    var = jnp.mean(c * c, axis=-1, keepdims=True)
    return c * lax.rsqrt(var + LN_EPS) * g + b


def _seq_position(t0, seq_tokens):
    n_prompt, s_prompt, s_sample = seq_tokens
    seq_len = jnp.where(t0 < n_prompt, s_prompt, s_sample)
    return lax.rem(t0, seq_len), seq_len


def _ffn_kernel(x_ref, wg_ref, wu_ref, wo_ref, g_ref, b_ref, o_ref, xb_ref, *, nk):
    k = pl.program_id(1)

    @pl.when(k == 0)
    def _():
        xb_ref[...] = x_ref[...].astype(BF16)
        o_ref[...] = jnp.zeros_like(o_ref)

    xb = xb_ref[...]
    gate = jnp.dot(xb, wg_ref[...], preferred_element_type=F32)
    up = jnp.dot(xb, wu_ref[...], preferred_element_type=F32)
    act = (gate * jax.nn.sigmoid(gate) * up).astype(BF16)
    o_ref[...] += jnp.dot(act, wo_ref[...], preferred_element_type=F32)

    @pl.when(k == nk - 1)
    def _():
        y = DN_ALPHA * x_ref[...] + 0.5 * o_ref[...]
        o_ref[...] = _layer_norm_rows(y, g_ref[...], b_ref[...])


def _ffn_layer(x, w_in, w_out, g, b, *, tm=512, tf=512):
    t, dm = x.shape
    dff = w_out.shape[0]
    nk = dff // tf
    return pl.pallas_call(
        functools.partial(_ffn_kernel, nk=nk),
        grid=(t // tm, nk),
        in_specs=[
            pl.BlockSpec((tm, dm), lambda i, k: (i, 0)),
            pl.BlockSpec((dm, tf), lambda i, k: (0, k)),
            pl.BlockSpec((dm, tf), lambda i, k: (0, k + nk)),
            pl.BlockSpec((tf, dm), lambda i, k: (k, 0)),
            pl.BlockSpec((1, dm), lambda i, k: (0, 0)),
            pl.BlockSpec((1, dm), lambda i, k: (0, 0)),
        ],
        out_specs=pl.BlockSpec((tm, dm), lambda i, k: (i, 0)),
        out_shape=jax.ShapeDtypeStruct((t, dm), F32),
        scratch_shapes=[pltpu.VMEM((tm, dm), BF16)],
        compiler_params=pltpu.CompilerParams(
            dimension_semantics=("parallel", "arbitrary"),
            vmem_limit_bytes=VMEM_LIMIT_BYTES),
        name="ffn_ln",
    )(x, w_in, w_in, w_out, g.reshape(1, dm), b.reshape(1, dm))


def _proj_kernel(x_ref, w_ref, b_ref, o_ref, xb_ref):
    @pl.when(pl.program_id(1) == 0)
    def _():
        xb_ref[...] = x_ref[...].astype(BF16)

    acc = jnp.dot(xb_ref[...], w_ref[...], preferred_element_type=F32)
    o_ref[...] = (acc + b_ref[...]).astype(o_ref.dtype)


def _projection(x, w, bias, out_dtype, *, tm=1024, tn=1024):
    t, dm = x.shape
    n = w.shape[1]
    return pl.pallas_call(
        _proj_kernel,
        grid=(t // tm, n // tn),
        in_specs=[
            pl.BlockSpec((tm, dm), lambda i, j: (i, 0)),
            pl.BlockSpec((dm, tn), lambda i, j: (0, j)),
            pl.BlockSpec((1, tn), lambda i, j: (0, j)),
        ],
        out_specs=pl.BlockSpec((tm, tn), lambda i, j: (i, j)),
        out_shape=jax.ShapeDtypeStruct((t, n), out_dtype),
        scratch_shapes=[pltpu.VMEM((tm, dm), BF16)],
        compiler_params=pltpu.CompilerParams(
            dimension_semantics=("parallel", "arbitrary"),
            vmem_limit_bytes=VMEM_LIMIT_BYTES),
        name="projection",
    )(x, w, bias.reshape(1, n))


HALO_ROWS = 16


def _proj_conv_kernel(xp_ref, x_ref, xn_ref, w_ref, b_ref, cw_ref, cb_ref, o_ref, xb_ref, *, tm, seq_tokens):
    @pl.when(pl.program_id(1) == 0)
    def _():
        xb_ref[0:HALO_ROWS, :] = xp_ref[...].astype(BF16)
        xb_ref[HALO_ROWS:HALO_ROWS + tm, :] = x_ref[...].astype(BF16)
        xb_ref[HALO_ROWS + tm:, :] = xn_ref[...].astype(BF16)

    rows = tm + 2 * HALO_ROWS
    acc = jnp.dot(xb_ref[...], w_ref[...], preferred_element_type=F32) + b_ref[...]
    pos0, seq_len = _seq_position(pl.program_id(0) * tm, seq_tokens)
    tile_row = lax.broadcasted_iota(jnp.int32, (tm, 1), 0)
    at_seq_start = (tile_row == 0) & (pos0 == 0)
    at_seq_end = (tile_row == tm - 1) & (pos0 + tm == seq_len)
    cur = acc[HALO_ROWS:HALO_ROWS + tm]
    prev = pltpu.roll(acc, 1, axis=0)[HALO_ROWS:HALO_ROWS + tm]
    nxt = pltpu.roll(acc, rows - 1, axis=0)[HALO_ROWS:HALO_ROWS + tm]
    prev = jnp.where(at_seq_start, 0.0, prev)
    nxt = jnp.where(at_seq_end, 0.0, nxt)
    o_ref[...] = prev * cw_ref[0:1, :] + cur * cw_ref[1:2, :] + nxt * cw_ref[2:3, :] + cb_ref[...]


def _projection_short_conv(x, w, bias, conv_w, conv_b, seq_tokens, *, tm=1024, tn=1024):
    t, dm = x.shape
    n = w.shape[1]
    ratio = tm // HALO_ROWS
    n_halo_blocks = t // HALO_ROWS
    return pl.pallas_call(
        functools.partial(_proj_conv_kernel, tm=tm, seq_tokens=seq_tokens),
        grid=(t // tm, n // tn),
        in_specs=[
            pl.BlockSpec((HALO_ROWS, dm), lambda i, j: (jnp.maximum(i * ratio - 1, 0), 0)),
            pl.BlockSpec((tm, dm), lambda i, j: (i, 0)),
            pl.BlockSpec((HALO_ROWS, dm), lambda i, j: (jnp.minimum((i + 1) * ratio, n_halo_blocks - 1), 0)),
            pl.BlockSpec((dm, tn), lambda i, j: (0, j)),
            pl.BlockSpec((1, tn), lambda i, j: (0, j)),
            pl.BlockSpec((SHORT_CONV, tn), lambda i, j: (0, j)),
            pl.BlockSpec((1, tn), lambda i, j: (0, j)),
        ],
        out_specs=pl.BlockSpec((tm, tn), lambda i, j: (i, j)),
        out_shape=jax.ShapeDtypeStruct((t, n), F32),
        scratch_shapes=[pltpu.VMEM((tm + 2 * HALO_ROWS, dm), BF16)],
        compiler_params=pltpu.CompilerParams(
            dimension_semantics=("parallel", "arbitrary"),
            vmem_limit_bytes=VMEM_LIMIT_BYTES),
        name="projection_short_conv",
    )(x, x, x, w, bias.reshape(1, n), conv_w, conv_b.reshape(1, n))


def _proj_ln_kernel(x_ref, z_ref, w_ref, bias_ref, g_ref, b_ref, o_ref):
    mix = jnp.dot(z_ref[...].astype(BF16), w_ref[...], preferred_element_type=F32)
    y = DN_ALPHA * x_ref[...] + (mix + bias_ref[...])
    o_ref[...] = _layer_norm_rows(y, g_ref[...], b_ref[...])


def _projection_ln(x, z, w, bias, g, b, *, tm=512):
    t, dm = x.shape
    kdim = z.shape[1]
    return pl.pallas_call(
        _proj_ln_kernel,
        grid=(t // tm,),
        in_specs=[
            pl.BlockSpec((tm, dm), lambda i: (i, 0)),
            pl.BlockSpec((tm, kdim), lambda i: (i, 0)),
            pl.BlockSpec((kdim, dm), lambda i: (0, 0)),
            pl.BlockSpec((1, dm), lambda i: (0, 0)),
            pl.BlockSpec((1, dm), lambda i: (0, 0)),
            pl.BlockSpec((1, dm), lambda i: (0, 0)),
        ],
        out_specs=pl.BlockSpec((tm, dm), lambda i: (i, 0)),
        out_shape=jax.ShapeDtypeStruct((t, dm), F32),
        compiler_params=pltpu.CompilerParams(
            dimension_semantics=("parallel",),
            vmem_limit_bytes=VMEM_LIMIT_BYTES),
        name="projection_ln",
    )(x, z, w, bias.reshape(1, dm), g.reshape(1, dm), b.reshape(1, dm))


def _attn_kernel(slopes_ref, *refs, tq, hg, seq_tokens):
    n_in = 7 * N_GROUPS
    in_refs = refs[:n_in]
    o_ref = refs[n_in]
    qf_ref, kf_ref, vf_ref, og_ref, lg_ref = refs[n_in + 1:]

    pos0, seq_len = _seq_position(pl.program_id(0) * tq, seq_tokens)
    scale = HEAD_DIM ** -0.5

    for hh in range(hg):
        cols = slice(hh * HEAD_DIM, (hh + 1) * HEAD_DIM)
        slope = slopes_ref[pl.program_id(1) * hg + hh]
        for g, (_, d) in enumerate(DILATED_GROUPS):
            q_ref, kp_ref, kc_ref, kn_ref, vp_ref, vc_ref, vn_ref = in_refs[7 * g:7 * g + 7]
            halo = HALF_WINDOW * d
            nq = tq // d
            sb = min(nq, 128)
            nkeys = sb + 2 * HALF_WINDOW
            chain_len = seq_len // d
            cpos0 = pos0 // d

            qf_ref[0:tq, :] = q_ref[:, cols].astype(F32)
            kf_ref[0:halo, :] = kp_ref[:, cols].astype(F32)
            kf_ref[halo:halo + tq, :] = kc_ref[:, cols].astype(F32)
            kf_ref[halo + tq:2 * halo + tq, :] = kn_ref[:, cols].astype(F32)
            vf_ref[0:halo, :] = vp_ref[:, cols].astype(F32)
            vf_ref[halo:halo + tq, :] = vc_ref[:, cols].astype(F32)
            vf_ref[halo + tq:2 * halo + tq, :] = vn_ref[:, cols].astype(F32)

            qi = lax.broadcasted_iota(jnp.int32, (sb, nkeys), 0)
            ki = lax.broadcasted_iota(jnp.int32, (sb, nkeys), 1)
            rel = ki - HALF_WINDOW - qi
            in_band = jnp.abs(rel) <= HALF_WINDOW
            alibi = (-slope) * (jnp.abs(rel) * d).astype(F32)

            def chain_blocks(it, carry, d=d, sb=sb, nkeys=nkeys, nq=nq, g=g,
                             in_band=in_band, alibi=alibi, ki=ki,
                             chain_len=chain_len, cpos0=cpos0):
                rows, scores = [], []
                for j in range(ATTN_BLOCKS):
                    idx = it * ATTN_BLOCKS + j
                    r = idx // (nq // sb)
                    s = idx % (nq // sb)
                    start = r + s * (sb * d)
                    if d == 1:
                        q_rows, k_rows = pl.ds(start, sb), pl.ds(start, nkeys)
                    else:
                        q_rows, k_rows = pl.ds(start, sb, stride=d), pl.ds(start, nkeys, stride=d)
                    qc = qf_ref[q_rows, :].astype(BF16)
                    kc = kf_ref[k_rows, :].astype(BF16)
                    sc = lax.dot_general(qc, kc, (((1,), (1,)), ((), ())), preferred_element_type=F32)
                    rows.append((q_rows, k_rows, s))
                    scores.append(sc)
                probs = []
                for (q_rows, k_rows, s), sc in zip(rows, scores):
                    kpos0 = cpos0 + s * sb - HALF_WINDOW
                    valid = in_band & (ki >= -kpos0) & (ki < chain_len - kpos0)
                    sc = jnp.where(valid, sc * scale + alibi, NEG_INF)
                    mx = jnp.max(sc, axis=-1, keepdims=True)
                    p = jnp.exp(sc - mx)
                    den = jnp.sum(p, axis=-1, keepdims=True)
                    probs.append((p.astype(BF16), mx, den))
                outs = [jnp.dot(p, vf_ref[k_rows, :].astype(BF16), preferred_element_type=F32)
                        for (p, _, _), (_, k_rows, _) in zip(probs, rows)]
                for (q_rows, _, _), (_, mx, den), o in zip(rows, probs, outs):
                    og_ref[g, q_rows, :] = o / den
                    lg_ref[g, q_rows, :] = jnp.broadcast_to(mx + jnp.log(den), (sb, HEAD_DIM))
                return carry

            lax.fori_loop(0, d * (nq // sb) // ATTN_BLOCKS, chain_blocks, 0)

        l0, l1, l2 = lg_ref[0], lg_ref[1], lg_ref[2]
        lmax = jnp.maximum(jnp.maximum(l0, l1), l2)
        w0, w1, w2 = jnp.exp(l0 - lmax), jnp.exp(l1 - lmax), jnp.exp(l2 - lmax)
        mixed = (w0 * og_ref[0] + w1 * og_ref[1] + w2 * og_ref[2]) / (w0 + w1 + w2)
        o_ref[:, cols] = mixed.astype(o_ref.dtype)


def _dilated_attention(qkv, slopes, seq_tokens, *, tq=1024, hg=4):
    t = qkv.shape[0]
    wcols = hg * HEAD_DIM
    blocks_per_part = ATTN_WIDTH // wcols
    in_specs = []
    args = []
    max_halo = HALF_WINDOW * max(d for _, d in DILATED_GROUPS)
    for g, (_, d) in enumerate(DILATED_GROUPS):
        halo = HALF_WINDOW * d
        ratio = tq // halo
        n_halo_blocks = t // halo

        def col(part, g=g):
            return lambda i, h, *_: (part * N_GROUPS + g) * blocks_per_part + h

        def cur(part, g=g):
            c = col(part, g)
            return pl.BlockSpec((tq, wcols), lambda i, h, *_: (i, c(i, h)))

        def prev(part, g=g, ratio=ratio):
            c = col(part, g)
            return pl.BlockSpec((halo, wcols),
                                lambda i, h, *_: (jnp.maximum(i * ratio - 1, 0), c(i, h)))

        def nxt(part, g=g, ratio=ratio, n_halo_blocks=n_halo_blocks):
            c = col(part, g)
            return pl.BlockSpec((halo, wcols),
                                lambda i, h, *_: (jnp.minimum((i + 1) * ratio, n_halo_blocks - 1), c(i, h)))

        in_specs += [cur(0), prev(1), cur(1), nxt(1), prev(2), cur(2), nxt(2)]
        args += [qkv] * 7

    grid_spec = pltpu.PrefetchScalarGridSpec(
        num_scalar_prefetch=1,
        grid=(t // tq, ATTN_WIDTH // wcols),
        in_specs=in_specs,
        out_specs=pl.BlockSpec((tq, wcols), lambda i, h, *_: (i, h)),
        scratch_shapes=[
            pltpu.VMEM((tq, HEAD_DIM), F32),
            pltpu.VMEM((tq + 2 * max_halo, HEAD_DIM), F32),
            pltpu.VMEM((tq + 2 * max_halo, HEAD_DIM), F32),
            pltpu.VMEM((N_GROUPS, tq, HEAD_DIM), F32),
            pltpu.VMEM((N_GROUPS, tq, HEAD_DIM), F32),
        ],
    )
    return pl.pallas_call(
        functools.partial(_attn_kernel, tq=tq, hg=hg, seq_tokens=seq_tokens),
        grid_spec=grid_spec,
        out_shape=jax.ShapeDtypeStruct((t, ATTN_WIDTH), BF16),
        compiler_params=pltpu.CompilerParams(
            dimension_semantics=("parallel", "arbitrary"),
            vmem_limit_bytes=VMEM_LIMIT_BYTES),
        name="dilated_attention",
    )(slopes, *args)


TWIDDLE_RADIX = 16
PITCH_PAD = 8


def _split_hi_lo_np(m):
    hi = m.astype(ml_dtypes.bfloat16)
    lo = (m - hi.astype(np.float64)).astype(ml_dtypes.bfloat16)
    return np.stack([hi, lo])


@functools.lru_cache(maxsize=None)
def _dft_constants(r):
    h = r // 2
    idx = np.arange(r, dtype=np.float64)
    ang = 2.0 * np.pi * np.outer(idx, idx) / r
    c, s = np.cos(ang), np.sin(ang)
    f1 = np.block([[c[:, :h], s[:, :h]], [-s[:, :h], c[:, :h]]])
    f1_real = np.concatenate([c, -s], axis=0)
    f2 = np.block([[c, s], [-s, c]])
    f2_inv = np.block([[c, -s], [s, c]])
    f1_inv = np.block([[c[:h, :], -s[:h, :]], [s[:h, :], c[:h, :]]])
    n = r * r
    coarse = np.arange(r // TWIDDLE_RADIX, dtype=np.float64)[:, None] * TWIDDLE_RADIX
    fine = np.arange(TWIDDLE_RADIX, dtype=np.float64)[:, None]

    def table(mult):
        a = 2.0 * np.pi * mult * idx[None, :] / n
        t = np.stack([np.cos(a), np.sin(a)], axis=1)
        return np.ascontiguousarray(np.broadcast_to(t[..., None], t.shape + (LANES,))).astype(np.float32)

    return dict(f1=_split_hi_lo_np(f1), f1_real=_split_hi_lo_np(f1_real), f2=_split_hi_lo_np(f2),
                f2_inv=_split_hi_lo_np(f2_inv), f1_inv=_split_hi_lo_np(f1_inv),
                t1=table(coarse), t2=table(fine))


def _split_hi_lo(data):
    d_hi = data.astype(BF16)
    return d_hi, (data - d_hi.astype(F32)).astype(BF16)


def _dot3(m_ref, d_hi, d_lo):
    m_hi, m_lo = m_ref[0], m_ref[1]
    return (jnp.dot(m_hi, d_hi, preferred_element_type=F32)
            + jnp.dot(m_hi, d_lo, preferred_element_type=F32)
            + jnp.dot(m_lo, d_hi, preferred_element_type=F32))


def _staged_loop(n_items, stages):
    def body(it, carry):
        state = [it * FFT_BLOCKS + j for j in range(FFT_BLOCKS)]
        for stage in stages:
            state = [stage(s) for s in state]
        return carry

    lax.fori_loop(0, n_items // FFT_BLOCKS, body, 0)


def _twiddle(t1_ref, t2_ref, idx):
    a = idx // TWIDDLE_RADIX
    b = idx % TWIDDLE_RADIX
    c1, s1 = t1_ref[a, 0], t1_ref[a, 1]
    c2, s2 = t2_ref[b, 0], t2_ref[b, 1]
    return c1 * c2 - s1 * s2, s1 * c2 + c1 * s2


def _lanes(x, half):
    return x[:, half * LANES:(half + 1) * LANES]


def _blocks_to_rows(src_ref, dst_ref, r):
    chunk = src_ref.shape[1]

    def body(g, carry):
        dst_ref[pl.ds(pl.multiple_of(g * chunk, 8), chunk), :] = src_ref[g]
        return carry

    lax.fori_loop(0, r, body, 0, unroll=COPY_UNROLL)


def _rows_to_blocks(src_ref, dst_ref, r):
    chunk = dst_ref.shape[1]

    def body(g, carry):
        dst_ref[g] = src_ref[pl.ds(pl.multiple_of(g * chunk, 8), chunk), :]
        return carry

    lax.fori_loop(0, r, body, 0, unroll=COPY_UNROLL)


def _stage_a(x_ref, x2d, chunk, first_n2, f1_ref, t1_ref, t2_ref, sre_ref, sim_ref, r):
    pitch = r + PITCH_PAD
    _blocks_to_rows(x_ref, x2d, r)

    def gather(jj):
        n2l = 2 * jj
        rhs = jnp.concatenate([x2d[pl.ds(n2l, r, stride=chunk), :],
                               x2d[pl.ds(n2l + 1, r, stride=chunk), :]], axis=1)
        return n2l, _split_hi_lo(rhs)

    def transform(state):
        n2l, (d_hi, d_lo) = state
        return n2l, _dot3(f1_ref, d_hi, d_lo)

    def twiddle_store(state):
        n2l, out = state
        for half in range(2):
            re, im = _lanes(out[:r], half), _lanes(out[r:], half)
            n2 = first_n2 + n2l + half
            c, s = _twiddle(t1_ref, t2_ref, n2)
            rows = pl.ds(n2, r, stride=pitch)
            sre_ref[rows, :] = re * c + im * s
            sim_ref[rows, :] = im * c - re * s

    _staged_loop(chunk // 2, [gather, transform, twiddle_store])


def _load_k1_pair(sre_ref, sim_ref, k1, r):
    pitch = r + PITCH_PAD
    r0 = pl.multiple_of(k1 * pitch, 8)
    r1 = pl.multiple_of((k1 + 1) * pitch, 8)
    rows = (pl.ds(r0, r), pl.ds(r1, r))
    rhs = jnp.concatenate([
        jnp.concatenate([sre_ref[rows[0], :], sre_ref[rows[1], :]], axis=1),
        jnp.concatenate([sim_ref[rows[0], :], sim_ref[rows[1], :]], axis=1)], axis=0)
    return rhs, rows


def _fft_conv_kernel(x_ref, v_ref, gate_ref, cs_ref, skip_ref, f1_ref, f1i_ref, f2_ref, f2i_ref, t1_ref, t2_ref,
                     o_ref, sre_ref, sim_ref, x2d, v2d, g2d, o2d, *, r, ca, cb, cd):
    pitch = r + PITCH_PAD
    n_a, n_bc = r // ca, r // cb
    step = pl.program_id(2)

    @pl.when(step < n_a)
    def _():
        _stage_a(x_ref, x2d, ca, step * ca, f1_ref, t1_ref, t2_ref, sre_ref, sim_ref, r)

    @pl.when((step >= n_a) & (step < n_a + n_bc))
    def _():
        first_k1 = (step - n_a) * cb

        def load(jj):
            k1l = 2 * jj
            rhs, rows = _load_k1_pair(sre_ref, sim_ref, first_k1 + k1l, r)
            return k1l, rows, _split_hi_lo(rhs)

        def forward(state):
            k1l, rows, (d_hi, d_lo) = state
            return k1l, rows, _dot3(f2_ref, d_hi, d_lo)

        def filter_multiply(state):
            k1l, rows, x = state
            p_re, p_im = [], []
            for half in range(2):
                xr, xi = _lanes(x[:r], half), _lanes(x[r:], half)
                crow = pl.ds(pl.multiple_of((k1l + half) * r, 8), r)
                cr, ci = cs_ref[0, crow, :], cs_ref[1, crow, :]
                p_re.append(xr * cr - xi * ci)
                p_im.append(xr * ci + xi * cr)
            prod = jnp.concatenate([jnp.concatenate(p_re, axis=1), jnp.concatenate(p_im, axis=1)], axis=0)
            return k1l, rows, _split_hi_lo(prod)

        def inverse(state):
            k1l, rows, (d_hi, d_lo) = state
            return k1l, rows, _dot3(f2i_ref, d_hi, d_lo)

        def twiddle_store(state):
            k1l, rows, y = state
            for half in range(2):
                re, im = _lanes(y[:r], half), _lanes(y[r:], half)
                c, s = _twiddle(t1_ref, t2_ref, first_k1 + k1l + half)
                sre_ref[rows[half], :] = re * c - im * s
                sim_ref[rows[half], :] = im * c + re * s

        _staged_loop(cb // 2, [load, forward, filter_multiply, inverse, twiddle_store])

    @pl.when(step >= n_a + n_bc)
    def _():
        first_n2 = (step - n_a - n_bc) * cd
        _blocks_to_rows(v_ref, v2d, r)
        _blocks_to_rows(gate_ref, g2d, r)
        skip = skip_ref[...]

        def gather(jj):
            n2l = 2 * jj
            n2 = first_n2 + n2l
            rows0 = pl.ds(n2, r, stride=pitch)
            rows1 = pl.ds(n2 + 1, r, stride=pitch)
            rhs = jnp.concatenate([
                jnp.concatenate([sre_ref[rows0, :], sre_ref[rows1, :]], axis=1),
                jnp.concatenate([sim_ref[rows0, :], sim_ref[rows1, :]], axis=1)], axis=0)
            return n2l, _split_hi_lo(rhs)

        def transform(state):
            n2l, (d_hi, d_lo) = state
            return n2l, _dot3(f1i_ref, d_hi, d_lo)

        def gate_store(state):
            n2l, y = state
            for half in range(2):
                tok = pl.ds(n2l + half, r, stride=cd)
                o2d[tok, :] = g2d[tok, :] * (_lanes(y, half) + v2d[tok, :] * skip)

        _staged_loop(cd // 2, [gather, transform, gate_store])
        _rows_to_blocks(o2d, o_ref, r)


def _fft_conv_kernel_aliased(*refs, **kw):
    _fft_conv_kernel(*refs[1:], **kw)


def _fft_conv(x, x_col0, gate, gate_col0, row0, spectrum, spec_col0, skip, r, n_pairs, dm, out_buf=None,
              *, ca=16, cb=16, cd=16):
    n_a, n_bc, n_d = r // ca, r // cb, r // cd
    n_slabs = dm // LANES
    consts = _dft_constants(r)
    t = x.shape[0]
    xv = x.reshape(t // r, r, x.shape[1])
    gv = gate.reshape(t // r, r, gate.shape[1])
    blk0 = row0 // (r * r)

    def d_step(st):
        return jnp.clip(st - n_a - n_bc, 0, n_d - 1)

    def full(a):
        return pl.BlockSpec(a.shape, lambda s, p, st, nd=a.ndim: (0,) * nd)

    const_args = [consts["f1"], consts["f1_inv"], consts["f2"], consts["f2_inv"], consts["t1"], consts["t2"]]
    in_specs = [
        pl.BlockSpec((r, ca, LANES), lambda s, p, st: (blk0 + p, jnp.minimum(st, n_a - 1), x_col0 + s)),
        pl.BlockSpec((r, cd, LANES), lambda s, p, st: (blk0 + p, d_step(st), x_col0 + s)),
        pl.BlockSpec((r, cd, LANES), lambda s, p, st: (blk0 + p, d_step(st), gate_col0 + s)),
        pl.BlockSpec((2, cb * r, LANES), lambda s, p, st: (0, jnp.clip(st - n_a, 0, n_bc - 1), spec_col0 + s)),
        pl.BlockSpec((1, LANES), lambda s, p, st: (0, s)),
    ] + [full(a) for a in const_args]
    args = [xv, xv, gv, spectrum, skip.reshape(1, dm), *const_args]
    body = _fft_conv_kernel
    aliases = {}
    if out_buf is not None:
        in_specs = [pl.BlockSpec(memory_space=pl.ANY)] + in_specs
        args = [out_buf.reshape(t // r, r, dm)] + args
        body = _fft_conv_kernel_aliased
        aliases = {0: 0}
    out = pl.pallas_call(
        functools.partial(body, r=r, ca=ca, cb=cb, cd=cd),
        grid=(n_slabs, n_pairs, n_a + n_bc + n_d),
        in_specs=in_specs,
        out_specs=pl.BlockSpec((r, cd, LANES), lambda s, p, st: (blk0 + p, d_step(st), s)),
        out_shape=jax.ShapeDtypeStruct((t // r, r, dm), F32),
        scratch_shapes=[pltpu.VMEM((r * (r + PITCH_PAD), LANES), F32),
                        pltpu.VMEM((r * (r + PITCH_PAD), LANES), F32),
                        pltpu.VMEM((r * ca, LANES), F32),
                        pltpu.VMEM((r * cd, LANES), F32),
                        pltpu.VMEM((r * cd, LANES), F32),
                        pltpu.VMEM((r * cd, LANES), F32)],
        input_output_aliases=aliases,
        compiler_params=pltpu.CompilerParams(
            dimension_semantics=("parallel", "arbitrary", "arbitrary"),
            vmem_limit_bytes=VMEM_LIMIT_BYTES),
        name="fft_conv",
    )(*args)
    return out.reshape(t, dm)


def _fft_spectrum_kernel(c_ref, f1_ref, f2_ref, t1_ref, t2_ref, o_ref, sre_ref, sim_ref, c2d, *, r, ca, cb, scale):
    n_a = r // ca
    step = pl.program_id(2)

    @pl.when(step < n_a)
    def _():
        _stage_a(c_ref, c2d, ca, step * ca, f1_ref, t1_ref, t2_ref, sre_ref, sim_ref, r)

    @pl.when(step >= n_a)
    def _():
        first_k1 = (step - n_a) * cb

        def load(jj):
            k1l = 2 * jj
            rhs, _ = _load_k1_pair(sre_ref, sim_ref, first_k1 + k1l, r)
            return k1l, _split_hi_lo(rhs)

        def forward(state):
            k1l, (d_hi, d_lo) = state
            return k1l, _dot3(f2_ref, d_hi, d_lo) * scale

        def store(state):
            k1l, x = state
            for half in range(2):
                orow = pl.ds(pl.multiple_of((k1l + half) * r, 8), r)
                o_ref[0, orow, :] = _lanes(x[:r], half)
                o_ref[1, orow, :] = _lanes(x[r:], half)

        _staged_loop(cb // 2, [load, forward, store])


def _fft_spectrum(c, r, *, ca=16, cb=16):
    n, width = c.shape
    n_a, n_bc = r // ca, r // cb
    consts = _dft_constants(r)
    cv = c.reshape(r, r, width)

    def full(a):
        return pl.BlockSpec(a.shape, lambda s, o, st, nd=a.ndim: (0,) * nd)

    const_args = [consts["f1_real"], consts["f2"], consts["t1"], consts["t2"]]
    return pl.pallas_call(
        functools.partial(_fft_spectrum_kernel, r=r, ca=ca, cb=cb, scale=1.0 / n),
        grid=(width // LANES, 1, n_a + n_bc),
        in_specs=[pl.BlockSpec((r, ca, LANES), lambda s, o, st: (0, jnp.minimum(st, n_a - 1), s))]
        + [full(a) for a in const_args],
        out_specs=pl.BlockSpec((2, cb * r, LANES), lambda s, o, st: (0, jnp.maximum(st - n_a, 0), s)),
        out_shape=jax.ShapeDtypeStruct((2, n, width), F32),
        scratch_shapes=[pltpu.VMEM((r * (r + PITCH_PAD), LANES), F32),
                        pltpu.VMEM((r * (r + PITCH_PAD), LANES), F32),
                        pltpu.VMEM((r * ca, LANES), F32)],
        compiler_params=pltpu.CompilerParams(
            dimension_semantics=("parallel", "arbitrary", "arbitrary"),
            vmem_limit_bytes=VMEM_LIMIT_BYTES),
        name="fft_spectrum",
    )(cv, *const_args)


def _hyena_filters(length, f_w1, f_b1, f_w2, f_b2, f_freq, f_w3):
    hp = lax.Precision.HIGHEST
    dm = f_w3.shape[1] // (2 * HYENA_ORDER)
    t = jnp.linspace(0.0, 1.0, length, dtype=F32)[:, None]
    w = 2.0 * math.pi * jnp.arange(length, dtype=F32)[:, None] / length
    bands = jnp.linspace(1e-4, FILTER_BANDS - 1, FILTER_BANDS, dtype=F32)[None, :]
    pos = jnp.concatenate([t, jnp.cos(bands * w), -jnp.sin(bands * w)], axis=-1)
    deltas = jnp.abs(jnp.linspace(MIN_DECAY, MAX_DECAY, dm, dtype=F32))
    w3 = f_w3.reshape(f_w3.shape[0], HYENA_ORDER, 2, dm)

    def evaluate(positions, times, direction):
        h = jnp.sin(f_freq * (jnp.dot(positions, f_w1, precision=hp) + f_b1))
        h = jnp.sin(f_freq * (jnp.dot(h, f_w2, precision=hp) + f_b2))
        h = jnp.dot(h, w3[:, :, direction].reshape(-1, HYENA_ORDER * dm), precision=hp)
        return h.reshape(length, HYENA_ORDER, dm) * jnp.exp(-times * deltas)[:, None, :]

    h_fwd = evaluate(pos, t, 0)
    h_bwd_rev = evaluate(pos[::-1], t[::-1], 1)
    norm = jnp.sum(jnp.abs(h_fwd), axis=0) + jnp.sum(jnp.abs(h_bwd_rev), axis=0)
    return ((h_fwd / norm).reshape(length, HYENA_ORDER * dm),
            (h_bwd_rev / norm).reshape(length, HYENA_ORDER * dm))


def _filter_spectra(length, r, filt_params):
    h_fwd, h_bwd_rev = _hyena_filters(length, *filt_params)
    c = jnp.concatenate([h_fwd.at[0].add(h_bwd_rev[length - 1]), jnp.zeros_like(h_fwd[:1]),
                         h_bwd_rev[:length - 1]], axis=0)
    return _fft_spectrum(c, r)


def _hyena_mixer(u, filt_params, skip, seq_tokens, batches):
    n_prompt, s_prompt, s_sample = seq_tokens
    dm = u.shape[1] // 3
    slabs = dm // LANES
    z1 = z2 = None
    for seq_len, n_batch, row0 in ((s_prompt, batches[0], 0), (s_sample, batches[1], n_prompt)):
        r = math.isqrt(2 * seq_len)
        assert r * r == 2 * seq_len and n_batch % 2 == 0 and row0 % (r * r) == 0
        spec = _filter_spectra(seq_len, r, filt_params)
        z1 = _fft_conv(u, 0, u, slabs, row0, spec, 0, skip[0], r, n_batch // 2, dm, z1)
        z2 = _fft_conv(z1, 0, u, 2 * slabs, row0, spec, slabs, skip[1], r, n_batch // 2, dm, z2)
    return z2


def kernel(x_prompt, x_sample, ln_g, ln_b, ffn_w_in, ffn_w_out, attn_w_qkv, attn_w_o, hy_w_in, hy_b_in,
           hy_conv_w, hy_conv_b, hy_f_w1, hy_f_b1, hy_f_w2, hy_f_b2, hy_f_freq, hy_f_w3, hy_skip,
           hy_w_out, hy_b_out):
    bp, sp, dm = x_prompt.shape
    bs, ss, _ = x_sample.shape
    n_prompt = bp * sp
    n_sample = bs * ss
    seq_tokens = (n_prompt, sp, ss)
    x = jnp.concatenate([x_prompt.reshape(n_prompt, dm), x_sample.reshape(n_sample, dm)], axis=0)
    slopes = jnp.exp2(-8.0 * jnp.arange(1, N_SLOTS + 1, dtype=F32) / N_SLOTS)
    zero_bias_qkv = jnp.zeros((QKV_WIDTH,), F32)
    zero_bias_dm = jnp.zeros((dm,), F32)

    for i in range(DEPTH):
        x = _ffn_layer(x, ffn_w_in[i, 0].astype(BF16), ffn_w_out[i, 0].astype(BF16), ln_g[i, 0], ln_b[i, 0])
        j = i // 2
        if i % 2 == 0:
            qkv = _projection(x, attn_w_qkv[j].astype(BF16), zero_bias_qkv, BF16)
            att = _dilated_attention(qkv, slopes, seq_tokens)
            x = _projection_ln(x, att, attn_w_o[j].astype(BF16), zero_bias_dm, ln_g[i, 1], ln_b[i, 1])
        else:
            u = _projection_short_conv(x, hy_w_in[j].astype(BF16), hy_b_in[j], hy_conv_w[j], hy_conv_b[j],
                                       seq_tokens)
            filt = (hy_f_w1[j], hy_f_b1[j], hy_f_w2[j], hy_f_b2[j], hy_f_freq[j], hy_f_w3[j])
            z = _hyena_mixer(u, filt, hy_skip[j], seq_tokens, (bp, bs))
            x = _projection_ln(x, z, hy_w_out[j].astype(BF16), hy_b_out[j], ln_g[i, 1], ln_b[i, 1])
        x = _ffn_layer(x, ffn_w_in[i, 1].astype(BF16), ffn_w_out[i, 1].astype(BF16), ln_g[i, 2], ln_b[i, 2])

    return (x[:n_prompt].reshape(bp, sp, dm), x[n_prompt:].reshape(bs, ss, dm))
```

```python
import functools
import math

import jax
import jax.numpy as jnp
import ml_dtypes
import numpy as np
from jax import lax
from jax.experimental import pallas as pl
from jax.experimental.pallas import tpu as pltpu

D_MODEL = 2048
DEPTH = 4
HEAD_DIM = 128
N_SLOTS = D_MODEL // HEAD_DIM
DILATED_GROUPS = ((128, 1), (512, 4), (2048, 16))
N_GROUPS = len(DILATED_GROUPS)
ATTN_WIDTH = N_SLOTS * HEAD_DIM
QKV_WIDTH = 3 * N_GROUPS * ATTN_WIDTH
HALF_WINDOW = 64
NEG_INF = -1e30
D_FF = 5632
HYENA_ORDER = 2
SHORT_CONV = 3
FILTER_BANDS = 16
DECAY_TARGET = 1e-2
DECAY_FAST = 0.3
DECAY_SLOW = 1.5
MAX_DECAY = math.log(DECAY_TARGET) / DECAY_FAST
MIN_DECAY = math.log(DECAY_TARGET) / DECAY_SLOW
DN_ALPHA = (2.0 * DEPTH) ** 0.25
LN_EPS = 1e-5

LANES = 128
ATTN_BLOCKS = 4
FFT_BLOCKS = 4
COPY_UNROLL = 8
VMEM_LIMIT_BYTES = 48 * 1024 * 1024
BF16 = jnp.bfloat16
F32 = jnp.float32


def _layer_norm_rows(y, g, b):
    mu = jnp.mean(y, axis=-1, keepdims=True)
    c = y - mu
    var = jnp.mean(c * c, axis=-1, keepdims=True)
    return c * lax.rsqrt(var + LN_EPS) * g + b


def _seq_position(t0, seq_tokens):
    n_prompt, s_prompt, s_sample = seq_tokens
    seq_len = jnp.where(t0 < n_prompt, s_prompt, s_sample)
    return lax.rem(t0, seq_len), seq_len


def _ffn_kernel(x_ref, wg_ref, wu_ref, wo_ref, g_ref, b_ref, o_ref, xb_ref, *, nk):
    k = pl.program_id(1)

    @pl.when(k == 0)
    def _():
        xb_ref[...] = x_ref[...].astype(BF16)
        o_ref[...] = jnp.zeros_like(o_ref)

    xb = xb_ref[...]
    gate = jnp.dot(xb, wg_ref[...], preferred_element_type=F32)
    up = jnp.dot(xb, wu_ref[...], preferred_element_type=F32)
    act = (gate * jax.nn.sigmoid(gate) * up).astype(BF16)
    o_ref[...] += jnp.dot(act, wo_ref[...], preferred_element_type=F32)

    @pl.when(k == nk - 1)
    def _():
        y = DN_ALPHA * x_ref[...] + 0.5 * o_ref[...]
        o_ref[...] = _layer_norm_rows(y, g_ref[...], b_ref[...])


def _ffn_layer(x, w_in, w_out, g, b, *, tm=512, tf=512):
    t, dm = x.shape
    dff = w_out.shape[0]
    nk = dff // tf
    return pl.pallas_call(
        functools.partial(_ffn_kernel, nk=nk),
        grid=(t // tm, nk),
        in_specs=[
            pl.BlockSpec((tm, dm), lambda i, k: (i, 0)),
            pl.BlockSpec((dm, tf), lambda i, k: (0, k)),
            pl.BlockSpec((dm, tf), lambda i, k: (0, k + nk)),
            pl.BlockSpec((tf, dm), lambda i, k: (k, 0)),
            pl.BlockSpec((1, dm), lambda i, k: (0, 0)),
            pl.BlockSpec((1, dm), lambda i, k: (0, 0)),
        ],
        out_specs=pl.BlockSpec((tm, dm), lambda i, k: (i, 0)),
        out_shape=jax.ShapeDtypeStruct((t, dm), F32),
        scratch_shapes=[pltpu.VMEM((tm, dm), BF16)],
        compiler_params=pltpu.CompilerParams(
            dimension_semantics=("parallel", "arbitrary"),
            vmem_limit_bytes=VMEM_LIMIT_BYTES),
        name="ffn_ln",
    )(x, w_in, w_in, w_out, g.reshape(1, dm), b.reshape(1, dm))


def _proj_kernel(x_ref, w_ref, b_ref, o_ref, xb_ref):
    @pl.when(pl.program_id(1) == 0)
    def _():
        xb_ref[...] = x_ref[...].astype(BF16)

    acc = jnp.dot(xb_ref[...], w_ref[...], preferred_element_type=F32)
    o_ref[...] = (acc + b_ref[...]).astype(o_ref.dtype)


def _projection(x, w, bias, out_dtype, *, tm=1024, tn=1024):
    t, dm = x.shape
    n = w.shape[1]
    return pl.pallas_call(
        _proj_kernel,
        grid=(t // tm, n // tn),
        in_specs=[
            pl.BlockSpec((tm, dm), lambda i, j: (i, 0)),
            pl.BlockSpec((dm, tn), lambda i, j: (0, j)),
            pl.BlockSpec((1, tn), lambda i, j: (0, j)),
        ],
        out_specs=pl.BlockSpec((tm, tn), lambda i, j: (i, j)),
        out_shape=jax.ShapeDtypeStruct((t, n), out_dtype),
        scratch_shapes=[pltpu.VMEM((tm, dm), BF16)],
        compiler_params=pltpu.CompilerParams(
            dimension_semantics=("parallel", "arbitrary"),
            vmem_limit_bytes=VMEM_LIMIT_BYTES),
        name="projection",
    )(x, w, bias.reshape(1, n))


HALO_ROWS = 16


def _proj_conv_kernel(xp_ref, x_ref, xn_ref, w_ref, b_ref, cw_ref, cb_ref, o_ref, xb_ref, *, tm, seq_tokens):
    @pl.when(pl.program_id(1) == 0)
    def _():
        xb_ref[0:HALO_ROWS, :] = xp_ref[...].astype(BF16)
        xb_ref[HALO_ROWS:HALO_ROWS + tm, :] = x_ref[...].astype(BF16)
        xb_ref[HALO_ROWS + tm:, :] = xn_ref[...].astype(BF16)

    rows = tm + 2 * HALO_ROWS
    acc = jnp.dot(xb_ref[...], w_ref[...], preferred_element_type=F32) + b_ref[...]
    pos0, seq_len = _seq_position(pl.program_id(0) * tm, seq_tokens)
    tile_row = lax.broadcasted_iota(jnp.int32, (tm, 1), 0)
    at_seq_start = (tile_row == 0) & (pos0 == 0)
    at_seq_end = (tile_row == tm - 1) & (pos0 + tm == seq_len)
    cur = acc[HALO_ROWS:HALO_ROWS + tm]
    prev = pltpu.roll(acc, 1, axis=0)[HALO_ROWS:HALO_ROWS + tm]
    nxt = pltpu.roll(acc, rows - 1, axis=0)[HALO_ROWS:HALO_ROWS + tm]
    prev = jnp.where(at_seq_start, 0.0, prev)
    nxt = jnp.where(at_seq_end, 0.0, nxt)
    o_ref[...] = prev * cw_ref[0:1, :] + cur * cw_ref[1:2, :] + nxt * cw_ref[2:3, :] + cb_ref[...]


def _projection_short_conv(x, w, bias, conv_w, conv_b, seq_tokens, *, tm=1024, tn=1024):
    t, dm = x.shape
    n = w.shape[1]
    ratio = tm // HALO_ROWS
    n_halo_blocks = t // HALO_ROWS
    return pl.pallas_call(
        functools.partial(_proj_conv_kernel, tm=tm, seq_tokens=seq_tokens),
        grid=(t // tm, n // tn),
        in_specs=[
            pl.BlockSpec((HALO_ROWS, dm), lambda i, j: (jnp.maximum(i * ratio - 1, 0), 0)),
            pl.BlockSpec((tm, dm), lambda i, j: (i, 0)),
            pl.BlockSpec((HALO_ROWS, dm), lambda i, j: (jnp.minimum((i + 1) * ratio, n_halo_blocks - 1), 0)),
            pl.BlockSpec((dm, tn), lambda i, j: (0, j)),
            pl.BlockSpec((1, tn), lambda i, j: (0, j)),
            pl.BlockSpec((SHORT_CONV, tn), lambda i, j: (0, j)),
            pl.BlockSpec((1, tn), lambda i, j: (0, j)),
        ],
        out_specs=pl.BlockSpec((tm, tn), lambda i, j: (i, j)),
        out_shape=jax.ShapeDtypeStruct((t, n), F32),
        scratch_shapes=[pltpu.VMEM((tm + 2 * HALO_ROWS, dm), BF16)],
        compiler_params=pltpu.CompilerParams(
            dimension_semantics=("parallel", "arbitrary"),
            vmem_limit_bytes=VMEM_LIMIT_BYTES),
        name="projection_short_conv",
    )(x, x, x, w, bias.reshape(1, n), conv_w, conv_b.reshape(1, n))


def _proj_ln_kernel(x_ref, z_ref, w_ref, bias_ref, g_ref, b_ref, o_ref):
    mix = jnp.dot(z_ref[...].astype(BF16), w_ref[...], preferred_element_type=F32)
    y = DN_ALPHA * x_ref[...] + (mix + bias_ref[...])
    o_ref[...] = _layer_norm_rows(y, g_ref[...], b_ref[...])


def _projection_ln(x, z, w, bias, g, b, *, tm=512):
    t, dm = x.shape
    kdim = z.shape[1]
    return pl.pallas_call(
        _proj_ln_kernel,
        grid=(t // tm,),
        in_specs=[
            pl.BlockSpec((tm, dm), lambda i: (i, 0)),
            pl.BlockSpec((tm, kdim), lambda i: (i, 0)),
            pl.BlockSpec((kdim, dm), lambda i: (0, 0)),
            pl.BlockSpec((1, dm), lambda i: (0, 0)),
            pl.BlockSpec((1, dm), lambda i: (0, 0)),
            pl.BlockSpec((1, dm), lambda i: (0, 0)),
        ],
        out_specs=pl.BlockSpec((tm, dm), lambda i: (i, 0)),
        out_shape=jax.ShapeDtypeStruct((t, dm), F32),
        compiler_params=pltpu.CompilerParams(
            dimension_semantics=("parallel",),
            vmem_limit_bytes=VMEM_LIMIT_BYTES),
        name="projection_ln",
    )(x, z, w, bias.reshape(1, dm), g.reshape(1, dm), b.reshape(1, dm))


def _attn_kernel(slopes_ref, *refs, tq, hg, seq_tokens):
    n_in = 7 * N_GROUPS
    in_refs = refs[:n_in]
    o_ref = refs[n_in]
    qf_ref, kf_ref, vf_ref, og_ref, lg_ref = refs[n_in + 1:]

    pos0, seq_len = _seq_position(pl.program_id(0) * tq, seq_tokens)
    scale = HEAD_DIM ** -0.5

    for hh in range(hg):
        cols = slice(hh * HEAD_DIM, (hh + 1) * HEAD_DIM)
        slope = slopes_ref[pl.program_id(1) * hg + hh]
        for g, (_, d) in enumerate(DILATED_GROUPS):
            q_ref, kp_ref, kc_ref, kn_ref, vp_ref, vc_ref, vn_ref = in_refs[7 * g:7 * g + 7]
            halo = HALF_WINDOW * d
            nq = tq // d
            sb = min(nq, 128)
            nkeys = sb + 2 * HALF_WINDOW
            chain_len = seq_len // d
            cpos0 = pos0 // d

            qf_ref[0:tq, :] = q_ref[:, cols].astype(F32)
            kf_ref[0:halo, :] = kp_ref[:, cols].astype(F32)
            kf_ref[halo:halo + tq, :] = kc_ref[:, cols].astype(F32)
            kf_ref[halo + tq:2 * halo + tq, :] = kn_ref[:, cols].astype(F32)
            vf_ref[0:halo, :] = vp_ref[:, cols].astype(F32)
            vf_ref[halo:halo + tq, :] = vc_ref[:, cols].astype(F32)
            vf_ref[halo + tq:2 * halo + tq, :] = vn_ref[:, cols].astype(F32)

            qi = lax.broadcasted_iota(jnp.int32, (sb, nkeys), 0)
            ki = lax.broadcasted_iota(jnp.int32, (sb, nkeys), 1)
            rel = ki - HALF_WINDOW - qi
            in_band = jnp.abs(rel) <= HALF_WINDOW
            alibi = (-slope) * (jnp.abs(rel) * d).astype(F32)

            def chain_blocks(it, carry, d=d, sb=sb, nkeys=nkeys, nq=nq, g=g,
                             in_band=in_band, alibi=alibi, ki=ki,
                             chain_len=chain_len, cpos0=cpos0):
                rows, scores = [], []
                for j in range(ATTN_BLOCKS):
                    idx = it * ATTN_BLOCKS + j
                    r = idx // (nq // sb)
                    s = idx % (nq // sb)
                    start = r + s * (sb * d)
                    if d == 1:
                        q_rows, k_rows = pl.ds(start, sb), pl.ds(start, nkeys)
                    else:
                        q_rows, k_rows = pl.ds(start, sb, stride=d), pl.ds(start, nkeys, stride=d)
                    qc = qf_ref[q_rows, :].astype(BF16)
                    kc = kf_ref[k_rows, :].astype(BF16)
                    sc = lax.dot_general(qc, kc, (((1,), (1,)), ((), ())), preferred_element_type=F32)
                    rows.append((q_rows, k_rows, s))
                    scores.append(sc)
                probs = []
                for (q_rows, k_rows, s), sc in zip(rows, scores):
                    kpos0 = cpos0 + s * sb - HALF_WINDOW
                    valid = in_band & (ki >= -kpos0) & (ki < chain_len - kpos0)
                    sc = jnp.where(valid, sc * scale + alibi, NEG_INF)
                    mx = jnp.max(sc, axis=-1, keepdims=True)
                    p = jnp.exp(sc - mx)
                    den = jnp.sum(p, axis=-1, keepdims=True)
                    probs.append((p.astype(BF16), mx, den))
                outs = [jnp.dot(p, vf_ref[k_rows, :].astype(BF16), preferred_element_type=F32)
                        for (p, _, _), (_, k_rows, _) in zip(probs, rows)]
                for (q_rows, _, _), (_, mx, den), o in zip(rows, probs, outs):
                    og_ref[g, q_rows, :] = o / den
                    lg_ref[g, q_rows, :] = jnp.broadcast_to(mx + jnp.log(den), (sb, HEAD_DIM))
                return carry

            lax.fori_loop(0, d * (nq // sb) // ATTN_BLOCKS, chain_blocks, 0)

        l0, l1, l2 = lg_ref[0], lg_ref[1], lg_ref[2]
        lmax = jnp.maximum(jnp.maximum(l0, l1), l2)
        w0, w1, w2 = jnp.exp(l0 - lmax), jnp.exp(l1 - lmax), jnp.exp(l2 - lmax)
        mixed = (w0 * og_ref[0] + w1 * og_ref[1] + w2 * og_ref[2]) / (w0 + w1 + w2)
        o_ref[:, cols] = mixed.astype(o_ref.dtype)


def _dilated_attention(qkv, slopes, seq_tokens, *, tq=1024, hg=4):
    t = qkv.shape[0]
    wcols = hg * HEAD_DIM
    blocks_per_part = ATTN_WIDTH // wcols
    in_specs = []
    args = []
    max_halo = HALF_WINDOW * max(d for _, d in DILATED_GROUPS)
    for g, (_, d) in enumerate(DILATED_GROUPS):
        halo = HALF_WINDOW * d
        ratio = tq // halo
        n_halo_blocks = t // halo

        def col(part, g=g):
            return lambda i, h, *_: (part * N_GROUPS + g) * blocks_per_part + h

        def cur(part, g=g):
            c = col(part, g)
            return pl.BlockSpec((tq, wcols), lambda i, h, *_: (i, c(i, h)))

        def prev(part, g=g, ratio=ratio):
            c = col(part, g)
            return pl.BlockSpec((halo, wcols),
                                lambda i, h, *_: (jnp.maximum(i * ratio - 1, 0), c(i, h)))

        def nxt(part, g=g, ratio=ratio, n_halo_blocks=n_halo_blocks):
            c = col(part, g)
            return pl.BlockSpec((halo, wcols),
                                lambda i, h, *_: (jnp.minimum((i + 1) * ratio, n_halo_blocks - 1), c(i, h)))

        in_specs += [cur(0), prev(1), cur(1), nxt(1), prev(2), cur(2), nxt(2)]
        args += [qkv] * 7

    grid_spec = pltpu.PrefetchScalarGridSpec(
        num_scalar_prefetch=1,
        grid=(t // tq, ATTN_WIDTH // wcols),
        in_specs=in_specs,
        out_specs=pl.BlockSpec((tq, wcols), lambda i, h, *_: (i, h)),
        scratch_shapes=[
            pltpu.VMEM((tq, HEAD_DIM), F32),
            pltpu.VMEM((tq + 2 * max_halo, HEAD_DIM), F32),
            pltpu.VMEM((tq + 2 * max_halo, HEAD_DIM), F32),
            pltpu.VMEM((N_GROUPS, tq, HEAD_DIM), F32),
            pltpu.VMEM((N_GROUPS, tq, HEAD_DIM), F32),
        ],
    )
    return pl.pallas_call(
        functools.partial(_attn_kernel, tq=tq, hg=hg, seq_tokens=seq_tokens),
        grid_spec=grid_spec,
        out_shape=jax.ShapeDtypeStruct((t, ATTN_WIDTH), BF16),
        compiler_params=pltpu.CompilerParams(
            dimension_semantics=("parallel", "arbitrary"),
            vmem_limit_bytes=VMEM_LIMIT_BYTES),
        name="dilated_attention",
    )(slopes, *args)


TWIDDLE_RADIX = 16
PITCH_PAD = 8


def _split_hi_lo_np(m):
    hi = m.astype(ml_dtypes.bfloat16)
    lo = (m - hi.astype(np.float64)).astype(ml_dtypes.bfloat16)
    return np.stack([hi, lo])


@functools.lru_cache(maxsize=None)
def _dft_constants(r):
    h = r // 2
    idx = np.arange(r, dtype=np.float64)
    ang = 2.0 * np.pi * np.outer(idx, idx) / r
    c, s = np.cos(ang), np.sin(ang)
    f1 = np.block([[c[:, :h], s[:, :h]], [-s[:, :h], c[:, :h]]])
    f1_real = np.concatenate([c, -s], axis=0)
    f2 = np.block([[c, s], [-s, c]])
    f2_inv = np.block([[c, -s], [s, c]])
    f1_inv = np.block([[c[:h, :], -s[:h, :]], [s[:h, :], c[:h, :]]])
    n = r * r
    coarse = np.arange(r // TWIDDLE_RADIX, dtype=np.float64)[:, None] * TWIDDLE_RADIX
    fine = np.arange(TWIDDLE_RADIX, dtype=np.float64)[:, None]

    def table(mult):
        a = 2.0 * np.pi * mult * idx[None, :] / n
        t = np.stack([np.cos(a), np.sin(a)], axis=1)
        return np.ascontiguousarray(np.broadcast_to(t[..., None], t.shape + (LANES,))).astype(np.float32)

    return dict(f1=_split_hi_lo_np(f1), f1_real=_split_hi_lo_np(f1_real), f2=_split_hi_lo_np(f2),
                f2_inv=_split_hi_lo_np(f2_inv), f1_inv=_split_hi_lo_np(f1_inv),
                t1=table(coarse), t2=table(fine))


def _split_hi_lo(data):
    d_hi = data.astype(BF16)
    return d_hi, (data - d_hi.astype(F32)).astype(BF16)


def _dot3(m_ref, d_hi, d_lo):
    m_hi, m_lo = m_ref[0], m_ref[1]
    return (jnp.dot(m_hi, d_hi, preferred_element_type=F32)
            + jnp.dot(m_hi, d_lo, preferred_element_type=F32)
            + jnp.dot(m_lo, d_hi, preferred_element_type=F32))


def _staged_loop(n_items, stages):
    def body(it, carry):
        state = [it * FFT_BLOCKS + j for j in range(FFT_BLOCKS)]
        for stage in stages:
            state = [stage(s) for s in state]
        return carry

    lax.fori_loop(0, n_items // FFT_BLOCKS, body, 0)


def _twiddle(t1_ref, t2_ref, idx):
    a = idx // TWIDDLE_RADIX
    b = idx % TWIDDLE_RADIX
    c1, s1 = t1_ref[a, 0], t1_ref[a, 1]
    c2, s2 = t2_ref[b, 0], t2_ref[b, 1]
    return c1 * c2 - s1 * s2, s1 * c2 + c1 * s2


def _lanes(x, half):
    return x[:, half * LANES:(half + 1) * LANES]


def _slab_rows(r, chunk):
    return r * (chunk + PITCH_PAD)


def _blocks_to_rows(src_ref, dst_ref, r):
    chunk = src_ref.shape[1]
    pitch = chunk + PITCH_PAD

    def body(g, carry):
        dst_ref[pl.ds(pl.multiple_of(g * pitch, 8), chunk), :] = src_ref[g]
        return carry

    lax.fori_loop(0, r, body, 0, unroll=COPY_UNROLL)


def _rows_to_blocks(src_ref, dst_ref, r):
    chunk = dst_ref.shape[1]
    pitch = chunk + PITCH_PAD

    def body(g, carry):
        dst_ref[g] = src_ref[pl.ds(pl.multiple_of(g * pitch, 8), chunk), :]
        return carry

    lax.fori_loop(0, r, body, 0, unroll=COPY_UNROLL)


def _stage_a(x_ref, x2d, chunk, first_n2, f1_ref, t1_ref, t2_ref, sre_ref, sim_ref, r):
    pitch = r + PITCH_PAD
    _blocks_to_rows(x_ref, x2d, r)

    def gather(jj):
        n2l = 2 * jj
        rhs = jnp.concatenate([x2d[pl.ds(n2l, r, stride=chunk + PITCH_PAD), :],
                               x2d[pl.ds(n2l + 1, r, stride=chunk + PITCH_PAD), :]], axis=1)
        return n2l, _split_hi_lo(rhs)

    def transform(state):
        n2l, (d_hi, d_lo) = state
        return n2l, _dot3(f1_ref, d_hi, d_lo)

    def twiddle_store(state):
        n2l, out = state
        for half in range(2):
            re, im = _lanes(out[:r], half), _lanes(out[r:], half)
            n2 = first_n2 + n2l + half
            c, s = _twiddle(t1_ref, t2_ref, n2)
            rows = pl.ds(n2, r, stride=pitch)
            sre_ref[rows, :] = re * c + im * s
            sim_ref[rows, :] = im * c - re * s

    _staged_loop(chunk // 2, [gather, transform, twiddle_store])


def _load_k1_pair(sre_ref, sim_ref, k1, r):
    pitch = r + PITCH_PAD
    r0 = pl.multiple_of(k1 * pitch, 8)
    r1 = pl.multiple_of((k1 + 1) * pitch, 8)
    rows = (pl.ds(r0, r), pl.ds(r1, r))
    rhs = jnp.concatenate([
        jnp.concatenate([sre_ref[rows[0], :], sre_ref[rows[1], :]], axis=1),
        jnp.concatenate([sim_ref[rows[0], :], sim_ref[rows[1], :]], axis=1)], axis=0)
    return rhs, rows


def _fft_conv_kernel(x_ref, gate_ref, cs_ref, f1_ref, f1i_ref, f2_ref, f2i_ref, t1_ref, t2_ref,
                     o_ref, sre_ref, sim_ref, x2d, g2d, o2d, *, r, ca, cb, cd):
    pitch = r + PITCH_PAD
    n_a, n_bc = r // ca, r // cb
    step = pl.program_id(2)

    @pl.when(step < n_a)
    def _():
        _stage_a(x_ref, x2d, ca, step * ca, f1_ref, t1_ref, t2_ref, sre_ref, sim_ref, r)

    @pl.when((step >= n_a) & (step < n_a + n_bc))
    def _():
        first_k1 = (step - n_a) * cb

        def load(jj):
            k1l = 2 * jj
            rhs, rows = _load_k1_pair(sre_ref, sim_ref, first_k1 + k1l, r)
            return k1l, rows, _split_hi_lo(rhs)

        def forward(state):
            k1l, rows, (d_hi, d_lo) = state
            return k1l, rows, _dot3(f2_ref, d_hi, d_lo)

        def filter_multiply(state):
            k1l, rows, x = state
            p_re, p_im = [], []
            for half in range(2):
                xr, xi = _lanes(x[:r], half), _lanes(x[r:], half)
                crow = pl.ds(pl.multiple_of((k1l + half) * r, 8), r)
                cr, ci = cs_ref[0, crow, :], cs_ref[1, crow, :]
                p_re.append(xr * cr - xi * ci)
                p_im.append(xr * ci + xi * cr)
            prod = jnp.concatenate([jnp.concatenate(p_re, axis=1), jnp.concatenate(p_im, axis=1)], axis=0)
            return k1l, rows, _split_hi_lo(prod)

        def inverse(state):
            k1l, rows, (d_hi, d_lo) = state
            return k1l, rows, _dot3(f2i_ref, d_hi, d_lo)

        def twiddle_store(state):
            k1l, rows, y = state
            for half in range(2):
                re, im = _lanes(y[:r], half), _lanes(y[r:], half)
                c, s = _twiddle(t1_ref, t2_ref, first_k1 + k1l + half)
                sre_ref[rows[half], :] = re * c - im * s
                sim_ref[rows[half], :] = im * c + re * s

        _staged_loop(cb // 2, [load, forward, filter_multiply, inverse, twiddle_store])

    @pl.when(step >= n_a + n_bc)
    def _():
        first_n2 = (step - n_a - n_bc) * cd
        _blocks_to_rows(gate_ref, g2d, r)

        def gather(jj):
            n2l = 2 * jj
            n2 = first_n2 + n2l
            rows0 = pl.ds(n2, r, stride=pitch)
            rows1 = pl.ds(n2 + 1, r, stride=pitch)
            rhs = jnp.concatenate([
                jnp.concatenate([sre_ref[rows0, :], sre_ref[rows1, :]], axis=1),
                jnp.concatenate([sim_ref[rows0, :], sim_ref[rows1, :]], axis=1)], axis=0)
            return n2l, _split_hi_lo(rhs)

        def transform(state):
            n2l, (d_hi, d_lo) = state
            return n2l, _dot3(f1i_ref, d_hi, d_lo)

        def gate_store(state):
            n2l, y = state
            for half in range(2):
                tok = pl.ds(n2l + half, r, stride=cd + PITCH_PAD)
                o2d[tok, :] = g2d[tok, :] * _lanes(y, half)

        _staged_loop(cd // 2, [gather, transform, gate_store])
        _rows_to_blocks(o2d, o_ref, r)


def _fft_conv_kernel_aliased(*refs, **kw):
    _fft_conv_kernel(*refs[1:], **kw)


def _fft_conv(x, x_col0, gate, gate_col0, row0, spectrum, spec_col0, r, n_pairs, dm, out_buf=None,
              *, ca=32, cb=16, cd=32):
    n_a, n_bc, n_d = r // ca, r // cb, r // cd
    n_slabs = dm // LANES
    consts = _dft_constants(r)
    t = x.shape[0]
    xv = x.reshape(t // r, r, x.shape[1])
    gv = gate.reshape(t // r, r, gate.shape[1])
    blk0 = row0 // (r * r)

    def d_step(st):
        return jnp.clip(st - n_a - n_bc, 0, n_d - 1)

    def full(a):
        return pl.BlockSpec(a.shape, lambda s, p, st, nd=a.ndim: (0,) * nd)

    const_args = [consts["f1"], consts["f1_inv"], consts["f2"], consts["f2_inv"], consts["t1"], consts["t2"]]
    in_specs = [
        pl.BlockSpec((r, ca, LANES), lambda s, p, st: (blk0 + p, jnp.minimum(st, n_a - 1), x_col0 + s)),
        pl.BlockSpec((r, cd, LANES), lambda s, p, st: (blk0 + p, d_step(st), gate_col0 + s)),
        pl.BlockSpec((2, cb * r, LANES), lambda s, p, st: (0, jnp.clip(st - n_a, 0, n_bc - 1), spec_col0 + s)),
    ] + [full(a) for a in const_args]
    args = [xv, gv, spectrum, *const_args]
    body = _fft_conv_kernel
    aliases = {}
    if out_buf is not None:
        in_specs = [pl.BlockSpec(memory_space=pl.ANY)] + in_specs
        args = [out_buf.reshape(t // r, r, dm)] + args
        body = _fft_conv_kernel_aliased
        aliases = {0: 0}
    out = pl.pallas_call(
        functools.partial(body, r=r, ca=ca, cb=cb, cd=cd),
        grid=(n_slabs, n_pairs, n_a + n_bc + n_d),
        in_specs=in_specs,
        out_specs=pl.BlockSpec((r, cd, LANES), lambda s, p, st: (blk0 + p, d_step(st), s)),
        out_shape=jax.ShapeDtypeStruct((t // r, r, dm), F32),
        scratch_shapes=[pltpu.VMEM((r * (r + PITCH_PAD), LANES), F32),
                        pltpu.VMEM((r * (r + PITCH_PAD), LANES), F32),
                        pltpu.VMEM((_slab_rows(r, ca), LANES), F32),
                        pltpu.VMEM((_slab_rows(r, cd), LANES), F32),
                        pltpu.VMEM((_slab_rows(r, cd), LANES), F32)],
        input_output_aliases=aliases,
        compiler_params=pltpu.CompilerParams(
            dimension_semantics=("parallel", "arbitrary", "arbitrary"),
            vmem_limit_bytes=VMEM_LIMIT_BYTES),
        name="fft_conv",
    )(*args)
    return out.reshape(t, dm)


def _fft_spectrum_kernel(c_ref, scale_ref, shift_ref, f1_ref, f2_ref, t1_ref, t2_ref, o_ref, sre_ref, sim_ref, c2d,
                         *, r, ca, cb):
    n_a = r // ca
    step = pl.program_id(2)

    @pl.when(step < n_a)
    def _():
        _stage_a(c_ref, c2d, ca, step * ca, f1_ref, t1_ref, t2_ref, sre_ref, sim_ref, r)

    @pl.when(step >= n_a)
    def _():
        first_k1 = (step - n_a) * cb

        def load(jj):
            k1l = 2 * jj
            rhs, _ = _load_k1_pair(sre_ref, sim_ref, first_k1 + k1l, r)
            return k1l, _split_hi_lo(rhs)

        def forward(state):
            k1l, (d_hi, d_lo) = state
            return k1l, _dot3(f2_ref, d_hi, d_lo)

        def store(state):
            k1l, x = state
            for half in range(2):
                orow = pl.ds(pl.multiple_of((k1l + half) * r, 8), r)
                o_ref[0, orow, :] = _lanes(x[:r], half) * scale_ref[...] + shift_ref[...]
                o_ref[1, orow, :] = _lanes(x[r:], half) * scale_ref[...]

        _staged_loop(cb // 2, [load, forward, store])


def _fft_spectrum(c, scale, shift, r, *, ca=32, cb=16):
    n, width = c.shape
    n_a, n_bc = r // ca, r // cb
    consts = _dft_constants(r)
    cv = c.reshape(r, r, width)

    def full(a):
        return pl.BlockSpec(a.shape, lambda s, o, st, nd=a.ndim: (0,) * nd)

    const_args = [consts["f1_real"], consts["f2"], consts["t1"], consts["t2"]]
    return pl.pallas_call(
        functools.partial(_fft_spectrum_kernel, r=r, ca=ca, cb=cb),
        grid=(width // LANES, 1, n_a + n_bc),
        in_specs=[pl.BlockSpec((r, ca, LANES), lambda s, o, st: (0, jnp.minimum(st, n_a - 1), s)),
                  pl.BlockSpec((1, LANES), lambda s, o, st: (0, s)),
                  pl.BlockSpec((1, LANES), lambda s, o, st: (0, s))]
        + [full(a) for a in const_args],
        out_specs=pl.BlockSpec((2, cb * r, LANES), lambda s, o, st: (0, jnp.maximum(st - n_a, 0), s)),
        out_shape=jax.ShapeDtypeStruct((2, n, width), F32),
        scratch_shapes=[pltpu.VMEM((r * (r + PITCH_PAD), LANES), F32),
                        pltpu.VMEM((r * (r + PITCH_PAD), LANES), F32),
                        pltpu.VMEM((_slab_rows(r, ca), LANES), F32)],
        compiler_params=pltpu.CompilerParams(
            dimension_semantics=("parallel", "arbitrary", "arbitrary"),
            vmem_limit_bytes=VMEM_LIMIT_BYTES),
        name="fft_spectrum",
    )(cv, scale.reshape(1, width), shift.reshape(1, width), *const_args)


FILTER_EMB = 2 * FILTER_BANDS + 1


def _dot_f32(a, b):
    a_hi, a_lo = _split_hi_lo(a)
    b_hi, b_lo = _split_hi_lo(b)
    return (jnp.dot(a_hi, b_hi, preferred_element_type=F32)
            + jnp.dot(a_hi, b_lo, preferred_element_type=F32)
            + jnp.dot(a_lo, b_hi, preferred_element_type=F32))


def _filter_kernel(bands_ref, w1_ref, b1_ref, w2_ref, b2_ref, freq_ref, w3f_ref, w3b_ref, delta_ref,
                   c_ref, norm_ref, hid_ref, *, length, tr):
    j = pl.program_id(0)
    i = pl.program_id(1)
    row = i * tr + lax.broadcasted_iota(jnp.int32, (tr, 1), 0)
    pos_idx = jnp.where(row < length, row, 2 * length - row)
    mf = pos_idx.astype(F32)
    t = mf / (length - 1)
    rows = pl.ds(pl.multiple_of(i * tr, 8), tr)

    @pl.when(j == 0)
    def _():
        w = 2.0 * math.pi * mf / length
        arg = w * bands_ref[...]
        lane = lax.broadcasted_iota(jnp.int32, (1, LANES), 1)
        pos = jnp.where(lane == 0, t,
                        jnp.where(lane <= FILTER_BANDS, jnp.cos(arg),
                                  jnp.where(lane < FILTER_EMB, -jnp.sin(arg), 0.0)))
        h = jnp.sin(freq_ref[...] * (_dot_f32(pos, w1_ref[...]) + b1_ref[...]))
        h = jnp.sin(freq_ref[...] * (_dot_f32(h, w2_ref[...]) + b2_ref[...]))
        hid_ref[rows, :] = h

    h = hid_ref[rows, :]
    decay = jnp.exp(-t * delta_ref[...])
    fwd = _dot_f32(h, w3f_ref[...]) * decay
    bwd = _dot_f32(h, w3b_ref[...]) * decay
    main = jnp.where(row < length, fwd, jnp.where(row == length, 0.0, bwd))
    extra = jnp.where(row == 0, bwd, 0.0)
    c_ref[...] = main + extra
    contrib = jnp.abs(main) + jnp.abs(extra)
    part = contrib[0:8]
    for k in range(1, tr // 8):
        part = part + contrib[8 * k:8 * (k + 1)]

    @pl.when(i == 0)
    def _():
        norm_ref[...] = part

    @pl.when(i > 0)
    def _():
        norm_ref[...] += part


def _hyena_filter_kernel(length, f_w1, f_b1, f_w2, f_b2, f_freq, f_w3, *, tr=512, tc=1024):
    hidden = f_w1.shape[1]
    dm = f_w3.shape[1] // (2 * HYENA_ORDER)
    width = HYENA_ORDER * dm
    tr, tc = min(tr, 2 * length), min(tc, width)
    bands = jnp.linspace(1e-4, FILTER_BANDS - 1, FILTER_BANDS, dtype=F32)
    bands_row = jnp.zeros((1, LANES), F32).at[0, 1:FILTER_BANDS + 1].set(bands)
    bands_row = bands_row.at[0, FILTER_BANDS + 1:FILTER_EMB].set(bands)

    def pad_to(a, rows, cols):
        return jnp.zeros((rows, cols), F32).at[:a.shape[0], :a.shape[1]].set(a)

    w3 = f_w3.reshape(hidden, HYENA_ORDER, 2, dm)
    w3f = pad_to(w3[:, :, 0].reshape(hidden, width), LANES, width)
    w3b = pad_to(w3[:, :, 1].reshape(hidden, width), LANES, width)
    deltas = jnp.abs(jnp.linspace(MIN_DECAY, MAX_DECAY, dm, dtype=F32))
    delta_row = jnp.tile(deltas, HYENA_ORDER).reshape(1, width)
    small = [bands_row, pad_to(f_w1, LANES, LANES), pad_to(f_b1[None], 1, LANES), pad_to(f_w2, LANES, LANES),
             pad_to(f_b2[None], 1, LANES), pad_to(f_freq[None], 1, LANES)]
    c, norm = pl.pallas_call(
        functools.partial(_filter_kernel, length=length, tr=tr),
        grid=(width // tc, 2 * length // tr),
        in_specs=[pl.BlockSpec(a.shape, lambda j, i: (0, 0)) for a in small] + [
            pl.BlockSpec((LANES, tc), lambda j, i: (0, j)),
            pl.BlockSpec((LANES, tc), lambda j, i: (0, j)),
            pl.BlockSpec((1, tc), lambda j, i: (0, j)),
        ],
        out_specs=[pl.BlockSpec((tr, tc), lambda j, i: (i, j)),
                   pl.BlockSpec((8, tc), lambda j, i: (0, j))],
        out_shape=[jax.ShapeDtypeStruct((2 * length, width), F32),
                   jax.ShapeDtypeStruct((8, width), F32)],
        scratch_shapes=[pltpu.VMEM((2 * length, LANES), F32)],
        compiler_params=pltpu.CompilerParams(
            dimension_semantics=("arbitrary", "arbitrary"),
            vmem_limit_bytes=VMEM_LIMIT_BYTES),
        name="hyena_filter",
    )(*small, w3f, w3b, delta_row)
    return c, jnp.sum(norm, axis=0)


def _filter_spectra(length, r, filt_params, skip):
    c, norm = _hyena_filter_kernel(length, *filt_params)
    n = 2 * length
    return _fft_spectrum(c, 1.0 / (n * norm), skip.reshape(-1) / n, r)


def _hyena_mixer(u, filt_params, skip, seq_tokens, batches):
    n_prompt, s_prompt, s_sample = seq_tokens
    dm = u.shape[1] // 3
    slabs = dm // LANES
    z1 = z2 = None
    for seq_len, n_batch, row0 in ((s_prompt, batches[0], 0), (s_sample, batches[1], n_prompt)):
        r = math.isqrt(2 * seq_len)
        assert r * r == 2 * seq_len and n_batch % 2 == 0 and row0 % (r * r) == 0
        spec = _filter_spectra(seq_len, r, filt_params, skip)
        chunk = 16 if r > 64 else 32
        z1 = _fft_conv(u, 0, u, slabs, row0, spec, 0, r, n_batch // 2, dm, z1, ca=chunk, cd=chunk)
        z2 = _fft_conv(z1, 0, u, 2 * slabs, row0, spec, slabs, r, n_batch // 2, dm, z2, ca=chunk, cd=chunk)
    return z2


def kernel(x_prompt, x_sample, ln_g, ln_b, ffn_w_in, ffn_w_out, attn_w_qkv, attn_w_o, hy_w_in, hy_b_in,
           hy_conv_w, hy_conv_b, hy_f_w1, hy_f_b1, hy_f_w2, hy_f_b2, hy_f_freq, hy_f_w3, hy_skip,
           hy_w_out, hy_b_out):
    bp, sp, dm = x_prompt.shape
    bs, ss, _ = x_sample.shape
    n_prompt = bp * sp
    n_sample = bs * ss
    seq_tokens = (n_prompt, sp, ss)
    x = jnp.concatenate([x_prompt.reshape(n_prompt, dm), x_sample.reshape(n_sample, dm)], axis=0)
    slopes = jnp.exp2(-8.0 * jnp.arange(1, N_SLOTS + 1, dtype=F32) / N_SLOTS)
    zero_bias_qkv = jnp.zeros((QKV_WIDTH,), F32)
    zero_bias_dm = jnp.zeros((dm,), F32)

    for i in range(DEPTH):
        x = _ffn_layer(x, ffn_w_in[i, 0].astype(BF16), ffn_w_out[i, 0].astype(BF16), ln_g[i, 0], ln_b[i, 0])
        j = i // 2
        if i % 2 == 0:
            qkv = _projection(x, attn_w_qkv[j].astype(BF16), zero_bias_qkv, BF16)
            att = _dilated_attention(qkv, slopes, seq_tokens)
            x = _projection_ln(x, att, attn_w_o[j].astype(BF16), zero_bias_dm, ln_g[i, 1], ln_b[i, 1])
        else:
            u = _projection_short_conv(x, hy_w_in[j].astype(BF16), hy_b_in[j], hy_conv_w[j], hy_conv_b[j],
                                       seq_tokens)
            filt = (hy_f_w1[j], hy_f_b1[j], hy_f_w2[j], hy_f_b2[j], hy_f_freq[j], hy_f_w3[j])
            z = _hyena_mixer(u, filt, hy_skip[j], seq_tokens, (bp, bs))
            x = _projection_ln(x, z, hy_w_out[j].astype(BF16), hy_b_out[j], ln_g[i, 1], ln_b[i, 1])
        x = _ffn_layer(x, ffn_w_in[i, 1].astype(BF16), ffn_w_out[i, 1].astype(BF16), ln_g[i, 2], ln_b[i, 2])

    return (x[:n_prompt].reshape(bp, sp, dm), x[n_prompt:].reshape(bs, ss, dm))
```

```python
import functools
import math

import jax
import jax.numpy as jnp
import ml_dtypes
import numpy as np
from jax import lax
from jax.experimental import pallas as pl
from jax.experimental.pallas import tpu as pltpu

D_MODEL = 2048
DEPTH = 4
HEAD_DIM = 128
N_SLOTS = D_MODEL // HEAD_DIM
DILATED_GROUPS = ((128, 1), (512, 4), (2048, 16))
N_GROUPS = len(DILATED_GROUPS)
ATTN_WIDTH = N_SLOTS * HEAD_DIM
QKV_WIDTH = 3 * N_GROUPS * ATTN_WIDTH
HALF_WINDOW = 64
NEG_INF = -1e30
D_FF = 5632
HYENA_ORDER = 2
SHORT_CONV = 3
FILTER_BANDS = 16
DECAY_TARGET = 1e-2
DECAY_FAST = 0.3
DECAY_SLOW = 1.5
MAX_DECAY = math.log(DECAY_TARGET) / DECAY_FAST
MIN_DECAY = math.log(DECAY_TARGET) / DECAY_SLOW
DN_ALPHA = (2.0 * DEPTH) ** 0.25
LN_EPS = 1e-5

LANES = 128
ATTN_BLOCKS = 4
FFT_BLOCKS = 4
COPY_UNROLL = 8
VMEM_LIMIT_BYTES = 48 * 1024 * 1024
BF16 = jnp.bfloat16
F32 = jnp.float32


def _layer_norm_rows(y, g, b):
    mu = jnp.mean(y, axis=-1, keepdims=True)
    c = y - mu
    var = jnp.mean(c * c, axis=-1, keepdims=True)
    return c * lax.rsqrt(var + LN_EPS) * g + b


def _seq_position(t0, seq_tokens):
    n_prompt, s_prompt, s_sample = seq_tokens
    seq_len = jnp.where(t0 < n_prompt, s_prompt, s_sample)
    return lax.rem(t0, seq_len), seq_len


def _ffn_kernel(x_ref, wg_ref, wu_ref, wo_ref, g_ref, b_ref, o_ref, xb_ref, *, nk):
    k = pl.program_id(1)

    @pl.when(k == 0)
    def _():
        xb_ref[...] = x_ref[...].astype(BF16)
        o_ref[...] = jnp.zeros_like(o_ref)

    xb = xb_ref[...]
    gate = jnp.dot(xb, wg_ref[...], preferred_element_type=F32)
    up = jnp.dot(xb, wu_ref[...], preferred_element_type=F32)
    act = (gate * jax.nn.sigmoid(gate) * up).astype(BF16)
    o_ref[...] += jnp.dot(act, wo_ref[...], preferred_element_type=F32)

    @pl.when(k == nk - 1)
    def _():
        y = DN_ALPHA * x_ref[...] + 0.5 * o_ref[...]
        o_ref[...] = _layer_norm_rows(y, g_ref[...], b_ref[...])


def _ffn_layer(x, w_in, w_out, g, b, *, tm=512, tf=512):
    t, dm = x.shape
    dff = w_out.shape[0]
    nk = dff // tf
    return pl.pallas_call(
        functools.partial(_ffn_kernel, nk=nk),
        grid=(t // tm, nk),
        in_specs=[
            pl.BlockSpec((tm, dm), lambda i, k: (i, 0)),
            pl.BlockSpec((dm, tf), lambda i, k: (0, k)),
            pl.BlockSpec((dm, tf), lambda i, k: (0, k + nk)),
            pl.BlockSpec((tf, dm), lambda i, k: (k, 0)),
            pl.BlockSpec((1, dm), lambda i, k: (0, 0)),
            pl.BlockSpec((1, dm), lambda i, k: (0, 0)),
        ],
        out_specs=pl.BlockSpec((tm, dm), lambda i, k: (i, 0)),
        out_shape=jax.ShapeDtypeStruct((t, dm), F32),
        scratch_shapes=[pltpu.VMEM((tm, dm), BF16)],
        compiler_params=pltpu.CompilerParams(
            dimension_semantics=("parallel", "arbitrary"),
            vmem_limit_bytes=VMEM_LIMIT_BYTES),
        name="ffn_ln",
    )(x, w_in, w_in, w_out, g.reshape(1, dm), b.reshape(1, dm))


def _proj_kernel(x_ref, w_ref, b_ref, o_ref, xb_ref):
    @pl.when(pl.program_id(1) == 0)
    def _():
        xb_ref[...] = x_ref[...].astype(BF16)

    acc = jnp.dot(xb_ref[...], w_ref[...], preferred_element_type=F32)
    o_ref[...] = (acc + b_ref[...]).astype(o_ref.dtype)


def _projection(x, w, bias, out_dtype, *, tm=1024, tn=1024):
    t, dm = x.shape
    n = w.shape[1]
    return pl.pallas_call(
        _proj_kernel,
        grid=(t // tm, n // tn),
        in_specs=[
            pl.BlockSpec((tm, dm), lambda i, j: (i, 0)),
            pl.BlockSpec((dm, tn), lambda i, j: (0, j)),
            pl.BlockSpec((1, tn), lambda i, j: (0, j)),
        ],
        out_specs=pl.BlockSpec((tm, tn), lambda i, j: (i, j)),
        out_shape=jax.ShapeDtypeStruct((t, n), out_dtype),
        scratch_shapes=[pltpu.VMEM((tm, dm), BF16)],
        compiler_params=pltpu.CompilerParams(
            dimension_semantics=("parallel", "arbitrary"),
            vmem_limit_bytes=VMEM_LIMIT_BYTES),
        name="projection",
    )(x, w, bias.reshape(1, n))


HALO_ROWS = 16


def _proj_conv_kernel(xp_ref, x_ref, xn_ref, w_ref, b_ref, cw_ref, cb_ref, o_ref, xb_ref, *, tm, seq_tokens):
    @pl.when(pl.program_id(1) == 0)
    def _():
        xb_ref[0:HALO_ROWS, :] = xp_ref[...].astype(BF16)
        xb_ref[HALO_ROWS:HALO_ROWS + tm, :] = x_ref[...].astype(BF16)
        xb_ref[HALO_ROWS + tm:, :] = xn_ref[...].astype(BF16)

    rows = tm + 2 * HALO_ROWS
    acc = jnp.dot(xb_ref[...], w_ref[...], preferred_element_type=F32) + b_ref[...]
    pos0, seq_len = _seq_position(pl.program_id(0) * tm, seq_tokens)
    tile_row = lax.broadcasted_iota(jnp.int32, (tm, 1), 0)
    at_seq_start = (tile_row == 0) & (pos0 == 0)
    at_seq_end = (tile_row == tm - 1) & (pos0 + tm == seq_len)
    cur = acc[HALO_ROWS:HALO_ROWS + tm]
    prev = pltpu.roll(acc, 1, axis=0)[HALO_ROWS:HALO_ROWS + tm]
    nxt = pltpu.roll(acc, rows - 1, axis=0)[HALO_ROWS:HALO_ROWS + tm]
    prev = jnp.where(at_seq_start, 0.0, prev)
    nxt = jnp.where(at_seq_end, 0.0, nxt)
    out = prev * cw_ref[0:1, :] + cur * cw_ref[1:2, :] + nxt * cw_ref[2:3, :] + cb_ref[...]
    for s in range(o_ref.shape[0]):
        o_ref[s] = out[:, s * LANES:(s + 1) * LANES]


def _projection_short_conv(x, w, bias, conv_w, conv_b, seq_tokens, *, tm=1024, tn=1024):
    t, dm = x.shape
    n = w.shape[1]
    ratio = tm // HALO_ROWS
    n_halo_blocks = t // HALO_ROWS
    return pl.pallas_call(
        functools.partial(_proj_conv_kernel, tm=tm, seq_tokens=seq_tokens),
        grid=(t // tm, n // tn),
        in_specs=[
            pl.BlockSpec((HALO_ROWS, dm), lambda i, j: (jnp.maximum(i * ratio - 1, 0), 0)),
            pl.BlockSpec((tm, dm), lambda i, j: (i, 0)),
            pl.BlockSpec((HALO_ROWS, dm), lambda i, j: (jnp.minimum((i + 1) * ratio, n_halo_blocks - 1), 0)),
            pl.BlockSpec((dm, tn), lambda i, j: (0, j)),
            pl.BlockSpec((1, tn), lambda i, j: (0, j)),
            pl.BlockSpec((SHORT_CONV, tn), lambda i, j: (0, j)),
            pl.BlockSpec((1, tn), lambda i, j: (0, j)),
        ],
        out_specs=pl.BlockSpec((tn // LANES, tm, LANES), lambda i, j: (j, i, 0)),
        out_shape=jax.ShapeDtypeStruct((n // LANES, t, LANES), F32),
        scratch_shapes=[pltpu.VMEM((tm + 2 * HALO_ROWS, dm), BF16)],
        compiler_params=pltpu.CompilerParams(
            dimension_semantics=("parallel", "arbitrary"),
            vmem_limit_bytes=VMEM_LIMIT_BYTES),
        name="projection_short_conv",
    )(x, x, x, w, bias.reshape(1, n), conv_w, conv_b.reshape(1, n))


def _proj_ln_kernel(x_ref, z_ref, w_ref, bias_ref, g_ref, b_ref, o_ref):
    if len(z_ref.shape) == 3:
        z = jnp.concatenate([z_ref[s].astype(BF16) for s in range(z_ref.shape[0])], axis=1)
    else:
        z = z_ref[...].astype(BF16)
    mix = jnp.dot(z, w_ref[...], preferred_element_type=F32)
    y = DN_ALPHA * x_ref[...] + (mix + bias_ref[...])
    o_ref[...] = _layer_norm_rows(y, g_ref[...], b_ref[...])


def _projection_ln(x, z, w, bias, g, b, *, tm=512):
    t, dm = x.shape
    if z.ndim == 3:
        kdim = z.shape[0] * LANES
        z_spec = pl.BlockSpec((z.shape[0], tm, LANES), lambda i: (0, i, 0))
    else:
        kdim = z.shape[1]
        z_spec = pl.BlockSpec((tm, kdim), lambda i: (i, 0))
    return pl.pallas_call(
        _proj_ln_kernel,
        grid=(t // tm,),
        in_specs=[
            pl.BlockSpec((tm, dm), lambda i: (i, 0)),
            z_spec,
            pl.BlockSpec((kdim, dm), lambda i: (0, 0)),
            pl.BlockSpec((1, dm), lambda i: (0, 0)),
            pl.BlockSpec((1, dm), lambda i: (0, 0)),
            pl.BlockSpec((1, dm), lambda i: (0, 0)),
        ],
        out_specs=pl.BlockSpec((tm, dm), lambda i: (i, 0)),
        out_shape=jax.ShapeDtypeStruct((t, dm), F32),
        compiler_params=pltpu.CompilerParams(
            dimension_semantics=("parallel",),
            vmem_limit_bytes=VMEM_LIMIT_BYTES),
        name="projection_ln",
    )(x, z, w, bias.reshape(1, dm), g.reshape(1, dm), b.reshape(1, dm))


def _attn_kernel(slopes_ref, *refs, tq, hg, seq_tokens):
    n_in = 7 * N_GROUPS
    in_refs = refs[:n_in]
    o_ref = refs[n_in]
    qf_ref, kf_ref, vf_ref, og_ref, lg_ref = refs[n_in + 1:]

    pos0, seq_len = _seq_position(pl.program_id(0) * tq, seq_tokens)
    scale = HEAD_DIM ** -0.5

    for hh in range(hg):
        cols = slice(hh * HEAD_DIM, (hh + 1) * HEAD_DIM)
        slope = slopes_ref[pl.program_id(1) * hg + hh]
        for g, (_, d) in enumerate(DILATED_GROUPS):
            q_ref, kp_ref, kc_ref, kn_ref, vp_ref, vc_ref, vn_ref = in_refs[7 * g:7 * g + 7]
            halo = HALF_WINDOW * d
            nq = tq // d
            sb = min(nq, 128)
            nkeys = sb + 2 * HALF_WINDOW
            chain_len = seq_len // d
            cpos0 = pos0 // d

            qf_ref[0:tq, :] = q_ref[:, cols].astype(F32)
            kf_ref[0:halo, :] = kp_ref[:, cols].astype(F32)
            kf_ref[halo:halo + tq, :] = kc_ref[:, cols].astype(F32)
            kf_ref[halo + tq:2 * halo + tq, :] = kn_ref[:, cols].astype(F32)
            vf_ref[0:halo, :] = vp_ref[:, cols].astype(F32)
            vf_ref[halo:halo + tq, :] = vc_ref[:, cols].astype(F32)
            vf_ref[halo + tq:2 * halo + tq, :] = vn_ref[:, cols].astype(F32)

            qi = lax.broadcasted_iota(jnp.int32, (sb, nkeys), 0)
            ki = lax.broadcasted_iota(jnp.int32, (sb, nkeys), 1)
            rel = ki - HALF_WINDOW - qi
            in_band = jnp.abs(rel) <= HALF_WINDOW
            alibi = (-slope) * (jnp.abs(rel) * d).astype(F32)

            def chain_blocks(it, carry, d=d, sb=sb, nkeys=nkeys, nq=nq, g=g,
                             in_band=in_band, alibi=alibi, ki=ki,
                             chain_len=chain_len, cpos0=cpos0):
                rows, scores = [], []
                for j in range(ATTN_BLOCKS):
                    idx = it * ATTN_BLOCKS + j
                    r = idx // (nq // sb)
                    s = idx % (nq // sb)
                    start = r + s * (sb * d)
                    if d == 1:
                        q_rows, k_rows = pl.ds(start, sb), pl.ds(start, nkeys)
                    else:
                        q_rows, k_rows = pl.ds(start, sb, stride=d), pl.ds(start, nkeys, stride=d)
                    qc = qf_ref[q_rows, :].astype(BF16)
                    kc = kf_ref[k_rows, :].astype(BF16)
                    sc = lax.dot_general(qc, kc, (((1,), (1,)), ((), ())), preferred_element_type=F32)
                    rows.append((q_rows, k_rows, s))
                    scores.append(sc)
                probs = []
                for (q_rows, k_rows, s), sc in zip(rows, scores):
                    kpos0 = cpos0 + s * sb - HALF_WINDOW
                    valid = in_band & (ki >= -kpos0) & (ki < chain_len - kpos0)
                    sc = jnp.where(valid, sc * scale + alibi, NEG_INF)
                    mx = jnp.max(sc, axis=-1, keepdims=True)
                    p = jnp.exp(sc - mx)
                    den = jnp.sum(p, axis=-1, keepdims=True)
                    probs.append((p.astype(BF16), mx, den))
                outs = [jnp.dot(p, vf_ref[k_rows, :].astype(BF16), preferred_element_type=F32)
                        for (p, _, _), (_, k_rows, _) in zip(probs, rows)]
                for (q_rows, _, _), (_, mx, den), o in zip(rows, probs, outs):
                    og_ref[g, q_rows, :] = o / den
                    lg_ref[g, q_rows, :] = jnp.broadcast_to(mx + jnp.log(den), (sb, HEAD_DIM))
                return carry

            lax.fori_loop(0, d * (nq // sb) // ATTN_BLOCKS, chain_blocks, 0)

        l0, l1, l2 = lg_ref[0], lg_ref[1], lg_ref[2]
        lmax = jnp.maximum(jnp.maximum(l0, l1), l2)
        w0, w1, w2 = jnp.exp(l0 - lmax), jnp.exp(l1 - lmax), jnp.exp(l2 - lmax)
        mixed = (w0 * og_ref[0] + w1 * og_ref[1] + w2 * og_ref[2]) / (w0 + w1 + w2)
        o_ref[:, cols] = mixed.astype(o_ref.dtype)


def _dilated_attention(qkv, slopes, seq_tokens, *, tq=1024, hg=4):
    t = qkv.shape[0]
    wcols = hg * HEAD_DIM
    blocks_per_part = ATTN_WIDTH // wcols
    in_specs = []
    args = []
    max_halo = HALF_WINDOW * max(d for _, d in DILATED_GROUPS)
    for g, (_, d) in enumerate(DILATED_GROUPS):
        halo = HALF_WINDOW * d
        ratio = tq // halo
        n_halo_blocks = t // halo

        def col(part, g=g):
            return lambda i, h, *_: (part * N_GROUPS + g) * blocks_per_part + h

        def cur(part, g=g):
            c = col(part, g)
            return pl.BlockSpec((tq, wcols), lambda i, h, *_: (i, c(i, h)))

        def prev(part, g=g, ratio=ratio):
            c = col(part, g)
            return pl.BlockSpec((halo, wcols),
                                lambda i, h, *_: (jnp.maximum(i * ratio - 1, 0), c(i, h)))

        def nxt(part, g=g, ratio=ratio, n_halo_blocks=n_halo_blocks):
            c = col(part, g)
            return pl.BlockSpec((halo, wcols),
                                lambda i, h, *_: (jnp.minimum((i + 1) * ratio, n_halo_blocks - 1), c(i, h)))

        in_specs += [cur(0), prev(1), cur(1), nxt(1), prev(2), cur(2), nxt(2)]
        args += [qkv] * 7

    grid_spec = pltpu.PrefetchScalarGridSpec(
        num_scalar_prefetch=1,
        grid=(t // tq, ATTN_WIDTH // wcols),
        in_specs=in_specs,
        out_specs=pl.BlockSpec((tq, wcols), lambda i, h, *_: (i, h)),
        scratch_shapes=[
            pltpu.VMEM((tq, HEAD_DIM), F32),
            pltpu.VMEM((tq + 2 * max_halo, HEAD_DIM), F32),
            pltpu.VMEM((tq + 2 * max_halo, HEAD_DIM), F32),
            pltpu.VMEM((N_GROUPS, tq, HEAD_DIM), F32),
            pltpu.VMEM((N_GROUPS, tq, HEAD_DIM), F32),
        ],
    )
    return pl.pallas_call(
        functools.partial(_attn_kernel, tq=tq, hg=hg, seq_tokens=seq_tokens),
        grid_spec=grid_spec,
        out_shape=jax.ShapeDtypeStruct((t, ATTN_WIDTH), BF16),
        compiler_params=pltpu.CompilerParams(
            dimension_semantics=("parallel", "arbitrary"),
            vmem_limit_bytes=VMEM_LIMIT_BYTES),
        name="dilated_attention",
    )(slopes, *args)


TWIDDLE_RADIX = 16
PITCH_PAD = 8


def _split_hi_lo_np(m):
    hi = m.astype(ml_dtypes.bfloat16)
    lo = (m - hi.astype(np.float64)).astype(ml_dtypes.bfloat16)
    return np.stack([hi, lo])


@functools.lru_cache(maxsize=None)
def _dft_constants(r):
    h = r // 2
    idx = np.arange(r, dtype=np.float64)
    ang = 2.0 * np.pi * np.outer(idx, idx) / r
    c, s = np.cos(ang), np.sin(ang)
    f1 = np.block([[c[:, :h], s[:, :h]], [-s[:, :h], c[:, :h]]])
    f1_real = np.concatenate([c, -s], axis=0)
    f2 = np.block([[c, s], [-s, c]])
    f2_inv = np.block([[c, -s], [s, c]])
    f1_inv = np.block([[c[:h, :], -s[:h, :]], [s[:h, :], c[:h, :]]])
    n = r * r
    coarse = np.arange(r // TWIDDLE_RADIX, dtype=np.float64)[:, None] * TWIDDLE_RADIX
    fine = np.arange(TWIDDLE_RADIX, dtype=np.float64)[:, None]

    def table(mult):
        a = 2.0 * np.pi * mult * idx[None, :] / n
        t = np.stack([np.cos(a), np.sin(a)], axis=1)
        return np.ascontiguousarray(np.broadcast_to(t[..., None], t.shape + (LANES,))).astype(np.float32)

    return dict(f1=_split_hi_lo_np(f1), f1_real=_split_hi_lo_np(f1_real), f2=_split_hi_lo_np(f2),
                f2_inv=_split_hi_lo_np(f2_inv), f1_inv=_split_hi_lo_np(f1_inv),
                t1=table(coarse), t2=table(fine))


def _split_hi_lo(data):
    d_hi = data.astype(BF16)
    return d_hi, (data - d_hi.astype(F32)).astype(BF16)


def _dot3(m_ref, d_hi, d_lo):
    m_hi, m_lo = m_ref[0], m_ref[1]
    return (jnp.dot(m_hi, d_hi, preferred_element_type=F32)
            + jnp.dot(m_hi, d_lo, preferred_element_type=F32)
            + jnp.dot(m_lo, d_hi, preferred_element_type=F32))


def _staged_loop(n_items, stages):
    def body(it, carry):
        state = [it * FFT_BLOCKS + j for j in range(FFT_BLOCKS)]
        for stage in stages:
            state = [stage(s) for s in state]
        return carry

    lax.fori_loop(0, n_items // FFT_BLOCKS, body, 0)


def _twiddle(t1_ref, t2_ref, idx):
    a = idx // TWIDDLE_RADIX
    b = idx % TWIDDLE_RADIX
    c1, s1 = t1_ref[a, 0], t1_ref[a, 1]
    c2, s2 = t2_ref[b, 0], t2_ref[b, 1]
    return c1 * c2 - s1 * s2, s1 * c2 + c1 * s2


def _lanes(x, half):
    return x[:, half * LANES:(half + 1) * LANES]


def _slab_rows(r, chunk):
    return r * (chunk + PITCH_PAD)


def _blocks_to_rows(src_ref, dst_ref, r):
    chunk = src_ref.shape[1]
    pitch = chunk + PITCH_PAD

    def body(g, carry):
        dst_ref[pl.ds(pl.multiple_of(g * pitch, 8), chunk), :] = src_ref[g]
        return carry

    lax.fori_loop(0, r, body, 0, unroll=COPY_UNROLL)


def _rows_to_blocks(src_ref, dst_ref, r):
    chunk = dst_ref.shape[1]
    pitch = chunk + PITCH_PAD

    def body(g, carry):
        dst_ref[g] = src_ref[pl.ds(pl.multiple_of(g * pitch, 8), chunk), :]
        return carry

    lax.fori_loop(0, r, body, 0, unroll=COPY_UNROLL)


def _stage_a(x_ref, x2d, chunk, first_n2, f1_ref, t1_ref, t2_ref, sre_ref, sim_ref, r):
    pitch = r + PITCH_PAD
    _blocks_to_rows(x_ref, x2d, r)

    def gather(jj):
        n2l = 2 * jj
        rhs = jnp.concatenate([x2d[pl.ds(n2l, r, stride=chunk + PITCH_PAD), :],
                               x2d[pl.ds(n2l + 1, r, stride=chunk + PITCH_PAD), :]], axis=1)
        return n2l, _split_hi_lo(rhs)

    def transform(state):
        n2l, (d_hi, d_lo) = state
        return n2l, _dot3(f1_ref, d_hi, d_lo)

    def twiddle_store(state):
        n2l, out = state
        for half in range(2):
            re, im = _lanes(out[:r], half), _lanes(out[r:], half)
            n2 = first_n2 + n2l + half
            c, s = _twiddle(t1_ref, t2_ref, n2)
            rows = pl.ds(n2, r, stride=pitch)
            sre_ref[rows, :] = re * c + im * s
            sim_ref[rows, :] = im * c - re * s

    _staged_loop(chunk // 2, [gather, transform, twiddle_store])


def _load_k1_pair(sre_ref, sim_ref, k1, r):
    pitch = r + PITCH_PAD
    r0 = pl.multiple_of(k1 * pitch, 8)
    r1 = pl.multiple_of((k1 + 1) * pitch, 8)
    rows = (pl.ds(r0, r), pl.ds(r1, r))
    rhs = jnp.concatenate([
        jnp.concatenate([sre_ref[rows[0], :], sre_ref[rows[1], :]], axis=1),
        jnp.concatenate([sim_ref[rows[0], :], sim_ref[rows[1], :]], axis=1)], axis=0)
    return rhs, rows


def _fft_conv_kernel(x_ref, gate_ref, cs_ref, f1_ref, f1i_ref, f2_ref, f2i_ref, t1_ref, t2_ref,
                     o_ref, sre_ref, sim_ref, x2d, g2d, o2d, *, r, ca, cb, cd):
    pitch = r + PITCH_PAD
    n_a, n_bc = r // ca, r // cb
    step = pl.program_id(2)

    @pl.when(step < n_a)
    def _():
        _stage_a(x_ref, x2d, ca, step * ca, f1_ref, t1_ref, t2_ref, sre_ref, sim_ref, r)

    @pl.when((step >= n_a) & (step < n_a + n_bc))
    def _():
        first_k1 = (step - n_a) * cb

        def load(jj):
            k1l = 2 * jj
            rhs, rows = _load_k1_pair(sre_ref, sim_ref, first_k1 + k1l, r)
            return k1l, rows, _split_hi_lo(rhs)

        def forward(state):
            k1l, rows, (d_hi, d_lo) = state
            return k1l, rows, _dot3(f2_ref, d_hi, d_lo)

        def filter_multiply(state):
            k1l, rows, x = state
            p_re, p_im = [], []
            for half in range(2):
                xr, xi = _lanes(x[:r], half), _lanes(x[r:], half)
                crow = pl.ds(pl.multiple_of((k1l + half) * r, 8), r)
                cr, ci = cs_ref[0, crow, :], cs_ref[1, crow, :]
                p_re.append(xr * cr - xi * ci)
                p_im.append(xr * ci + xi * cr)
            prod = jnp.concatenate([jnp.concatenate(p_re, axis=1), jnp.concatenate(p_im, axis=1)], axis=0)
            return k1l, rows, _split_hi_lo(prod)

        def inverse(state):
            k1l, rows, (d_hi, d_lo) = state
            return k1l, rows, _dot3(f2i_ref, d_hi, d_lo)

        def twiddle_store(state):
            k1l, rows, y = state
            for half in range(2):
                re, im = _lanes(y[:r], half), _lanes(y[r:], half)
                c, s = _twiddle(t1_ref, t2_ref, first_k1 + k1l + half)
                sre_ref[rows[half], :] = re * c - im * s
                sim_ref[rows[half], :] = im * c + re * s

        _staged_loop(cb // 2, [load, forward, filter_multiply, inverse, twiddle_store])

    @pl.when(step >= n_a + n_bc)
    def _():
        first_n2 = (step - n_a - n_bc) * cd
        _blocks_to_rows(gate_ref, g2d, r)

        def gather(jj):
            n2l = 2 * jj
            n2 = first_n2 + n2l
            rows0 = pl.ds(n2, r, stride=pitch)
            rows1 = pl.ds(n2 + 1, r, stride=pitch)
            rhs = jnp.concatenate([
                jnp.concatenate([sre_ref[rows0, :], sre_ref[rows1, :]], axis=1),
                jnp.concatenate([sim_ref[rows0, :], sim_ref[rows1, :]], axis=1)], axis=0)
            return n2l, _split_hi_lo(rhs)

        def transform(state):
            n2l, (d_hi, d_lo) = state
            return n2l, _dot3(f1i_ref, d_hi, d_lo)

        def gate_store(state):
            n2l, y = state
            for half in range(2):
                tok = pl.ds(n2l + half, r, stride=cd + PITCH_PAD)
                o2d[tok, :] = g2d[tok, :] * _lanes(y, half)

        _staged_loop(cd // 2, [gather, transform, gate_store])
        _rows_to_blocks(o2d, o_ref, r)


def _fft_conv_kernel_aliased(*refs, **kw):
    _fft_conv_kernel(*refs[1:], **kw)


def _fft_conv(x, x_col0, gate, gate_col0, row0, spectrum, spec_col0, r, n_pairs, dm, out_buf=None,
              *, ca=32, cb=16, cd=32):
    n_a, n_bc, n_d = r // ca, r // cb, r // cd
    n_slabs = dm // LANES
    consts = _dft_constants(r)
    t = x.shape[1]
    xv = x.reshape(x.shape[0], t // r, r, LANES)
    gv = gate.reshape(gate.shape[0], t // r, r, LANES)
    blk0 = row0 // (r * r)

    def d_step(st):
        return jnp.clip(st - n_a - n_bc, 0, n_d - 1)

    def full(a):
        return pl.BlockSpec(a.shape, lambda s, p, st, nd=a.ndim: (0,) * nd)

    const_args = [consts["f1"], consts["f1_inv"], consts["f2"], consts["f2_inv"], consts["t1"], consts["t2"]]
    in_specs = [
        pl.BlockSpec((None, r, ca, LANES), lambda s, p, st: (x_col0 + s, blk0 + p, jnp.minimum(st, n_a - 1), 0)),
        pl.BlockSpec((None, r, cd, LANES), lambda s, p, st: (gate_col0 + s, blk0 + p, d_step(st), 0)),
        pl.BlockSpec((None, 2, cb * r, LANES),
                     lambda s, p, st: (spec_col0 + s, 0, jnp.clip(st - n_a, 0, n_bc - 1), 0)),
    ] + [full(a) for a in const_args]
    args = [xv, gv, spectrum, *const_args]
    body = _fft_conv_kernel
    aliases = {}
    if out_buf is not None:
        in_specs = [pl.BlockSpec(memory_space=pl.ANY)] + in_specs
        args = [out_buf.reshape(n_slabs, t // r, r, LANES)] + args
        body = _fft_conv_kernel_aliased
        aliases = {0: 0}
    out = pl.pallas_call(
        functools.partial(body, r=r, ca=ca, cb=cb, cd=cd),
        grid=(n_slabs, n_pairs, n_a + n_bc + n_d),
        in_specs=in_specs,
        out_specs=pl.BlockSpec((None, r, cd, LANES), lambda s, p, st: (s, blk0 + p, d_step(st), 0)),
        out_shape=jax.ShapeDtypeStruct((n_slabs, t // r, r, LANES), F32),
        scratch_shapes=[pltpu.VMEM((r * (r + PITCH_PAD), LANES), F32),
                        pltpu.VMEM((r * (r + PITCH_PAD), LANES), F32),
                        pltpu.VMEM((_slab_rows(r, ca), LANES), F32),
                        pltpu.VMEM((_slab_rows(r, cd), LANES), F32),
                        pltpu.VMEM((_slab_rows(r, cd), LANES), F32)],
        input_output_aliases=aliases,
        compiler_params=pltpu.CompilerParams(
            dimension_semantics=("parallel", "arbitrary", "arbitrary"),
            vmem_limit_bytes=VMEM_LIMIT_BYTES),
        name="fft_conv",
    )(*args)
    return out.reshape(n_slabs, t, LANES)


def _fft_spectrum_kernel(c_ref, scale_ref, shift_ref, f1_ref, f2_ref, t1_ref, t2_ref, o_ref, sre_ref, sim_ref, c2d,
                         *, r, ca, cb):
    n_a = r // ca
    step = pl.program_id(2)

    @pl.when(step < n_a)
    def _():
        _stage_a(c_ref, c2d, ca, step * ca, f1_ref, t1_ref, t2_ref, sre_ref, sim_ref, r)

    @pl.when(step >= n_a)
    def _():
        first_k1 = (step - n_a) * cb

        def load(jj):
            k1l = 2 * jj
            rhs, _ = _load_k1_pair(sre_ref, sim_ref, first_k1 + k1l, r)
            return k1l, _split_hi_lo(rhs)

        def forward(state):
            k1l, (d_hi, d_lo) = state
            return k1l, _dot3(f2_ref, d_hi, d_lo)

        def store(state):
            k1l, x = state
            for half in range(2):
                orow = pl.ds(pl.multiple_of((k1l + half) * r, 8), r)
                o_ref[0, orow, :] = _lanes(x[:r], half) * scale_ref[...] + shift_ref[...]
                o_ref[1, orow, :] = _lanes(x[r:], half) * scale_ref[...]

        _staged_loop(cb // 2, [load, forward, store])


def _fft_spectrum(c, scale, shift, r, *, ca=32, cb=16):
    n_slabs, n, _ = c.shape
    width = n_slabs * LANES
    n_a, n_bc = r // ca, r // cb
    consts = _dft_constants(r)
    cv = c.reshape(n_slabs, r, r, LANES)

    def full(a):
        return pl.BlockSpec(a.shape, lambda s, o, st, nd=a.ndim: (0,) * nd)

    const_args = [consts["f1_real"], consts["f2"], consts["t1"], consts["t2"]]
    return pl.pallas_call(
        functools.partial(_fft_spectrum_kernel, r=r, ca=ca, cb=cb),
        grid=(n_slabs, 1, n_a + n_bc),
        in_specs=[pl.BlockSpec((None, r, ca, LANES), lambda s, o, st: (s, 0, jnp.minimum(st, n_a - 1), 0)),
                  pl.BlockSpec((1, LANES), lambda s, o, st: (0, s)),
                  pl.BlockSpec((1, LANES), lambda s, o, st: (0, s))]
        + [full(a) for a in const_args],
        out_specs=pl.BlockSpec((None, 2, cb * r, LANES), lambda s, o, st: (s, 0, jnp.maximum(st - n_a, 0), 0)),
        out_shape=jax.ShapeDtypeStruct((n_slabs, 2, n, LANES), F32),
        scratch_shapes=[pltpu.VMEM((r * (r + PITCH_PAD), LANES), F32),
                        pltpu.VMEM((r * (r + PITCH_PAD), LANES), F32),
                        pltpu.VMEM((_slab_rows(r, ca), LANES), F32)],
        compiler_params=pltpu.CompilerParams(
            dimension_semantics=("parallel", "arbitrary", "arbitrary"),
            vmem_limit_bytes=VMEM_LIMIT_BYTES),
        name="fft_spectrum",
    )(cv, scale.reshape(1, width), shift.reshape(1, width), *const_args)


FILTER_EMB = 2 * FILTER_BANDS + 1


def _dot_f32(a, b):
    a_hi, a_lo = _split_hi_lo(a)
    b_hi, b_lo = _split_hi_lo(b)
    return (jnp.dot(a_hi, b_hi, preferred_element_type=F32)
            + jnp.dot(a_hi, b_lo, preferred_element_type=F32)
            + jnp.dot(a_lo, b_hi, preferred_element_type=F32))


def _filter_kernel(bands_ref, w1_ref, b1_ref, w2_ref, b2_ref, freq_ref, w3f_ref, w3b_ref, delta_ref,
                   c_ref, norm_ref, hid_ref, *, length, tr):
    j = pl.program_id(0)
    i = pl.program_id(1)
    row = i * tr + lax.broadcasted_iota(jnp.int32, (tr, 1), 0)
    pos_idx = jnp.where(row < length, row, 2 * length - row)
    mf = pos_idx.astype(F32)
    t = mf / (length - 1)
    rows = pl.ds(pl.multiple_of(i * tr, 8), tr)

    @pl.when(j == 0)
    def _():
        w = 2.0 * math.pi * mf / length
        arg = w * bands_ref[...]
        lane = lax.broadcasted_iota(jnp.int32, (1, LANES), 1)
        pos = jnp.where(lane == 0, t,
                        jnp.where(lane <= FILTER_BANDS, jnp.cos(arg),
                                  jnp.where(lane < FILTER_EMB, -jnp.sin(arg), 0.0)))
        h = jnp.sin(freq_ref[...] * (_dot_f32(pos, w1_ref[...]) + b1_ref[...]))
        h = jnp.sin(freq_ref[...] * (_dot_f32(h, w2_ref[...]) + b2_ref[...]))
        hid_ref[rows, :] = h

    h = hid_ref[rows, :]
    decay = jnp.exp(-t * delta_ref[...])
    fwd = _dot_f32(h, w3f_ref[...]) * decay
    bwd = _dot_f32(h, w3b_ref[...]) * decay
    main = jnp.where(row < length, fwd, jnp.where(row == length, 0.0, bwd))
    extra = jnp.where(row == 0, bwd, 0.0)
    taps = main + extra
    for s in range(c_ref.shape[0]):
        c_ref[s] = taps[:, s * LANES:(s + 1) * LANES]
    contrib = jnp.abs(main) + jnp.abs(extra)
    part = contrib[0:8]
    for k in range(1, tr // 8):
        part = part + contrib[8 * k:8 * (k + 1)]

    @pl.when(i == 0)
    def _():
        norm_ref[...] = part

    @pl.when(i > 0)
    def _():
        norm_ref[...] += part


def _hyena_filter_kernel(length, f_w1, f_b1, f_w2, f_b2, f_freq, f_w3, *, tr=512, tc=1024):
    hidden = f_w1.shape[1]
    dm = f_w3.shape[1] // (2 * HYENA_ORDER)
    width = HYENA_ORDER * dm
    tr, tc = min(tr, 2 * length), min(tc, width)
    bands = jnp.linspace(1e-4, FILTER_BANDS - 1, FILTER_BANDS, dtype=F32)
    bands_row = jnp.zeros((1, LANES), F32).at[0, 1:FILTER_BANDS + 1].set(bands)
    bands_row = bands_row.at[0, FILTER_BANDS + 1:FILTER_EMB].set(bands)

    def pad_to(a, rows, cols):
        return jnp.zeros((rows, cols), F32).at[:a.shape[0], :a.shape[1]].set(a)

    w3 = f_w3.reshape(hidden, HYENA_ORDER, 2, dm)
    w3f = pad_to(w3[:, :, 0].reshape(hidden, width), LANES, width)
    w3b = pad_to(w3[:, :, 1].reshape(hidden, width), LANES, width)
    deltas = jnp.abs(jnp.linspace(MIN_DECAY, MAX_DECAY, dm, dtype=F32))
    delta_row = jnp.tile(deltas, HYENA_ORDER).reshape(1, width)
    small = [bands_row, pad_to(f_w1, LANES, LANES), pad_to(f_b1[None], 1, LANES), pad_to(f_w2, LANES, LANES),
             pad_to(f_b2[None], 1, LANES), pad_to(f_freq[None], 1, LANES)]
    c, norm = pl.pallas_call(
        functools.partial(_filter_kernel, length=length, tr=tr),
        grid=(width // tc, 2 * length // tr),
        in_specs=[pl.BlockSpec(a.shape, lambda j, i: (0, 0)) for a in small] + [
            pl.BlockSpec((LANES, tc), lambda j, i: (0, j)),
            pl.BlockSpec((LANES, tc), lambda j, i: (0, j)),
            pl.BlockSpec((1, tc), lambda j, i: (0, j)),
        ],
        out_specs=[pl.BlockSpec((tc // LANES, tr, LANES), lambda j, i: (j, i, 0)),
                   pl.BlockSpec((8, tc), lambda j, i: (0, j))],
        out_shape=[jax.ShapeDtypeStruct((width // LANES, 2 * length, LANES), F32),
                   jax.ShapeDtypeStruct((8, width), F32)],
        scratch_shapes=[pltpu.VMEM((2 * length, LANES), F32)],
        compiler_params=pltpu.CompilerParams(
            dimension_semantics=("arbitrary", "arbitrary"),
            vmem_limit_bytes=VMEM_LIMIT_BYTES),
        name="hyena_filter",
    )(*small, w3f, w3b, delta_row)
    return c, jnp.sum(norm, axis=0)


def _filter_spectra(length, r, filt_params, skip):
    c, norm = _hyena_filter_kernel(length, *filt_params)
    n = 2 * length
    return _fft_spectrum(c, 1.0 / (n * norm), skip.reshape(-1) / n, r)


def _hyena_mixer(u, filt_params, skip, seq_tokens, batches):
    n_prompt, s_prompt, s_sample = seq_tokens
    slabs = u.shape[0] // 3
    dm = slabs * LANES
    z1 = z2 = None
    for seq_len, n_batch, row0 in ((s_prompt, batches[0], 0), (s_sample, batches[1], n_prompt)):
        r = math.isqrt(2 * seq_len)
        assert r * r == 2 * seq_len and n_batch % 2 == 0 and row0 % (r * r) == 0
        spec = _filter_spectra(seq_len, r, filt_params, skip)
        chunk = 16 if r > 64 else 32
        z1 = _fft_conv(u, 0, u, slabs, row0, spec, 0, r, n_batch // 2, dm, z1, ca=chunk, cd=chunk)
        z2 = _fft_conv(z1, 0, u, 2 * slabs, row0, spec, slabs, r, n_batch // 2, dm, z2, ca=chunk, cd=chunk)
    return z2


def kernel(x_prompt, x_sample, ln_g, ln_b, ffn_w_in, ffn_w_out, attn_w_qkv, attn_w_o, hy_w_in, hy_b_in,
           hy_conv_w, hy_conv_b, hy_f_w1, hy_f_b1, hy_f_w2, hy_f_b2, hy_f_freq, hy_f_w3, hy_skip,
           hy_w_out, hy_b_out):
    bp, sp, dm = x_prompt.shape
    bs, ss, _ = x_sample.shape
    n_prompt = bp * sp
    n_sample = bs * ss
    seq_tokens = (n_prompt, sp, ss)
    x = jnp.concatenate([x_prompt.reshape(n_prompt, dm), x_sample.reshape(n_sample, dm)], axis=0)
    slopes = jnp.exp2(-8.0 * jnp.arange(1, N_SLOTS + 1, dtype=F32) / N_SLOTS)
    zero_bias_qkv = jnp.zeros((QKV_WIDTH,), F32)
    zero_bias_dm = jnp.zeros((dm,), F32)

    for i in range(DEPTH):
        x = _ffn_layer(x, ffn_w_in[i, 0].astype(BF16), ffn_w_out[i, 0].astype(BF16), ln_g[i, 0], ln_b[i, 0])
        j = i // 2
        if i % 2 == 0:
            qkv = _projection(x, attn_w_qkv[j].astype(BF16), zero_bias_qkv, BF16)
            att = _dilated_attention(qkv, slopes, seq_tokens)
            x = _projection_ln(x, att, attn_w_o[j].astype(BF16), zero_bias_dm, ln_g[i, 1], ln_b[i, 1])
        else:
            u = _projection_short_conv(x, hy_w_in[j].astype(BF16), hy_b_in[j], hy_conv_w[j], hy_conv_b[j],
                                       seq_tokens)
            filt = (hy_f_w1[j], hy_f_b1[j], hy_f_w2[j], hy_f_b2[j], hy_f_freq[j], hy_f_w3[j])
            z = _hyena_mixer(u, filt, hy_skip[j], seq_tokens, (bp, bs))
            x = _projection_ln(x, z, hy_w_out[j].astype(BF16), hy_b_out[j], ln_g[i, 1], ln_b[i, 1])
        x = _ffn_layer(x, ffn_w_in[i, 1].astype(BF16), ffn_w_out[i, 1].astype(BF16), ln_g[i, 2], ln_b[i, 2])

    return (x[:n_prompt].reshape(bp, sp, dm), x[n_prompt:].reshape(bs, ss, dm))
```

```python
import functools
import math

import jax
import jax.numpy as jnp
import ml_dtypes
import numpy as np
from jax import lax
from jax.experimental import pallas as pl
from jax.experimental.pallas import tpu as pltpu

D_MODEL = 2048
DEPTH = 4
HEAD_DIM = 128
N_SLOTS = D_MODEL // HEAD_DIM
DILATED_GROUPS = ((128, 1), (512, 4), (2048, 16))
N_GROUPS = len(DILATED_GROUPS)
ATTN_WIDTH = N_SLOTS * HEAD_DIM
QKV_WIDTH = 3 * N_GROUPS * ATTN_WIDTH
HALF_WINDOW = 64
NEG_INF = -1e30
D_FF = 5632
HYENA_ORDER = 2
SHORT_CONV = 3
FILTER_BANDS = 16
DECAY_TARGET = 1e-2
DECAY_FAST = 0.3
DECAY_SLOW = 1.5
MAX_DECAY = math.log(DECAY_TARGET) / DECAY_FAST
MIN_DECAY = math.log(DECAY_TARGET) / DECAY_SLOW
DN_ALPHA = (2.0 * DEPTH) ** 0.25
LN_EPS = 1e-5

LANES = 128
ATTN_BLOCKS = 4
FFT_BLOCKS = 4
COPY_UNROLL = 8
VMEM_LIMIT_BYTES = 48 * 1024 * 1024
BF16 = jnp.bfloat16
F32 = jnp.float32


def _layer_norm_rows(y, g, b):
    mu = jnp.mean(y, axis=-1, keepdims=True)
    c = y - mu
    var = jnp.mean(c * c, axis=-1, keepdims=True)
    return c * lax.rsqrt(var + LN_EPS) * g + b


def _seq_position(t0, seq_tokens):
    n_prompt, s_prompt, s_sample = seq_tokens
    seq_len = jnp.where(t0 < n_prompt, s_prompt, s_sample)
    return lax.rem(t0, seq_len), seq_len


def _ffn_kernel(x_ref, wg_ref, wu_ref, wo_ref, g_ref, b_ref, o_ref, xb_ref, *, nk):
    k = pl.program_id(1)

    @pl.when(k == 0)
    def _():
        xb_ref[...] = x_ref[...].astype(BF16)
        o_ref[...] = jnp.zeros_like(o_ref)

    xb = xb_ref[...]
    gate = jnp.dot(xb, wg_ref[...], preferred_element_type=F32)
    up = jnp.dot(xb, wu_ref[...], preferred_element_type=F32)
    act = (gate * jax.nn.sigmoid(gate) * up).astype(BF16)
    o_ref[...] += jnp.dot(act, wo_ref[...], preferred_element_type=F32)

    @pl.when(k == nk - 1)
    def _():
        y = DN_ALPHA * x_ref[...] + 0.5 * o_ref[...]
        o_ref[...] = _layer_norm_rows(y, g_ref[...], b_ref[...])


def _ffn_layer(x, w_in, w_out, g, b, *, tm=512, tf=512):
    t, dm = x.shape
    dff = w_out.shape[0]
    nk = dff // tf
    return pl.pallas_call(
        functools.partial(_ffn_kernel, nk=nk),
        grid=(t // tm, nk),
        in_specs=[
            pl.BlockSpec((tm, dm), lambda i, k: (i, 0)),
            pl.BlockSpec((dm, tf), lambda i, k: (0, k)),
            pl.BlockSpec((dm, tf), lambda i, k: (0, k + nk)),
            pl.BlockSpec((tf, dm), lambda i, k: (k, 0)),
            pl.BlockSpec((1, dm), lambda i, k: (0, 0)),
            pl.BlockSpec((1, dm), lambda i, k: (0, 0)),
        ],
        out_specs=pl.BlockSpec((tm, dm), lambda i, k: (i, 0)),
        out_shape=jax.ShapeDtypeStruct((t, dm), F32),
        scratch_shapes=[pltpu.VMEM((tm, dm), BF16)],
        compiler_params=pltpu.CompilerParams(
            dimension_semantics=("parallel", "arbitrary"),
            vmem_limit_bytes=VMEM_LIMIT_BYTES),
        name="ffn_ln",
    )(x, w_in, w_in, w_out, g.reshape(1, dm), b.reshape(1, dm))


def _proj_kernel(x_ref, w_ref, b_ref, o_ref, xb_ref):
    @pl.when(pl.program_id(1) == 0)
    def _():
        xb_ref[...] = x_ref[...].astype(BF16)

    acc = jnp.dot(xb_ref[...], w_ref[...], preferred_element_type=F32)
    o_ref[...] = (acc + b_ref[...]).astype(o_ref.dtype)


def _projection(x, w, bias, out_dtype, *, tm=1024, tn=1024):
    t, dm = x.shape
    n = w.shape[1]
    return pl.pallas_call(
        _proj_kernel,
        grid=(t // tm, n // tn),
        in_specs=[
            pl.BlockSpec((tm, dm), lambda i, j: (i, 0)),
            pl.BlockSpec((dm, tn), lambda i, j: (0, j)),
            pl.BlockSpec((1, tn), lambda i, j: (0, j)),
        ],
        out_specs=pl.BlockSpec((tm, tn), lambda i, j: (i, j)),
        out_shape=jax.ShapeDtypeStruct((t, n), out_dtype),
        scratch_shapes=[pltpu.VMEM((tm, dm), BF16)],
        compiler_params=pltpu.CompilerParams(
            dimension_semantics=("parallel", "arbitrary"),
            vmem_limit_bytes=VMEM_LIMIT_BYTES),
        name="projection",
    )(x, w, bias.reshape(1, n))


HALO_ROWS = 16


def _proj_conv_kernel(xp_ref, x_ref, xn_ref, w_ref, b_ref, cw_ref, cb_ref, o_ref, xb_ref, *, tm, seq_tokens):
    @pl.when(pl.program_id(1) == 0)
    def _():
        xb_ref[0:HALO_ROWS, :] = xp_ref[...].astype(BF16)
        xb_ref[HALO_ROWS:HALO_ROWS + tm, :] = x_ref[...].astype(BF16)
        xb_ref[HALO_ROWS + tm:, :] = xn_ref[...].astype(BF16)

    rows = tm + 2 * HALO_ROWS
    acc = jnp.dot(xb_ref[...], w_ref[...], preferred_element_type=F32) + b_ref[...]
    pos0, seq_len = _seq_position(pl.program_id(0) * tm, seq_tokens)
    tile_row = lax.broadcasted_iota(jnp.int32, (tm, 1), 0)
    at_seq_start = (tile_row == 0) & (pos0 == 0)
    at_seq_end = (tile_row == tm - 1) & (pos0 + tm == seq_len)
    cur = acc[HALO_ROWS:HALO_ROWS + tm]
    prev = pltpu.roll(acc, 1, axis=0)[HALO_ROWS:HALO_ROWS + tm]
    nxt = pltpu.roll(acc, rows - 1, axis=0)[HALO_ROWS:HALO_ROWS + tm]
    prev = jnp.where(at_seq_start, 0.0, prev)
    nxt = jnp.where(at_seq_end, 0.0, nxt)
    out = prev * cw_ref[0:1, :] + cur * cw_ref[1:2, :] + nxt * cw_ref[2:3, :] + cb_ref[...]
    for s in range(o_ref.shape[0]):
        o_ref[s] = out[:, s * LANES:(s + 1) * LANES]


def _projection_short_conv(x, w, bias, conv_w, conv_b, seq_tokens, *, tm=1024, tn=1024):
    t, dm = x.shape
    n = w.shape[1]
    ratio = tm // HALO_ROWS
    n_halo_blocks = t // HALO_ROWS
    return pl.pallas_call(
        functools.partial(_proj_conv_kernel, tm=tm, seq_tokens=seq_tokens),
        grid=(t // tm, n // tn),
        in_specs=[
            pl.BlockSpec((HALO_ROWS, dm), lambda i, j: (jnp.maximum(i * ratio - 1, 0), 0)),
            pl.BlockSpec((tm, dm), lambda i, j: (i, 0)),
            pl.BlockSpec((HALO_ROWS, dm), lambda i, j: (jnp.minimum((i + 1) * ratio, n_halo_blocks - 1), 0)),
            pl.BlockSpec((dm, tn), lambda i, j: (0, j)),
            pl.BlockSpec((1, tn), lambda i, j: (0, j)),
            pl.BlockSpec((SHORT_CONV, tn), lambda i, j: (0, j)),
            pl.BlockSpec((1, tn), lambda i, j: (0, j)),
        ],
        out_specs=pl.BlockSpec((tn // LANES, tm, LANES), lambda i, j: (j, i, 0)),
        out_shape=jax.ShapeDtypeStruct((n // LANES, t, LANES), F32),
        scratch_shapes=[pltpu.VMEM((tm + 2 * HALO_ROWS, dm), BF16)],
        compiler_params=pltpu.CompilerParams(
            dimension_semantics=("parallel", "arbitrary"),
            vmem_limit_bytes=VMEM_LIMIT_BYTES),
        name="projection_short_conv",
    )(x, x, x, w, bias.reshape(1, n), conv_w, conv_b.reshape(1, n))


def _proj_ln_kernel(x_ref, z_ref, w_ref, bias_ref, g_ref, b_ref, o_ref):
    if len(z_ref.shape) == 3:
        z = jnp.concatenate([z_ref[s].astype(BF16) for s in range(z_ref.shape[0])], axis=1)
    else:
        z = z_ref[...].astype(BF16)
    mix = jnp.dot(z, w_ref[...], preferred_element_type=F32)
    y = DN_ALPHA * x_ref[...] + (mix + bias_ref[...])
    o_ref[...] = _layer_norm_rows(y, g_ref[...], b_ref[...])


def _projection_ln(x, z, w, bias, g, b, *, tm=512):
    t, dm = x.shape
    if z.ndim == 3:
        kdim = z.shape[0] * LANES
        z_spec = pl.BlockSpec((z.shape[0], tm, LANES), lambda i: (0, i, 0))
    else:
        kdim = z.shape[1]
        z_spec = pl.BlockSpec((tm, kdim), lambda i: (i, 0))
    return pl.pallas_call(
        _proj_ln_kernel,
        grid=(t // tm,),
        in_specs=[
            pl.BlockSpec((tm, dm), lambda i: (i, 0)),
            z_spec,
            pl.BlockSpec((kdim, dm), lambda i: (0, 0)),
            pl.BlockSpec((1, dm), lambda i: (0, 0)),
            pl.BlockSpec((1, dm), lambda i: (0, 0)),
            pl.BlockSpec((1, dm), lambda i: (0, 0)),
        ],
        out_specs=pl.BlockSpec((tm, dm), lambda i: (i, 0)),
        out_shape=jax.ShapeDtypeStruct((t, dm), F32),
        compiler_params=pltpu.CompilerParams(
            dimension_semantics=("parallel",),
            vmem_limit_bytes=VMEM_LIMIT_BYTES),
        name="projection_ln",
    )(x, z, w, bias.reshape(1, dm), g.reshape(1, dm), b.reshape(1, dm))


def _attn_kernel(slopes_ref, *refs, tq, hg, seq_tokens):
    n_in = 7 * N_GROUPS
    in_refs = refs[:n_in]
    o_ref = refs[n_in]
    qf_ref, kf_ref, vf_ref, og_ref, lg_ref = refs[n_in + 1:]

    pos0, seq_len = _seq_position(pl.program_id(0) * tq, seq_tokens)
    scale = HEAD_DIM ** -0.5

    for hh in range(hg):
        cols = slice(hh * HEAD_DIM, (hh + 1) * HEAD_DIM)
        slope = slopes_ref[pl.program_id(1) * hg + hh]
        for g, (_, d) in enumerate(DILATED_GROUPS):
            q_ref, kp_ref, kc_ref, kn_ref, vp_ref, vc_ref, vn_ref = in_refs[7 * g:7 * g + 7]
            halo = HALF_WINDOW * d
            nq = tq // d
            sb = min(nq, 128)
            nkeys = sb + 2 * HALF_WINDOW
            chain_len = seq_len // d
            cpos0 = pos0 // d

            qf_ref[0:tq, :] = q_ref[:, cols].astype(F32)
            kf_ref[0:halo, :] = kp_ref[:, cols].astype(F32)
            kf_ref[halo:halo + tq, :] = kc_ref[:, cols].astype(F32)
            kf_ref[halo + tq:2 * halo + tq, :] = kn_ref[:, cols].astype(F32)
            vf_ref[0:halo, :] = vp_ref[:, cols].astype(F32)
            vf_ref[halo:halo + tq, :] = vc_ref[:, cols].astype(F32)
            vf_ref[halo + tq:2 * halo + tq, :] = vn_ref[:, cols].astype(F32)

            qi = lax.broadcasted_iota(jnp.int32, (sb, nkeys), 0)
            ki = lax.broadcasted_iota(jnp.int32, (sb, nkeys), 1)
            rel = ki - HALF_WINDOW - qi
            band_bias = jnp.where(jnp.abs(rel) <= HALF_WINDOW,
                                  (-slope) * (jnp.abs(rel) * d).astype(F32), NEG_INF)
            key_col = lax.broadcasted_iota(jnp.int32, (1, nkeys), 1)

            def chain_blocks(it, carry, d=d, sb=sb, nkeys=nkeys, nq=nq, g=g,
                             band_bias=band_bias, key_col=key_col,
                             chain_len=chain_len, cpos0=cpos0):
                rows, scores = [], []
                for j in range(ATTN_BLOCKS):
                    idx = it * ATTN_BLOCKS + j
                    r = idx // (nq // sb)
                    s = idx % (nq // sb)
                    start = r + s * (sb * d)
                    if d == 1:
                        q_rows, k_rows = pl.ds(start, sb), pl.ds(start, nkeys)
                    else:
                        q_rows, k_rows = pl.ds(start, sb, stride=d), pl.ds(start, nkeys, stride=d)
                    qc = qf_ref[q_rows, :].astype(BF16)
                    kc = kf_ref[k_rows, :].astype(BF16)
                    sc = lax.dot_general(qc, kc, (((1,), (1,)), ((), ())), preferred_element_type=F32)
                    rows.append((q_rows, k_rows, s))
                    scores.append(sc)
                probs = []
                for (q_rows, k_rows, s), sc in zip(rows, scores):
                    kpos0 = cpos0 + s * sb - HALF_WINDOW
                    in_seq = (key_col >= -kpos0) & (key_col < chain_len - kpos0)
                    sc = sc * scale + band_bias + jnp.where(in_seq, 0.0, NEG_INF)
                    mx = jnp.max(sc, axis=-1, keepdims=True)
                    p = jnp.exp(sc - mx)
                    den = jnp.sum(p, axis=-1, keepdims=True)
                    probs.append((p.astype(BF16), mx, den))
                outs = [jnp.dot(p, vf_ref[k_rows, :].astype(BF16), preferred_element_type=F32)
                        for (p, _, _), (_, k_rows, _) in zip(probs, rows)]
                for (q_rows, _, _), (_, mx, den), o in zip(rows, probs, outs):
                    og_ref[g, q_rows, :] = o / den
                    lg_ref[g, q_rows, :] = jnp.broadcast_to(mx + jnp.log(den), (sb, HEAD_DIM))
                return carry

            lax.fori_loop(0, d * (nq // sb) // ATTN_BLOCKS, chain_blocks, 0)

        l0, l1, l2 = lg_ref[0], lg_ref[1], lg_ref[2]
        lmax = jnp.maximum(jnp.maximum(l0, l1), l2)
        w0, w1, w2 = jnp.exp(l0 - lmax), jnp.exp(l1 - lmax), jnp.exp(l2 - lmax)
        mixed = (w0 * og_ref[0] + w1 * og_ref[1] + w2 * og_ref[2]) / (w0 + w1 + w2)
        o_ref[:, cols] = mixed.astype(o_ref.dtype)


def _dilated_attention(qkv, slopes, seq_tokens, *, tq=1024, hg=4):
    t = qkv.shape[0]
    wcols = hg * HEAD_DIM
    blocks_per_part = ATTN_WIDTH // wcols
    in_specs = []
    args = []
    max_halo = HALF_WINDOW * max(d for _, d in DILATED_GROUPS)
    for g, (_, d) in enumerate(DILATED_GROUPS):
        halo = HALF_WINDOW * d
        ratio = tq // halo
        n_halo_blocks = t // halo

        def col(part, g=g):
            return lambda i, h, *_: (part * N_GROUPS + g) * blocks_per_part + h

        def cur(part, g=g):
            c = col(part, g)
            return pl.BlockSpec((tq, wcols), lambda i, h, *_: (i, c(i, h)))

        def prev(part, g=g, ratio=ratio):
            c = col(part, g)
            return pl.BlockSpec((halo, wcols),
                                lambda i, h, *_: (jnp.maximum(i * ratio - 1, 0), c(i, h)))

        def nxt(part, g=g, ratio=ratio, n_halo_blocks=n_halo_blocks):
            c = col(part, g)
            return pl.BlockSpec((halo, wcols),
                                lambda i, h, *_: (jnp.minimum((i + 1) * ratio, n_halo_blocks - 1), c(i, h)))

        in_specs += [cur(0), prev(1), cur(1), nxt(1), prev(2), cur(2), nxt(2)]
        args += [qkv] * 7

    grid_spec = pltpu.PrefetchScalarGridSpec(
        num_scalar_prefetch=1,
        grid=(t // tq, ATTN_WIDTH // wcols),
        in_specs=in_specs,
        out_specs=pl.BlockSpec((tq, wcols), lambda i, h, *_: (i, h)),
        scratch_shapes=[
            pltpu.VMEM((tq, HEAD_DIM), F32),
            pltpu.VMEM((tq + 2 * max_halo, HEAD_DIM), F32),
            pltpu.VMEM((tq + 2 * max_halo, HEAD_DIM), F32),
            pltpu.VMEM((N_GROUPS, tq, HEAD_DIM), F32),
            pltpu.VMEM((N_GROUPS, tq, HEAD_DIM), F32),
        ],
    )
    return pl.pallas_call(
        functools.partial(_attn_kernel, tq=tq, hg=hg, seq_tokens=seq_tokens),
        grid_spec=grid_spec,
        out_shape=jax.ShapeDtypeStruct((t, ATTN_WIDTH), BF16),
        compiler_params=pltpu.CompilerParams(
            dimension_semantics=("parallel", "arbitrary"),
            vmem_limit_bytes=VMEM_LIMIT_BYTES),
        name="dilated_attention",
    )(slopes, *args)


TWIDDLE_RADIX = 16
PITCH_PAD = 8


def _split_hi_lo_np(m):
    hi = m.astype(ml_dtypes.bfloat16)
    lo = (m - hi.astype(np.float64)).astype(ml_dtypes.bfloat16)
    return np.stack([hi, lo])


@functools.lru_cache(maxsize=None)
def _dft_constants(r):
    h = r // 2
    idx = np.arange(r, dtype=np.float64)
    ang = 2.0 * np.pi * np.outer(idx, idx) / r
    c, s = np.cos(ang), np.sin(ang)
    f1 = np.block([[c[:, :h], s[:, :h]], [-s[:, :h], c[:, :h]]])
    f1_real = np.concatenate([c, -s], axis=0)
    f2 = np.block([[c, s], [-s, c]])
    f2_inv = np.block([[c, -s], [s, c]])
    f1_inv = np.block([[c[:h, :], -s[:h, :]], [s[:h, :], c[:h, :]]])
    n = r * r
    coarse = np.arange(r // TWIDDLE_RADIX, dtype=np.float64)[:, None] * TWIDDLE_RADIX
    fine = np.arange(TWIDDLE_RADIX, dtype=np.float64)[:, None]

    def table(mult):
        a = 2.0 * np.pi * mult * idx[None, :] / n
        t = np.stack([np.cos(a), np.sin(a)], axis=1)
        return np.ascontiguousarray(np.broadcast_to(t[..., None], t.shape + (LANES,))).astype(np.float32)

    return dict(f1=_split_hi_lo_np(f1), f1_real=_split_hi_lo_np(f1_real), f2=_split_hi_lo_np(f2),
                f2_inv=_split_hi_lo_np(f2_inv), f1_inv=_split_hi_lo_np(f1_inv),
                t1=table(coarse), t2=table(fine))


def _split_hi_lo(data):
    d_hi = data.astype(BF16)
    return d_hi, (data - d_hi.astype(F32)).astype(BF16)


def _dot3(m_ref, d_hi, d_lo):
    m_hi, m_lo = m_ref[0], m_ref[1]
    return (jnp.dot(m_hi, d_hi, preferred_element_type=F32)
            + jnp.dot(m_hi, d_lo, preferred_element_type=F32)
            + jnp.dot(m_lo, d_hi, preferred_element_type=F32))


def _staged_loop(n_items, stages):
    def body(it, carry):
        state = [it * FFT_BLOCKS + j for j in range(FFT_BLOCKS)]
        for stage in stages:
            state = [stage(s) for s in state]
        return carry

    lax.fori_loop(0, n_items // FFT_BLOCKS, body, 0)


def _twiddle(t1_ref, t2_ref, idx):
    a = idx // TWIDDLE_RADIX
    b = idx % TWIDDLE_RADIX
    c1, s1 = t1_ref[a, 0], t1_ref[a, 1]
    c2, s2 = t2_ref[b, 0], t2_ref[b, 1]
    return c1 * c2 - s1 * s2, s1 * c2 + c1 * s2


def _lanes(x, half):
    return x[:, half * LANES:(half + 1) * LANES]


def _slab_rows(r, chunk):
    return r * (chunk + PITCH_PAD)


def _blocks_to_rows(src_ref, dst_ref, r):
    chunk = src_ref.shape[1]
    pitch = chunk + PITCH_PAD

    def body(g, carry):
        dst_ref[pl.ds(pl.multiple_of(g * pitch, 8), chunk), :] = src_ref[g]
        return carry

    lax.fori_loop(0, r, body, 0, unroll=COPY_UNROLL)


def _rows_to_blocks(src_ref, dst_ref, r):
    chunk = dst_ref.shape[1]
    pitch = chunk + PITCH_PAD

    def body(g, carry):
        dst_ref[g] = src_ref[pl.ds(pl.multiple_of(g * pitch, 8), chunk), :]
        return carry

    lax.fori_loop(0, r, body, 0, unroll=COPY_UNROLL)


def _forward_n1_store(out, first_n2, n2l, t1_ref, t2_ref, sre_ref, sim_ref, r):
    pitch = r + PITCH_PAD
    for half in range(2):
        re, im = _lanes(out[:r], half), _lanes(out[r:], half)
        n2 = first_n2 + n2l + half
        c, s = _twiddle(t1_ref, t2_ref, n2)
        rows = pl.ds(n2, r, stride=pitch)
        sre_ref[rows, :] = re * c + im * s
        sim_ref[rows, :] = im * c - re * s


def _stage_a(x_ref, x2d, chunk, first_n2, f1_ref, t1_ref, t2_ref, sre_ref, sim_ref, r):
    _blocks_to_rows(x_ref, x2d, r)

    def gather(jj):
        n2l = 2 * jj
        rhs = jnp.concatenate([x2d[pl.ds(n2l, r, stride=chunk + PITCH_PAD), :],
                               x2d[pl.ds(n2l + 1, r, stride=chunk + PITCH_PAD), :]], axis=1)
        return n2l, _split_hi_lo(rhs)

    def transform(state):
        n2l, (d_hi, d_lo) = state
        return n2l, _dot3(f1_ref, d_hi, d_lo)

    def twiddle_store(state):
        n2l, out = state
        _forward_n1_store(out, first_n2, n2l, t1_ref, t2_ref, sre_ref, sim_ref, r)

    _staged_loop(chunk // 2, [gather, transform, twiddle_store])


def _load_k1_pair(sre_ref, sim_ref, k1, r):
    pitch = r + PITCH_PAD
    r0 = pl.multiple_of(k1 * pitch, 8)
    r1 = pl.multiple_of((k1 + 1) * pitch, 8)
    rows = (pl.ds(r0, r), pl.ds(r1, r))
    rhs = jnp.concatenate([
        jnp.concatenate([sre_ref[rows[0], :], sre_ref[rows[1], :]], axis=1),
        jnp.concatenate([sim_ref[rows[0], :], sim_ref[rows[1], :]], axis=1)], axis=0)
    return rhs, rows


def _load_n2_pair(sre_ref, sim_ref, n2, r):
    pitch = r + PITCH_PAD
    rows0 = pl.ds(n2, r, stride=pitch)
    rows1 = pl.ds(n2 + 1, r, stride=pitch)
    return jnp.concatenate([
        jnp.concatenate([sre_ref[rows0, :], sre_ref[rows1, :]], axis=1),
        jnp.concatenate([sim_ref[rows0, :], sim_ref[rows1, :]], axis=1)], axis=0)


def _stage_bc(first_k1, chunk, cs_ref, f2_ref, f2i_ref, t1_ref, t2_ref, sre_ref, sim_ref, r):
    def load(jj):
        k1l = 2 * jj
        rhs, rows = _load_k1_pair(sre_ref, sim_ref, first_k1 + k1l, r)
        return k1l, rows, _split_hi_lo(rhs)

    def forward(state):
        k1l, rows, (d_hi, d_lo) = state
        return k1l, rows, _dot3(f2_ref, d_hi, d_lo)

    def filter_multiply(state):
        k1l, rows, x = state
        p_re, p_im = [], []
        for half in range(2):
            xr, xi = _lanes(x[:r], half), _lanes(x[r:], half)
            crow = pl.ds(pl.multiple_of((k1l + half) * r, 8), r)
            cr, ci = cs_ref[0, crow, :], cs_ref[1, crow, :]
            p_re.append(xr * cr - xi * ci)
            p_im.append(xr * ci + xi * cr)
        prod = jnp.concatenate([jnp.concatenate(p_re, axis=1), jnp.concatenate(p_im, axis=1)], axis=0)
        return k1l, rows, _split_hi_lo(prod)

    def inverse(state):
        k1l, rows, (d_hi, d_lo) = state
        return k1l, rows, _dot3(f2i_ref, d_hi, d_lo)

    def twiddle_store(state):
        k1l, rows, y = state
        for half in range(2):
            re, im = _lanes(y[:r], half), _lanes(y[r:], half)
            c, s = _twiddle(t1_ref, t2_ref, first_k1 + k1l + half)
            sre_ref[rows[half], :] = re * c - im * s
            sim_ref[rows[half], :] = im * c + re * s

    _staged_loop(chunk // 2, [load, forward, filter_multiply, inverse, twiddle_store])


def _hyena_conv_kernel(x_ref, gate_ref, cs_ref, f1_ref, f1i_ref, f2_ref, f2i_ref, t1_ref, t2_ref,
                       o_ref, sre_ref, sim_ref, x2d, g2d, *, r, ca, cb):
    n_a, n_bc = r // ca, r // cb
    p_bc1, p_da, p_bc2, p_d2 = n_a, n_a + n_bc, 2 * n_a + n_bc, 2 * n_a + 2 * n_bc
    step = pl.program_id(2)
    slab_stride = ca + PITCH_PAD

    def inverse_n1_stages(first_n2):
        def gather(jj):
            n2l = 2 * jj
            return n2l, _split_hi_lo(_load_n2_pair(sre_ref, sim_ref, first_n2 + n2l, r))

        def inverse_n1(state):
            n2l, (d_hi, d_lo) = state
            return n2l, _dot3(f1i_ref, d_hi, d_lo)

        return [gather, inverse_n1]

    @pl.when(step < p_bc1)
    def _():
        _stage_a(x_ref, x2d, ca, step * ca, f1_ref, t1_ref, t2_ref, sre_ref, sim_ref, r)

    @pl.when((step >= p_bc1) & (step < p_da))
    def _():
        _stage_bc((step - p_bc1) * cb, cb, cs_ref, f2_ref, f2i_ref, t1_ref, t2_ref, sre_ref, sim_ref, r)

    @pl.when((step >= p_da) & (step < p_bc2))
    def _():
        first_n2 = (step - p_da) * ca
        _blocks_to_rows(gate_ref, g2d, r)

        def gate(state):
            n2l, y = state
            z = jnp.concatenate([g2d[pl.ds(n2l + half, r, stride=slab_stride), :] * _lanes(y, half)
                                 for half in range(2)], axis=1)
            return n2l, _split_hi_lo(z)

        def forward_n1(state):
            n2l, (d_hi, d_lo) = state
            return n2l, _dot3(f1_ref, d_hi, d_lo)

        def twiddle_store(state):
            n2l, out = state
            _forward_n1_store(out, first_n2, n2l, t1_ref, t2_ref, sre_ref, sim_ref, r)

        _staged_loop(ca // 2, inverse_n1_stages(first_n2) + [gate, forward_n1, twiddle_store])

    @pl.when((step >= p_bc2) & (step < p_d2))
    def _():
        _stage_bc((step - p_bc2) * cb, cb, cs_ref, f2_ref, f2i_ref, t1_ref, t2_ref, sre_ref, sim_ref, r)

    @pl.when(step >= p_d2)
    def _():
        first_n2 = (step - p_d2) * ca
        _blocks_to_rows(gate_ref, g2d, r)

        def gate_store(state):
            n2l, y = state
            for half in range(2):
                tok = pl.ds(n2l + half, r, stride=slab_stride)
                x2d[tok, :] = g2d[tok, :] * _lanes(y, half)

        _staged_loop(ca // 2, inverse_n1_stages(first_n2) + [gate_store])
        _rows_to_blocks(x2d, o_ref, r)


def _hyena_conv_kernel_aliased(*refs, **kw):
    _hyena_conv_kernel(*refs[1:], **kw)


def _hyena_long_conv(u, row0, spectrum, r, n_pairs, out_buf=None, *, ca, cb=16):
    n_a, n_bc = r // ca, r // cb
    p_bc1, p_da, p_bc2, p_d2 = n_a, n_a + n_bc, 2 * n_a + n_bc, 2 * n_a + 2 * n_bc
    n_slabs = u.shape[0] // 3
    consts = _dft_constants(r)
    t = u.shape[1]
    uv = u.reshape(u.shape[0], t // r, r, LANES)
    blk0 = row0 // (r * r)

    def gate_index(s, p, st):
        second = st >= p_bc2
        chunk = jnp.where(second, st - p_d2, st - p_da)
        return (jnp.where(second, 2 * n_slabs, n_slabs) + s, blk0 + p, jnp.clip(chunk, 0, n_a - 1), 0)

    def spectrum_index(s, p, st):
        second = st >= p_bc2
        chunk = jnp.where(second, st - p_bc2, st - p_bc1)
        return (jnp.where(second, n_slabs, 0) + s, 0, jnp.clip(chunk, 0, n_bc - 1), 0)

    def full(a):
        return pl.BlockSpec(a.shape, lambda s, p, st, nd=a.ndim: (0,) * nd)

    const_args = [consts["f1"], consts["f1_inv"], consts["f2"], consts["f2_inv"], consts["t1"], consts["t2"]]
    in_specs = [
        pl.BlockSpec((None, r, ca, LANES), lambda s, p, st: (s, blk0 + p, jnp.minimum(st, n_a - 1), 0)),
        pl.BlockSpec((None, r, ca, LANES), gate_index),
        pl.BlockSpec((None, 2, cb * r, LANES), spectrum_index),
    ] + [full(a) for a in const_args]
    args = [uv, uv, spectrum, *const_args]
    body = _hyena_conv_kernel
    aliases = {}
    if out_buf is not None:
        in_specs = [pl.BlockSpec(memory_space=pl.ANY)] + in_specs
        args = [out_buf.reshape(n_slabs, t // r, r, LANES)] + args
        body = _hyena_conv_kernel_aliased
        aliases = {0: 0}
    out = pl.pallas_call(
        functools.partial(body, r=r, ca=ca, cb=cb),
        grid=(n_slabs, n_pairs, 3 * n_a + 2 * n_bc),
        in_specs=in_specs,
        out_specs=pl.BlockSpec((None, r, ca, LANES),
                               lambda s, p, st: (s, blk0 + p, jnp.clip(st - p_d2, 0, n_a - 1), 0)),
        out_shape=jax.ShapeDtypeStruct((n_slabs, t // r, r, LANES), F32),
        scratch_shapes=[pltpu.VMEM((r * (r + PITCH_PAD), LANES), F32),
                        pltpu.VMEM((r * (r + PITCH_PAD), LANES), F32),
                        pltpu.VMEM((_slab_rows(r, ca), LANES), F32),
                        pltpu.VMEM((_slab_rows(r, ca), LANES), F32)],
        input_output_aliases=aliases,
        compiler_params=pltpu.CompilerParams(
            dimension_semantics=("parallel", "arbitrary", "arbitrary"),
            vmem_limit_bytes=VMEM_LIMIT_BYTES),
        name="hyena_long_conv",
    )(*args)
    return out.reshape(n_slabs, t, LANES)


def _fft_spectrum_kernel(c_ref, scale_ref, shift_ref, f1_ref, f2_ref, t1_ref, t2_ref, o_ref, sre_ref, sim_ref, c2d,
                         *, r, ca, cb):
    n_a = r // ca
    step = pl.program_id(2)

    @pl.when(step < n_a)
    def _():
        _stage_a(c_ref, c2d, ca, step * ca, f1_ref, t1_ref, t2_ref, sre_ref, sim_ref, r)

    @pl.when(step >= n_a)
    def _():
        first_k1 = (step - n_a) * cb

        def load(jj):
            k1l = 2 * jj
            rhs, _ = _load_k1_pair(sre_ref, sim_ref, first_k1 + k1l, r)
            return k1l, _split_hi_lo(rhs)

        def forward(state):
            k1l, (d_hi, d_lo) = state
            return k1l, _dot3(f2_ref, d_hi, d_lo)

        def store(state):
            k1l, x = state
            for half in range(2):
                orow = pl.ds(pl.multiple_of((k1l + half) * r, 8), r)
                o_ref[0, orow, :] = _lanes(x[:r], half) * scale_ref[...] + shift_ref[...]
                o_ref[1, orow, :] = _lanes(x[r:], half) * scale_ref[...]

        _staged_loop(cb // 2, [load, forward, store])


def _fft_spectrum(c, scale, shift, r, *, ca=32, cb=16):
    n_slabs, n, _ = c.shape
    width = n_slabs * LANES
    n_a, n_bc = r // ca, r // cb
    consts = _dft_constants(r)
    cv = c.reshape(n_slabs, r, r, LANES)

    def full(a):
        return pl.BlockSpec(a.shape, lambda s, o, st, nd=a.ndim: (0,) * nd)

    const_args = [consts["f1_real"], consts["f2"], consts["t1"], consts["t2"]]
    return pl.pallas_call(
        functools.partial(_fft_spectrum_kernel, r=r, ca=ca, cb=cb),
        grid=(n_slabs, 1, n_a + n_bc),
        in_specs=[pl.BlockSpec((None, r, ca, LANES), lambda s, o, st: (s, 0, jnp.minimum(st, n_a - 1), 0)),
                  pl.BlockSpec((1, LANES), lambda s, o, st: (0, s)),
                  pl.BlockSpec((1, LANES), lambda s, o, st: (0, s))]
        + [full(a) for a in const_args],
        out_specs=pl.BlockSpec((None, 2, cb * r, LANES), lambda s, o, st: (s, 0, jnp.maximum(st - n_a, 0), 0)),
        out_shape=jax.ShapeDtypeStruct((n_slabs, 2, n, LANES), F32),
        scratch_shapes=[pltpu.VMEM((r * (r + PITCH_PAD), LANES), F32),
                        pltpu.VMEM((r * (r + PITCH_PAD), LANES), F32),
                        pltpu.VMEM((_slab_rows(r, ca), LANES), F32)],
        compiler_params=pltpu.CompilerParams(
            dimension_semantics=("parallel", "arbitrary", "arbitrary"),
            vmem_limit_bytes=VMEM_LIMIT_BYTES),
        name="fft_spectrum",
    )(cv, scale.reshape(1, width), shift.reshape(1, width), *const_args)


FILTER_EMB = 2 * FILTER_BANDS + 1


def _dot_f32(a, b):
    a_hi, a_lo = _split_hi_lo(a)
    b_hi, b_lo = _split_hi_lo(b)
    return (jnp.dot(a_hi, b_hi, preferred_element_type=F32)
            + jnp.dot(a_hi, b_lo, preferred_element_type=F32)
            + jnp.dot(a_lo, b_hi, preferred_element_type=F32))


def _filter_kernel(bands_ref, w1_ref, b1_ref, w2_ref, b2_ref, freq_ref, w3f_ref, w3b_ref, delta_ref,
                   c_ref, norm_ref, hid_ref, *, length, tr):
    j = pl.program_id(0)
    i = pl.program_id(1)
    row = i * tr + lax.broadcasted_iota(jnp.int32, (tr, 1), 0)
    pos_idx = jnp.where(row < length, row, 2 * length - row)
    mf = pos_idx.astype(F32)
    t = mf / (length - 1)
    rows = pl.ds(pl.multiple_of(i * tr, 8), tr)

    @pl.when(j == 0)
    def _():
        w = 2.0 * math.pi * mf / length
        arg = w * bands_ref[...]
        lane = lax.broadcasted_iota(jnp.int32, (1, LANES), 1)
        pos = jnp.where(lane == 0, t,
                        jnp.where(lane <= FILTER_BANDS, jnp.cos(arg),
                                  jnp.where(lane < FILTER_EMB, -jnp.sin(arg), 0.0)))
        h = jnp.sin(freq_ref[...] * (_dot_f32(pos, w1_ref[...]) + b1_ref[...]))
        h = jnp.sin(freq_ref[...] * (_dot_f32(h, w2_ref[...]) + b2_ref[...]))
        hid_ref[rows, :] = h

    h = hid_ref[rows, :]
    decay = jnp.exp(-t * delta_ref[...])
    fwd = _dot_f32(h, w3f_ref[...]) * decay
    bwd = _dot_f32(h, w3b_ref[...]) * decay
    main = jnp.where(row < length, fwd, jnp.where(row == length, 0.0, bwd))
    extra = jnp.where(row == 0, bwd, 0.0)
    taps = main + extra
    for s in range(c_ref.shape[0]):
        c_ref[s] = taps[:, s * LANES:(s + 1) * LANES]
    contrib = jnp.abs(main) + jnp.abs(extra)
    part = contrib[0:8]
    for k in range(1, tr // 8):
        part = part + contrib[8 * k:8 * (k + 1)]

    @pl.when(i == 0)
    def _():
        norm_ref[...] = part

    @pl.when(i > 0)
    def _():
        norm_ref[...] += part


def _hyena_filter_kernel(length, f_w1, f_b1, f_w2, f_b2, f_freq, f_w3, *, tr=512, tc=1024):
    hidden = f_w1.shape[1]
    dm = f_w3.shape[1] // (2 * HYENA_ORDER)
    width = HYENA_ORDER * dm
    tr, tc = min(tr, 2 * length), min(tc, width)
    bands = jnp.linspace(1e-4, FILTER_BANDS - 1, FILTER_BANDS, dtype=F32)
    bands_row = jnp.zeros((1, LANES), F32).at[0, 1:FILTER_BANDS + 1].set(bands)
    bands_row = bands_row.at[0, FILTER_BANDS + 1:FILTER_EMB].set(bands)

    def pad_to(a, rows, cols):
        return jnp.zeros((rows, cols), F32).at[:a.shape[0], :a.shape[1]].set(a)

    w3 = f_w3.reshape(hidden, HYENA_ORDER, 2, dm)
    w3f = pad_to(w3[:, :, 0].reshape(hidden, width), LANES, width)
    w3b = pad_to(w3[:, :, 1].reshape(hidden, width), LANES, width)
    deltas = jnp.abs(jnp.linspace(MIN_DECAY, MAX_DECAY, dm, dtype=F32))
    delta_row = jnp.tile(deltas, HYENA_ORDER).reshape(1, width)
    small = [bands_row, pad_to(f_w1, LANES, LANES), pad_to(f_b1[None], 1, LANES), pad_to(f_w2, LANES, LANES),
             pad_to(f_b2[None], 1, LANES), pad_to(f_freq[None], 1, LANES)]
    c, norm = pl.pallas_call(
        functools.partial(_filter_kernel, length=length, tr=tr),
        grid=(width // tc, 2 * length // tr),
        in_specs=[pl.BlockSpec(a.shape, lambda j, i: (0, 0)) for a in small] + [
            pl.BlockSpec((LANES, tc), lambda j, i: (0, j)),
            pl.BlockSpec((LANES, tc), lambda j, i: (0, j)),
            pl.BlockSpec((1, tc), lambda j, i: (0, j)),
        ],
        out_specs=[pl.BlockSpec((tc // LANES, tr, LANES), lambda j, i: (j, i, 0)),
                   pl.BlockSpec((8, tc), lambda j, i: (0, j))],
        out_shape=[jax.ShapeDtypeStruct((width // LANES, 2 * length, LANES), F32),
                   jax.ShapeDtypeStruct((8, width), F32)],
        scratch_shapes=[pltpu.VMEM((2 * length, LANES), F32)],
        compiler_params=pltpu.CompilerParams(
            dimension_semantics=("arbitrary", "arbitrary"),
            vmem_limit_bytes=VMEM_LIMIT_BYTES),
        name="hyena_filter",
    )(*small, w3f, w3b, delta_row)
    return c, jnp.sum(norm, axis=0)


def _filter_spectra(length, r, filt_params, skip):
    c, norm = _hyena_filter_kernel(length, *filt_params)
    n = 2 * length
    return _fft_spectrum(c, 1.0 / (n * norm), skip.reshape(-1) / n, r)


def _hyena_mixer(u, filt_params, skip, seq_tokens, batches):
    n_prompt, s_prompt, s_sample = seq_tokens
    slabs = u.shape[0] // 3
    dm = slabs * LANES
    z = None
    for seq_len, n_batch, row0 in ((s_prompt, batches[0], 0), (s_sample, batches[1], n_prompt)):
        r = math.isqrt(2 * seq_len)
        assert r * r == 2 * seq_len and n_batch % 2 == 0 and row0 % (r * r) == 0
        spec = _filter_spectra(seq_len, r, filt_params, skip)
        chunk = 16 if r > 64 else 32
        z = _hyena_long_conv(u, row0, spec, r, n_batch // 2, z, ca=chunk)
    return z


def kernel(x_prompt, x_sample, ln_g, ln_b, ffn_w_in, ffn_w_out, attn_w_qkv, attn_w_o, hy_w_in, hy_b_in,
           hy_conv_w, hy_conv_b, hy_f_w1, hy_f_b1, hy_f_w2, hy_f_b2, hy_f_freq, hy_f_w3, hy_skip,
           hy_w_out, hy_b_out):
    bp, sp, dm = x_prompt.shape
    bs, ss, _ = x_sample.shape
    n_prompt = bp * sp
    n_sample = bs * ss
    seq_tokens = (n_prompt, sp, ss)
    x = jnp.concatenate([x_prompt.reshape(n_prompt, dm), x_sample.reshape(n_sample, dm)], axis=0)
    slopes = jnp.exp2(-8.0 * jnp.arange(1, N_SLOTS + 1, dtype=F32) / N_SLOTS)
    zero_bias_qkv = jnp.zeros((QKV_WIDTH,), F32)
    zero_bias_dm = jnp.zeros((dm,), F32)

    for i in range(DEPTH):
        x = _ffn_layer(x, ffn_w_in[i, 0].astype(BF16), ffn_w_out[i, 0].astype(BF16), ln_g[i, 0], ln_b[i, 0])
        j = i // 2
        if i % 2 == 0:
            qkv = _projection(x, attn_w_qkv[j].astype(BF16), zero_bias_qkv, BF16)
            att = _dilated_attention(qkv, slopes, seq_tokens)
            x = _projection_ln(x, att, attn_w_o[j].astype(BF16), zero_bias_dm, ln_g[i, 1], ln_b[i, 1])
        else:
            u = _projection_short_conv(x, hy_w_in[j].astype(BF16), hy_b_in[j], hy_conv_w[j], hy_conv_b[j],
                                       seq_tokens)
            filt = (hy_f_w1[j], hy_f_b1[j], hy_f_w2[j], hy_f_b2[j], hy_f_freq[j], hy_f_w3[j])
            z = _hyena_mixer(u, filt, hy_skip[j], seq_tokens, (bp, bs))
            x = _projection_ln(x, z, hy_w_out[j].astype(BF16), hy_b_out[j], ln_g[i, 1], ln_b[i, 1])
        x = _ffn_layer(x, ffn_w_in[i, 1].astype(BF16), ffn_w_out[i, 1].astype(BF16), ln_g[i, 2], ln_b[i, 2])

    return (x[:n_prompt].reshape(bp, sp, dm), x[n_prompt:].reshape(bs, ss, dm))
```

```python
import functools
import math

import jax
import jax.numpy as jnp
import ml_dtypes
import numpy as np
from jax import lax
from jax.experimental import pallas as pl
from jax.experimental.pallas import tpu as pltpu

D_MODEL = 2048
DEPTH = 4
HEAD_DIM = 128
N_SLOTS = D_MODEL // HEAD_DIM
DILATED_GROUPS = ((128, 1), (512, 4), (2048, 16))
N_GROUPS = len(DILATED_GROUPS)
ATTN_WIDTH = N_SLOTS * HEAD_DIM
QKV_WIDTH = 3 * N_GROUPS * ATTN_WIDTH
HALF_WINDOW = 64
NEG_INF = -1e30
D_FF = 5632
HYENA_ORDER = 2
SHORT_CONV = 3
FILTER_BANDS = 16
DECAY_TARGET = 1e-2
DECAY_FAST = 0.3
DECAY_SLOW = 1.5
MAX_DECAY = math.log(DECAY_TARGET) / DECAY_FAST
MIN_DECAY = math.log(DECAY_TARGET) / DECAY_SLOW
DN_ALPHA = (2.0 * DEPTH) ** 0.25
LN_EPS = 1e-5

LANES = 128
FFN_TOKEN_TILE = 512
FFN_HIDDEN_TILE = 512
PROJ_TOKEN_TILE = 1024
PROJ_COLUMN_TILE = 1024
OUT_PROJ_TOKEN_TILE = 512
ATTN_QUERY_TILE = 1024
ATTN_HEADS_PER_STEP = 4
ATTN_BLOCKS = 4
FFT_BLOCKS = 4
COPY_UNROLL = 8
VMEM_LIMIT_BYTES = 48 * 1024 * 1024
BF16 = jnp.bfloat16
F32 = jnp.float32


def _layer_norm_rows(y, g, b):
    mu = jnp.mean(y, axis=-1, keepdims=True)
    c = y - mu
    var = jnp.mean(c * c, axis=-1, keepdims=True)
    return c * lax.rsqrt(var + LN_EPS) * g + b


def _seq_position(t0, seq_tokens):
    n_prompt, s_prompt, s_sample = seq_tokens
    seq_len = jnp.where(t0 < n_prompt, s_prompt, s_sample)
    return lax.rem(t0, seq_len), seq_len


def _ffn_kernel(x_ref, wg_ref, wu_ref, wo_ref, g_ref, b_ref, o_ref, xb_ref, *, nk):
    k = pl.program_id(1)

    @pl.when(k == 0)
    def _():
        xb_ref[...] = x_ref[...].astype(BF16)
        o_ref[...] = jnp.zeros_like(o_ref)

    xb = xb_ref[...]
    gate = jnp.dot(xb, wg_ref[...], preferred_element_type=F32)
    up = jnp.dot(xb, wu_ref[...], preferred_element_type=F32)
    act = (gate * jax.nn.sigmoid(gate) * up).astype(BF16)
    o_ref[...] += jnp.dot(act, wo_ref[...], preferred_element_type=F32)

    @pl.when(k == nk - 1)
    def _():
        y = DN_ALPHA * x_ref[...] + 0.5 * o_ref[...]
        o_ref[...] = _layer_norm_rows(y, g_ref[...], b_ref[...])


def _ffn_kernel_aliased(buf_ref, *refs, nk):
    _ffn_kernel(*refs, nk=nk)


def _ffn_layer(x, w_in, w_out, g, b, *, x_row0=0, n_rows=None, out_rows=None, out_row0=0, out_buf=None,
               tm=FFN_TOKEN_TILE, tf=FFN_HIDDEN_TILE):
    dm = x.shape[1]
    n_rows = x.shape[0] if n_rows is None else n_rows
    out_rows = n_rows if out_rows is None else out_rows
    nk = w_out.shape[0] // tf
    x_blk0, o_blk0 = x_row0 // tm, out_row0 // tm
    in_specs = [
        pl.BlockSpec((tm, dm), lambda i, k: (x_blk0 + i, 0)),
        pl.BlockSpec((dm, tf), lambda i, k: (0, k)),
        pl.BlockSpec((dm, tf), lambda i, k: (0, k + nk)),
        pl.BlockSpec((tf, dm), lambda i, k: (k, 0)),
        pl.BlockSpec((1, dm), lambda i, k: (0, 0)),
        pl.BlockSpec((1, dm), lambda i, k: (0, 0)),
    ]
    args = [x, w_in, w_in, w_out, g.reshape(1, dm), b.reshape(1, dm)]
    body, aliases = _ffn_kernel, {}
    if out_buf is not None:
        in_specs = [pl.BlockSpec(memory_space=pl.ANY)] + in_specs
        args = [out_buf] + args
        body, aliases = _ffn_kernel_aliased, {0: 0}
    return pl.pallas_call(
        functools.partial(body, nk=nk),
        grid=(n_rows // tm, nk),
        in_specs=in_specs,
        out_specs=pl.BlockSpec((tm, dm), lambda i, k: (o_blk0 + i, 0)),
        out_shape=jax.ShapeDtypeStruct((out_rows, dm), F32),
        scratch_shapes=[pltpu.VMEM((tm, dm), BF16)],
        input_output_aliases=aliases,
        compiler_params=pltpu.CompilerParams(
            dimension_semantics=("parallel", "arbitrary"),
            vmem_limit_bytes=VMEM_LIMIT_BYTES),
        name="ffn_ln",
    )(*args)


def _proj_kernel(x_ref, w_ref, o_ref, xb_ref):
    @pl.when(pl.program_id(1) == 0)
    def _():
        xb_ref[...] = x_ref[...].astype(BF16)

    o_ref[...] = jnp.dot(xb_ref[...], w_ref[...], preferred_element_type=F32).astype(o_ref.dtype)


def _projection(x, w, out_dtype, *, tm=PROJ_TOKEN_TILE, tn=PROJ_COLUMN_TILE):
    t, dm = x.shape
    n = w.shape[1]
    return pl.pallas_call(
        _proj_kernel,
        grid=(t // tm, n // tn),
        in_specs=[
            pl.BlockSpec((tm, dm), lambda i, j: (i, 0)),
            pl.BlockSpec((dm, tn), lambda i, j: (0, j)),
        ],
        out_specs=pl.BlockSpec((tm, tn), lambda i, j: (i, j)),
        out_shape=jax.ShapeDtypeStruct((t, n), out_dtype),
        scratch_shapes=[pltpu.VMEM((tm, dm), BF16)],
        compiler_params=pltpu.CompilerParams(
            dimension_semantics=("parallel", "arbitrary"),
            vmem_limit_bytes=VMEM_LIMIT_BYTES),
        name="projection",
    )(x, w)


HALO_ROWS = 16


def _proj_conv_kernel(xp_ref, x_ref, xn_ref, w_ref, b_ref, cw_ref, cb_ref, o_ref, xb_ref, *, tm, seq_tokens):
    @pl.when(pl.program_id(1) == 0)
    def _():
        xb_ref[0:HALO_ROWS, :] = xp_ref[...].astype(BF16)
        xb_ref[HALO_ROWS:HALO_ROWS + tm, :] = x_ref[...].astype(BF16)
        xb_ref[HALO_ROWS + tm:, :] = xn_ref[...].astype(BF16)

    rows = tm + 2 * HALO_ROWS
    acc = jnp.dot(xb_ref[...], w_ref[...], preferred_element_type=F32) + b_ref[...]
    pos0, seq_len = _seq_position(pl.program_id(0) * tm, seq_tokens)
    tile_row = lax.broadcasted_iota(jnp.int32, (tm, 1), 0)
    at_seq_start = (tile_row == 0) & (pos0 == 0)
    at_seq_end = (tile_row == tm - 1) & (pos0 + tm == seq_len)
    cur = acc[HALO_ROWS:HALO_ROWS + tm]
    prev = pltpu.roll(acc, 1, axis=0)[HALO_ROWS:HALO_ROWS + tm]
    nxt = pltpu.roll(acc, rows - 1, axis=0)[HALO_ROWS:HALO_ROWS + tm]
    prev = jnp.where(at_seq_start, 0.0, prev)
    nxt = jnp.where(at_seq_end, 0.0, nxt)
    out = prev * cw_ref[0:1, :] + cur * cw_ref[1:2, :] + nxt * cw_ref[2:3, :] + cb_ref[...]
    for s in range(o_ref.shape[0]):
        o_ref[s] = out[:, s * LANES:(s + 1) * LANES]


def _projection_short_conv(x, w, bias, conv_w, conv_b, seq_tokens, *, tm=PROJ_TOKEN_TILE, tn=PROJ_COLUMN_TILE):
    t, dm = x.shape
    n = w.shape[1]
    ratio = tm // HALO_ROWS
    n_halo_blocks = t // HALO_ROWS
    return pl.pallas_call(
        functools.partial(_proj_conv_kernel, tm=tm, seq_tokens=seq_tokens),
        grid=(t // tm, n // tn),
        in_specs=[
            pl.BlockSpec((HALO_ROWS, dm), lambda i, j: (jnp.maximum(i * ratio - 1, 0), 0)),
            pl.BlockSpec((tm, dm), lambda i, j: (i, 0)),
            pl.BlockSpec((HALO_ROWS, dm), lambda i, j: (jnp.minimum((i + 1) * ratio, n_halo_blocks - 1), 0)),
            pl.BlockSpec((dm, tn), lambda i, j: (0, j)),
            pl.BlockSpec((1, tn), lambda i, j: (0, j)),
            pl.BlockSpec((SHORT_CONV, tn), lambda i, j: (0, j)),
            pl.BlockSpec((1, tn), lambda i, j: (0, j)),
        ],
        out_specs=pl.BlockSpec((tn // LANES, tm, LANES), lambda i, j: (j, i, 0)),
        out_shape=jax.ShapeDtypeStruct((n // LANES, t, LANES), F32),
        scratch_shapes=[pltpu.VMEM((tm + 2 * HALO_ROWS, dm), BF16)],
        compiler_params=pltpu.CompilerParams(
            dimension_semantics=("parallel", "arbitrary"),
            vmem_limit_bytes=VMEM_LIMIT_BYTES),
        name="projection_short_conv",
    )(x, x, x, w, bias.reshape(1, n), conv_w, conv_b.reshape(1, n))


def _proj_ln_kernel(x_ref, z_ref, w_ref, bias_ref, g_ref, b_ref, o_ref):
    if len(z_ref.shape) == 3:
        z = jnp.concatenate([z_ref[s].astype(BF16) for s in range(z_ref.shape[0])], axis=1)
    else:
        z = z_ref[...].astype(BF16)
    mix = jnp.dot(z, w_ref[...], preferred_element_type=F32)
    y = DN_ALPHA * x_ref[...] + (mix + bias_ref[...])
    o_ref[...] = _layer_norm_rows(y, g_ref[...], b_ref[...])


def _projection_ln(x, z, w, bias, g, b, *, tm=OUT_PROJ_TOKEN_TILE):
    t, dm = x.shape
    if z.ndim == 3:
        kdim = z.shape[0] * LANES
        z_spec = pl.BlockSpec((z.shape[0], tm, LANES), lambda i: (0, i, 0))
    else:
        kdim = z.shape[1]
        z_spec = pl.BlockSpec((tm, kdim), lambda i: (i, 0))
    return pl.pallas_call(
        _proj_ln_kernel,
        grid=(t // tm,),
        in_specs=[
            pl.BlockSpec((tm, dm), lambda i: (i, 0)),
            z_spec,
            pl.BlockSpec((kdim, dm), lambda i: (0, 0)),
            pl.BlockSpec((1, dm), lambda i: (0, 0)),
            pl.BlockSpec((1, dm), lambda i: (0, 0)),
            pl.BlockSpec((1, dm), lambda i: (0, 0)),
        ],
        out_specs=pl.BlockSpec((tm, dm), lambda i: (i, 0)),
        out_shape=jax.ShapeDtypeStruct((t, dm), F32),
        compiler_params=pltpu.CompilerParams(
            dimension_semantics=("parallel",),
            vmem_limit_bytes=VMEM_LIMIT_BYTES),
        name="projection_ln",
    )(x, z, w, bias.reshape(1, dm), g.reshape(1, dm), b.reshape(1, dm))


def _attn_kernel(slopes_ref, *refs, tq, hg, seq_tokens):
    n_in = 7 * N_GROUPS
    in_refs = refs[:n_in]
    o_ref = refs[n_in]
    qf_ref, kf_ref, vf_ref, og_ref, lg_ref = refs[n_in + 1:]

    pos0, seq_len = _seq_position(pl.program_id(0) * tq, seq_tokens)
    scale = HEAD_DIM ** -0.5

    for hh in range(hg):
        cols = slice(hh * HEAD_DIM, (hh + 1) * HEAD_DIM)
        slope = slopes_ref[pl.program_id(1) * hg + hh]
        for g, (_, d) in enumerate(DILATED_GROUPS):
            q_ref, kp_ref, kc_ref, kn_ref, vp_ref, vc_ref, vn_ref = in_refs[7 * g:7 * g + 7]
            halo = HALF_WINDOW * d
            nq = tq // d
            sb = min(nq, 128)
            nkeys = sb + 2 * HALF_WINDOW
            chain_len = seq_len // d
            cpos0 = pos0 // d

            qf_ref[0:tq, :] = q_ref[:, cols].astype(F32)
            kf_ref[0:halo, :] = kp_ref[:, cols].astype(F32)
            kf_ref[halo:halo + tq, :] = kc_ref[:, cols].astype(F32)
            kf_ref[halo + tq:2 * halo + tq, :] = kn_ref[:, cols].astype(F32)
            vf_ref[0:halo, :] = vp_ref[:, cols].astype(F32)
            vf_ref[halo:halo + tq, :] = vc_ref[:, cols].astype(F32)
            vf_ref[halo + tq:2 * halo + tq, :] = vn_ref[:, cols].astype(F32)

            qi = lax.broadcasted_iota(jnp.int32, (sb, nkeys), 0)
            ki = lax.broadcasted_iota(jnp.int32, (sb, nkeys), 1)
            rel = ki - HALF_WINDOW - qi
            band_bias = jnp.where(jnp.abs(rel) <= HALF_WINDOW,
                                  (-slope) * (jnp.abs(rel) * d).astype(F32), NEG_INF)
            key_col = lax.broadcasted_iota(jnp.int32, (1, nkeys), 1)

            def chain_blocks(it, carry, d=d, sb=sb, nkeys=nkeys, nq=nq, g=g,
                             band_bias=band_bias, key_col=key_col,
                             chain_len=chain_len, cpos0=cpos0):
                rows, scores = [], []
                for j in range(ATTN_BLOCKS):
                    idx = it * ATTN_BLOCKS + j
                    r = idx // (nq // sb)
                    s = idx % (nq // sb)
                    start = r + s * (sb * d)
                    if d == 1:
                        q_rows, k_rows = pl.ds(start, sb), pl.ds(start, nkeys)
                    else:
                        q_rows, k_rows = pl.ds(start, sb, stride=d), pl.ds(start, nkeys, stride=d)
                    qc = qf_ref[q_rows, :].astype(BF16)
                    kc = kf_ref[k_rows, :].astype(BF16)
                    sc = lax.dot_general(qc, kc, (((1,), (1,)), ((), ())), preferred_element_type=F32)
                    rows.append((q_rows, k_rows, s))
                    scores.append(sc)
                probs = []
                for (q_rows, k_rows, s), sc in zip(rows, scores):
                    kpos0 = cpos0 + s * sb - HALF_WINDOW
                    in_seq = (key_col >= -kpos0) & (key_col < chain_len - kpos0)
                    sc = sc * scale + band_bias + jnp.where(in_seq, 0.0, NEG_INF)
                    mx = jnp.max(sc, axis=-1, keepdims=True)
                    p = jnp.exp(sc - mx)
                    den = jnp.sum(p, axis=-1, keepdims=True)
                    probs.append((p.astype(BF16), mx, den))
                outs = [jnp.dot(p, vf_ref[k_rows, :].astype(BF16), preferred_element_type=F32)
                        for (p, _, _), (_, k_rows, _) in zip(probs, rows)]
                for (q_rows, _, _), (_, mx, den), o in zip(rows, probs, outs):
                    og_ref[g, q_rows, :] = o / den
                    lg_ref[g, q_rows, :] = jnp.broadcast_to(mx + jnp.log(den), (sb, HEAD_DIM))
                return carry

            lax.fori_loop(0, d * (nq // sb) // ATTN_BLOCKS, chain_blocks, 0)

        l0, l1, l2 = lg_ref[0], lg_ref[1], lg_ref[2]
        lmax = jnp.maximum(jnp.maximum(l0, l1), l2)
        w0, w1, w2 = jnp.exp(l0 - lmax), jnp.exp(l1 - lmax), jnp.exp(l2 - lmax)
        mixed = (w0 * og_ref[0] + w1 * og_ref[1] + w2 * og_ref[2]) / (w0 + w1 + w2)
        o_ref[:, cols] = mixed.astype(o_ref.dtype)


def _dilated_attention(qkv, slopes, seq_tokens, *, tq=ATTN_QUERY_TILE, hg=ATTN_HEADS_PER_STEP):
    t = qkv.shape[0]
    wcols = hg * HEAD_DIM
    blocks_per_part = ATTN_WIDTH // wcols
    in_specs = []
    args = []
    max_halo = HALF_WINDOW * max(d for _, d in DILATED_GROUPS)
    for g, (_, d) in enumerate(DILATED_GROUPS):
        halo = HALF_WINDOW * d
        ratio = tq // halo
        n_halo_blocks = t // halo

        def col(part, g=g):
            return lambda i, h, *_: (part * N_GROUPS + g) * blocks_per_part + h

        def cur(part, g=g):
            c = col(part, g)
            return pl.BlockSpec((tq, wcols), lambda i, h, *_: (i, c(i, h)))

        def prev(part, g=g, ratio=ratio):
            c = col(part, g)
            return pl.BlockSpec((halo, wcols),
                                lambda i, h, *_: (jnp.maximum(i * ratio - 1, 0), c(i, h)))

        def nxt(part, g=g, ratio=ratio, n_halo_blocks=n_halo_blocks):
            c = col(part, g)
            return pl.BlockSpec((halo, wcols),
                                lambda i, h, *_: (jnp.minimum((i + 1) * ratio, n_halo_blocks - 1), c(i, h)))

        in_specs += [cur(0), prev(1), cur(1), nxt(1), prev(2), cur(2), nxt(2)]
        args += [qkv] * 7

    grid_spec = pltpu.PrefetchScalarGridSpec(
        num_scalar_prefetch=1,
        grid=(t // tq, ATTN_WIDTH // wcols),
        in_specs=in_specs,
        out_specs=pl.BlockSpec((tq, wcols), lambda i, h, *_: (i, h)),
        scratch_shapes=[
            pltpu.VMEM((tq, HEAD_DIM), F32),
            pltpu.VMEM((tq + 2 * max_halo, HEAD_DIM), F32),
            pltpu.VMEM((tq + 2 * max_halo, HEAD_DIM), F32),
            pltpu.VMEM((N_GROUPS, tq, HEAD_DIM), F32),
            pltpu.VMEM((N_GROUPS, tq, HEAD_DIM), F32),
        ],
    )
    return pl.pallas_call(
        functools.partial(_attn_kernel, tq=tq, hg=hg, seq_tokens=seq_tokens),
        grid_spec=grid_spec,
        out_shape=jax.ShapeDtypeStruct((t, ATTN_WIDTH), BF16),
        compiler_params=pltpu.CompilerParams(
            dimension_semantics=("parallel", "arbitrary"),
            vmem_limit_bytes=VMEM_LIMIT_BYTES),
        name="dilated_attention",
    )(slopes, *args)


TWIDDLE_RADIX = 16
PITCH_PAD = 8


def _split_hi_lo_np(m):
    hi = m.astype(ml_dtypes.bfloat16)
    lo = (m - hi.astype(np.float64)).astype(ml_dtypes.bfloat16)
    return np.stack([hi, lo])


@functools.lru_cache(maxsize=None)
def _dft_constants(r):
    h = r // 2
    idx = np.arange(r, dtype=np.float64)
    ang = 2.0 * np.pi * np.outer(idx, idx) / r
    c, s = np.cos(ang), np.sin(ang)
    f1 = np.block([[c[:, :h], s[:, :h]], [-s[:, :h], c[:, :h]]])
    f1_real = np.concatenate([c, -s], axis=0)
    f2 = np.block([[c, s], [-s, c]])
    f2_inv = np.block([[c, -s], [s, c]])
    f1_inv = np.block([[c[:h, :], -s[:h, :]], [s[:h, :], c[:h, :]]])
    n = r * r
    coarse = np.arange(r // TWIDDLE_RADIX, dtype=np.float64)[:, None] * TWIDDLE_RADIX
    fine = np.arange(TWIDDLE_RADIX, dtype=np.float64)[:, None]

    def table(mult):
        a = 2.0 * np.pi * mult * idx[None, :] / n
        t = np.stack([np.cos(a), np.sin(a)], axis=1)
        return np.ascontiguousarray(np.broadcast_to(t[..., None], t.shape + (LANES,))).astype(np.float32)

    return dict(f1=_split_hi_lo_np(f1), f1_real=_split_hi_lo_np(f1_real), f2=_split_hi_lo_np(f2),
                f2_inv=_split_hi_lo_np(f2_inv), f1_inv=_split_hi_lo_np(f1_inv),
                t1=table(coarse), t2=table(fine))


def _split_hi_lo(data):
    d_hi = data.astype(BF16)
    return d_hi, (data - d_hi.astype(F32)).astype(BF16)


def _dot3(m_ref, d_hi, d_lo):
    m_hi, m_lo = m_ref[0], m_ref[1]
    return (jnp.dot(m_hi, d_hi, preferred_element_type=F32)
            + jnp.dot(m_hi, d_lo, preferred_element_type=F32)
            + jnp.dot(m_lo, d_hi, preferred_element_type=F32))


def _staged_loop(n_items, stages):
    def body(it, carry):
        state = [it * FFT_BLOCKS + j for j in range(FFT_BLOCKS)]
        for stage in stages:
            state = [stage(s) for s in state]
        return carry

    lax.fori_loop(0, n_items // FFT_BLOCKS, body, 0)


def _twiddle(t1_ref, t2_ref, idx):
    a = idx // TWIDDLE_RADIX
    b = idx % TWIDDLE_RADIX
    c1, s1 = t1_ref[a, 0], t1_ref[a, 1]
    c2, s2 = t2_ref[b, 0], t2_ref[b, 1]
    return c1 * c2 - s1 * s2, s1 * c2 + c1 * s2


def _lanes(x, half):
    return x[:, half * LANES:(half + 1) * LANES]


def _slab_rows(r, chunk):
    return r * (chunk + PITCH_PAD)


def _blocks_to_rows(src_ref, dst_ref, r):
    chunk = src_ref.shape[1]
    pitch = chunk + PITCH_PAD

    def body(g, carry):
        dst_ref[pl.ds(pl.multiple_of(g * pitch, 8), chunk), :] = src_ref[g]
        return carry

    lax.fori_loop(0, r, body, 0, unroll=COPY_UNROLL)


def _rows_to_blocks(src_ref, dst_ref, r):
    chunk = dst_ref.shape[1]
    pitch = chunk + PITCH_PAD

    def body(g, carry):
        dst_ref[g] = src_ref[pl.ds(pl.multiple_of(g * pitch, 8), chunk), :]
        return carry

    lax.fori_loop(0, r, body, 0, unroll=COPY_UNROLL)


def _forward_n1_store(out, first_n2, n2l, t1_ref, t2_ref, sre_ref, sim_ref, r):
    pitch = r + PITCH_PAD
    for half in range(2):
        re, im = _lanes(out[:r], half), _lanes(out[r:], half)
        n2 = first_n2 + n2l + half
        c, s = _twiddle(t1_ref, t2_ref, n2)
        rows = pl.ds(n2, r, stride=pitch)
        sre_ref[rows, :] = re * c + im * s
        sim_ref[rows, :] = im * c - re * s


def _stage_a(x_ref, x2d, chunk, first_n2, f1_ref, t1_ref, t2_ref, sre_ref, sim_ref, r):
    _blocks_to_rows(x_ref, x2d, r)

    def gather(jj):
        n2l = 2 * jj
        rhs = jnp.concatenate([x2d[pl.ds(n2l, r, stride=chunk + PITCH_PAD), :],
                               x2d[pl.ds(n2l + 1, r, stride=chunk + PITCH_PAD), :]], axis=1)
        return n2l, _split_hi_lo(rhs)

    def transform(state):
        n2l, (d_hi, d_lo) = state
        return n2l, _dot3(f1_ref, d_hi, d_lo)

    def twiddle_store(state):
        n2l, out = state
        _forward_n1_store(out, first_n2, n2l, t1_ref, t2_ref, sre_ref, sim_ref, r)

    _staged_loop(chunk // 2, [gather, transform, twiddle_store])


def _load_k1_pair(sre_ref, sim_ref, k1, r):
    pitch = r + PITCH_PAD
    r0 = pl.multiple_of(k1 * pitch, 8)
    r1 = pl.multiple_of((k1 + 1) * pitch, 8)
    rows = (pl.ds(r0, r), pl.ds(r1, r))
    rhs = jnp.concatenate([
        jnp.concatenate([sre_ref[rows[0], :], sre_ref[rows[1], :]], axis=1),
        jnp.concatenate([sim_ref[rows[0], :], sim_ref[rows[1], :]], axis=1)], axis=0)
    return rhs, rows


def _load_n2_pair(sre_ref, sim_ref, n2, r):
    pitch = r + PITCH_PAD
    rows0 = pl.ds(n2, r, stride=pitch)
    rows1 = pl.ds(n2 + 1, r, stride=pitch)
    return jnp.concatenate([
        jnp.concatenate([sre_ref[rows0, :], sre_ref[rows1, :]], axis=1),
        jnp.concatenate([sim_ref[rows0, :], sim_ref[rows1, :]], axis=1)], axis=0)


def _stage_bc(first_k1, chunk, cs_ref, f2_ref, f2i_ref, t1_ref, t2_ref, sre_ref, sim_ref, r):
    def load(jj):
        k1l = 2 * jj
        rhs, rows = _load_k1_pair(sre_ref, sim_ref, first_k1 + k1l, r)
        return k1l, rows, _split_hi_lo(rhs)

    def forward(state):
        k1l, rows, (d_hi, d_lo) = state
        return k1l, rows, _dot3(f2_ref, d_hi, d_lo)

    def filter_multiply(state):
        k1l, rows, x = state
        p_re, p_im = [], []
        for half in range(2):
            xr, xi = _lanes(x[:r], half), _lanes(x[r:], half)
            crow = pl.ds(pl.multiple_of((k1l + half) * r, 8), r)
            cr, ci = cs_ref[0, crow, :], cs_ref[1, crow, :]
            p_re.append(xr * cr - xi * ci)
            p_im.append(xr * ci + xi * cr)
        prod = jnp.concatenate([jnp.concatenate(p_re, axis=1), jnp.concatenate(p_im, axis=1)], axis=0)
        return k1l, rows, _split_hi_lo(prod)

    def inverse(state):
        k1l, rows, (d_hi, d_lo) = state
        return k1l, rows, _dot3(f2i_ref, d_hi, d_lo)

    def twiddle_store(state):
        k1l, rows, y = state
        for half in range(2):
            re, im = _lanes(y[:r], half), _lanes(y[r:], half)
            c, s = _twiddle(t1_ref, t2_ref, first_k1 + k1l + half)
            sre_ref[rows[half], :] = re * c - im * s
            sim_ref[rows[half], :] = im * c + re * s

    _staged_loop(chunk // 2, [load, forward, filter_multiply, inverse, twiddle_store])


def _hyena_conv_kernel(x_ref, gate_ref, cs_ref, f1_ref, f1i_ref, f2_ref, f2i_ref, t1_ref, t2_ref,
                       o_ref, sre_ref, sim_ref, x2d, g2d, *, r, ca, cb):
    n_a, n_bc = r // ca, r // cb
    p_bc1, p_da, p_bc2, p_d2 = n_a, n_a + n_bc, 2 * n_a + n_bc, 2 * n_a + 2 * n_bc
    step = pl.program_id(2)
    slab_stride = ca + PITCH_PAD

    def inverse_n1_stages(first_n2):
        def gather(jj):
            n2l = 2 * jj
            return n2l, _split_hi_lo(_load_n2_pair(sre_ref, sim_ref, first_n2 + n2l, r))

        def inverse_n1(state):
            n2l, (d_hi, d_lo) = state
            return n2l, _dot3(f1i_ref, d_hi, d_lo)

        return [gather, inverse_n1]

    @pl.when(step < p_bc1)
    def _():
        _stage_a(x_ref, x2d, ca, step * ca, f1_ref, t1_ref, t2_ref, sre_ref, sim_ref, r)

    @pl.when((step >= p_bc1) & (step < p_da))
    def _():
        _stage_bc((step - p_bc1) * cb, cb, cs_ref, f2_ref, f2i_ref, t1_ref, t2_ref, sre_ref, sim_ref, r)

    @pl.when((step >= p_da) & (step < p_bc2))
    def _():
        first_n2 = (step - p_da) * ca
        _blocks_to_rows(gate_ref, g2d, r)

        def gate(state):
            n2l, y = state
            z = jnp.concatenate([g2d[pl.ds(n2l + half, r, stride=slab_stride), :] * _lanes(y, half)
                                 for half in range(2)], axis=1)
            return n2l, _split_hi_lo(z)

        def forward_n1(state):
            n2l, (d_hi, d_lo) = state
            return n2l, _dot3(f1_ref, d_hi, d_lo)

        def twiddle_store(state):
            n2l, out = state
            _forward_n1_store(out, first_n2, n2l, t1_ref, t2_ref, sre_ref, sim_ref, r)

        _staged_loop(ca // 2, inverse_n1_stages(first_n2) + [gate, forward_n1, twiddle_store])

    @pl.when((step >= p_bc2) & (step < p_d2))
    def _():
        _stage_bc((step - p_bc2) * cb, cb, cs_ref, f2_ref, f2i_ref, t1_ref, t2_ref, sre_ref, sim_ref, r)

    @pl.when(step >= p_d2)
    def _():
        first_n2 = (step - p_d2) * ca
        _blocks_to_rows(gate_ref, g2d, r)

        def gate_store(state):
            n2l, y = state
            for half in range(2):
                tok = pl.ds(n2l + half, r, stride=slab_stride)
                x2d[tok, :] = g2d[tok, :] * _lanes(y, half)

        _staged_loop(ca // 2, inverse_n1_stages(first_n2) + [gate_store])
        _rows_to_blocks(x2d, o_ref, r)


def _hyena_conv_kernel_aliased(*refs, **kw):
    _hyena_conv_kernel(*refs[1:], **kw)


def _hyena_long_conv(u, row0, spectrum, r, n_pairs, out_buf=None, *, ca, cb=16):
    n_a, n_bc = r // ca, r // cb
    p_bc1, p_da, p_bc2, p_d2 = n_a, n_a + n_bc, 2 * n_a + n_bc, 2 * n_a + 2 * n_bc
    n_slabs = u.shape[0] // 3
    consts = _dft_constants(r)
    t = u.shape[1]
    uv = u.reshape(u.shape[0], t // r, r, LANES)
    blk0 = row0 // (r * r)

    def gate_index(s, p, st):
        second = st >= p_bc2
        chunk = jnp.where(second, st - p_d2, st - p_da)
        return (jnp.where(second, 2 * n_slabs, n_slabs) + s, blk0 + p, jnp.clip(chunk, 0, n_a - 1), 0)

    def spectrum_index(s, p, st):
        second = st >= p_bc2
        chunk = jnp.where(second, st - p_bc2, st - p_bc1)
        return (jnp.where(second, n_slabs, 0) + s, 0, jnp.clip(chunk, 0, n_bc - 1), 0)

    def full(a):
        return pl.BlockSpec(a.shape, lambda s, p, st, nd=a.ndim: (0,) * nd)

    const_args = [consts["f1"], consts["f1_inv"], consts["f2"], consts["f2_inv"], consts["t1"], consts["t2"]]
    in_specs = [
        pl.BlockSpec((None, r, ca, LANES), lambda s, p, st: (s, blk0 + p, jnp.minimum(st, n_a - 1), 0)),
        pl.BlockSpec((None, r, ca, LANES), gate_index),
        pl.BlockSpec((None, 2, cb * r, LANES), spectrum_index),
    ] + [full(a) for a in const_args]
    args = [uv, uv, spectrum, *const_args]
    body = _hyena_conv_kernel
    aliases = {}
    if out_buf is not None:
        in_specs = [pl.BlockSpec(memory_space=pl.ANY)] + in_specs
        args = [out_buf.reshape(n_slabs, t // r, r, LANES)] + args
        body = _hyena_conv_kernel_aliased
        aliases = {0: 0}
    out = pl.pallas_call(
        functools.partial(body, r=r, ca=ca, cb=cb),
        grid=(n_slabs, n_pairs, 3 * n_a + 2 * n_bc),
        in_specs=in_specs,
        out_specs=pl.BlockSpec((None, r, ca, LANES),
                               lambda s, p, st: (s, blk0 + p, jnp.clip(st - p_d2, 0, n_a - 1), 0)),
        out_shape=jax.ShapeDtypeStruct((n_slabs, t // r, r, LANES), F32),
        scratch_shapes=[pltpu.VMEM((r * (r + PITCH_PAD), LANES), F32),
                        pltpu.VMEM((r * (r + PITCH_PAD), LANES), F32),
                        pltpu.VMEM((_slab_rows(r, ca), LANES), F32),
                        pltpu.VMEM((_slab_rows(r, ca), LANES), F32)],
        input_output_aliases=aliases,
        compiler_params=pltpu.CompilerParams(
            dimension_semantics=("parallel", "arbitrary", "arbitrary"),
            vmem_limit_bytes=VMEM_LIMIT_BYTES),
        name="hyena_long_conv",
    )(*args)
    return out.reshape(n_slabs, t, LANES)


def _fft_spectrum_kernel(c_ref, scale_ref, shift_ref, f1_ref, f2_ref, t1_ref, t2_ref, o_ref, sre_ref, sim_ref, c2d,
                         *, r, ca, cb):
    n_a = r // ca
    step = pl.program_id(2)

    @pl.when(step < n_a)
    def _():
        _stage_a(c_ref, c2d, ca, step * ca, f1_ref, t1_ref, t2_ref, sre_ref, sim_ref, r)

    @pl.when(step >= n_a)
    def _():
        first_k1 = (step - n_a) * cb

        def load(jj):
            k1l = 2 * jj
            rhs, _ = _load_k1_pair(sre_ref, sim_ref, first_k1 + k1l, r)
            return k1l, _split_hi_lo(rhs)

        def forward(state):
            k1l, (d_hi, d_lo) = state
            return k1l, _dot3(f2_ref, d_hi, d_lo)

        def store(state):
            k1l, x = state
            for half in range(2):
                orow = pl.ds(pl.multiple_of((k1l + half) * r, 8), r)
                o_ref[0, orow, :] = _lanes(x[:r], half) * scale_ref[...] + shift_ref[...]
                o_ref[1, orow, :] = _lanes(x[r:], half) * scale_ref[...]

        _staged_loop(cb // 2, [load, forward, store])


def _fft_spectrum(c, scale, shift, r, *, ca=32, cb=16):
    n_slabs, n, _ = c.shape
    width = n_slabs * LANES
    n_a, n_bc = r // ca, r // cb
    consts = _dft_constants(r)
    cv = c.reshape(n_slabs, r, r, LANES)

    def full(a):
        return pl.BlockSpec(a.shape, lambda s, o, st, nd=a.ndim: (0,) * nd)

    const_args = [consts["f1_real"], consts["f2"], consts["t1"], consts["t2"]]
    return pl.pallas_call(
        functools.partial(_fft_spectrum_kernel, r=r, ca=ca, cb=cb),
        grid=(n_slabs, 1, n_a + n_bc),
        in_specs=[pl.BlockSpec((None, r, ca, LANES), lambda s, o, st: (s, 0, jnp.minimum(st, n_a - 1), 0)),
                  pl.BlockSpec((1, LANES), lambda s, o, st: (0, s)),
                  pl.BlockSpec((1, LANES), lambda s, o, st: (0, s))]
        + [full(a) for a in const_args],
        out_specs=pl.BlockSpec((None, 2, cb * r, LANES), lambda s, o, st: (s, 0, jnp.maximum(st - n_a, 0), 0)),
        out_shape=jax.ShapeDtypeStruct((n_slabs, 2, n, LANES), F32),
        scratch_shapes=[pltpu.VMEM((r * (r + PITCH_PAD), LANES), F32),
                        pltpu.VMEM((r * (r + PITCH_PAD), LANES), F32),
                        pltpu.VMEM((_slab_rows(r, ca), LANES), F32)],
        compiler_params=pltpu.CompilerParams(
            dimension_semantics=("parallel", "arbitrary", "arbitrary"),
            vmem_limit_bytes=VMEM_LIMIT_BYTES),
        name="fft_spectrum",
    )(cv, scale.reshape(1, width), shift.reshape(1, width), *const_args)


FILTER_EMB = 2 * FILTER_BANDS + 1


def _dot_f32(a, b):
    a_hi, a_lo = _split_hi_lo(a)
    b_hi, b_lo = _split_hi_lo(b)
    return (jnp.dot(a_hi, b_hi, preferred_element_type=F32)
            + jnp.dot(a_hi, b_lo, preferred_element_type=F32)
            + jnp.dot(a_lo, b_hi, preferred_element_type=F32))


def _filter_kernel(bands_ref, w1_ref, b1_ref, w2_ref, b2_ref, freq_ref, w3f_ref, w3b_ref, delta_ref,
                   c_ref, norm_ref, hid_ref, *, length, tr):
    j = pl.program_id(0)
    i = pl.program_id(1)
    row = i * tr + lax.broadcasted_iota(jnp.int32, (tr, 1), 0)
    pos_idx = jnp.where(row < length, row, 2 * length - row)
    mf = pos_idx.astype(F32)
    t = mf / (length - 1)
    rows = pl.ds(pl.multiple_of(i * tr, 8), tr)

    @pl.when(j == 0)
    def _():
        w = 2.0 * math.pi * mf / length
        arg = w * bands_ref[...]
        lane = lax.broadcasted_iota(jnp.int32, (1, LANES), 1)
        pos = jnp.where(lane == 0, t,
                        jnp.where(lane <= FILTER_BANDS, jnp.cos(arg),
                                  jnp.where(lane < FILTER_EMB, -jnp.sin(arg), 0.0)))
        h = jnp.sin(freq_ref[...] * (_dot_f32(pos, w1_ref[...]) + b1_ref[...]))
        h = jnp.sin(freq_ref[...] * (_dot_f32(h, w2_ref[...]) + b2_ref[...]))
        hid_ref[rows, :] = h

    h = hid_ref[rows, :]
    decay = jnp.exp(-t * delta_ref[...])
    fwd = _dot_f32(h, w3f_ref[...]) * decay
    bwd = _dot_f32(h, w3b_ref[...]) * decay
    main = jnp.where(row < length, fwd, jnp.where(row == length, 0.0, bwd))
    extra = jnp.where(row == 0, bwd, 0.0)
    taps = main + extra
    for s in range(c_ref.shape[0]):
        c_ref[s] = taps[:, s * LANES:(s + 1) * LANES]
    contrib = jnp.abs(main) + jnp.abs(extra)
    part = contrib[0:8]
    for k in range(1, tr // 8):
        part = part + contrib[8 * k:8 * (k + 1)]

    @pl.when(i == 0)
    def _():
        norm_ref[...] = part

    @pl.when(i > 0)
    def _():
        norm_ref[...] += part


def _hyena_filter_kernel(length, f_w1, f_b1, f_w2, f_b2, f_freq, f_w3, *, tr=512, tc=1024):
    hidden = f_w1.shape[1]
    dm = f_w3.shape[1] // (2 * HYENA_ORDER)
    width = HYENA_ORDER * dm
    tr, tc = min(tr, 2 * length), min(tc, width)
    bands = jnp.linspace(1e-4, FILTER_BANDS - 1, FILTER_BANDS, dtype=F32)
    bands_row = jnp.zeros((1, LANES), F32).at[0, 1:FILTER_BANDS + 1].set(bands)
    bands_row = bands_row.at[0, FILTER_BANDS + 1:FILTER_EMB].set(bands)

    def pad_to(a, rows, cols):
        return jnp.zeros((rows, cols), F32).at[:a.shape[0], :a.shape[1]].set(a)

    w3 = f_w3.reshape(hidden, HYENA_ORDER, 2, dm)
    w3f = pad_to(w3[:, :, 0].reshape(hidden, width), LANES, width)
    w3b = pad_to(w3[:, :, 1].reshape(hidden, width), LANES, width)
    deltas = jnp.abs(jnp.linspace(MIN_DECAY, MAX_DECAY, dm, dtype=F32))
    delta_row = jnp.tile(deltas, HYENA_ORDER).reshape(1, width)
    small = [bands_row, pad_to(f_w1, LANES, LANES), pad_to(f_b1[None], 1, LANES), pad_to(f_w2, LANES, LANES),
             pad_to(f_b2[None], 1, LANES), pad_to(f_freq[None], 1, LANES)]
    c, norm = pl.pallas_call(
        functools.partial(_filter_kernel, length=length, tr=tr),
        grid=(width // tc, 2 * length // tr),
        in_specs=[pl.BlockSpec(a.shape, lambda j, i: (0, 0)) for a in small] + [
            pl.BlockSpec((LANES, tc), lambda j, i: (0, j)),
            pl.BlockSpec((LANES, tc), lambda j, i: (0, j)),
            pl.BlockSpec((1, tc), lambda j, i: (0, j)),
        ],
        out_specs=[pl.BlockSpec((tc // LANES, tr, LANES), lambda j, i: (j, i, 0)),
                   pl.BlockSpec((8, tc), lambda j, i: (0, j))],
        out_shape=[jax.ShapeDtypeStruct((width // LANES, 2 * length, LANES), F32),
                   jax.ShapeDtypeStruct((8, width), F32)],
        scratch_shapes=[pltpu.VMEM((2 * length, LANES), F32)],
        compiler_params=pltpu.CompilerParams(
            dimension_semantics=("arbitrary", "arbitrary"),
            vmem_limit_bytes=VMEM_LIMIT_BYTES),
        name="hyena_filter",
    )(*small, w3f, w3b, delta_row)
    return c, jnp.sum(norm, axis=0)


def _filter_spectra(length, r, filt_params, skip):
    c, norm = _hyena_filter_kernel(length, *filt_params)
    n = 2 * length
    return _fft_spectrum(c, 1.0 / (n * norm), skip.reshape(-1) / n, r)


def _hyena_mixer(u, filt_params, skip, seq_tokens, batches):
    n_prompt, s_prompt, s_sample = seq_tokens
    z = None
    for seq_len, n_batch, row0 in ((s_prompt, batches[0], 0), (s_sample, batches[1], n_prompt)):
        r = math.isqrt(2 * seq_len)
        assert r * r == 2 * seq_len and n_batch % 2 == 0 and row0 % (r * r) == 0
        spec = _filter_spectra(seq_len, r, filt_params, skip)
        chunk = 16 if r > 64 else 32
        z = _hyena_long_conv(u, row0, spec, r, n_batch // 2, z, ca=chunk)
    return z


def kernel(x_prompt, x_sample, ln_g, ln_b, ffn_w_in, ffn_w_out, attn_w_qkv, attn_w_o, hy_w_in, hy_b_in,
           hy_conv_w, hy_conv_b, hy_f_w1, hy_f_b1, hy_f_w2, hy_f_b2, hy_f_freq, hy_f_w3, hy_skip,
           hy_w_out, hy_b_out):
    bp, sp, dm = x_prompt.shape
    bs, ss, _ = x_sample.shape
    n_prompt = bp * sp
    n_sample = bs * ss
    seq_tokens = (n_prompt, sp, ss)
    n_tokens = n_prompt + n_sample
    slopes = jnp.exp2(-8.0 * jnp.arange(1, N_SLOTS + 1, dtype=F32) / N_SLOTS)
    zero_bias_dm = jnp.zeros((dm,), F32)

    x = None
    for i in range(DEPTH):
        ffn = (ffn_w_in[i, 0].astype(BF16), ffn_w_out[i, 0].astype(BF16), ln_g[i, 0], ln_b[i, 0])
        if i == 0:
            x = _ffn_layer(x_prompt.reshape(n_prompt, dm), *ffn, out_rows=n_tokens)
            x = _ffn_layer(x_sample.reshape(n_sample, dm), *ffn, out_rows=n_tokens, out_row0=n_prompt, out_buf=x)
        else:
            x = _ffn_layer(x, *ffn)
        j = i // 2
        if i % 2 == 0:
            qkv = _projection(x, attn_w_qkv[j].astype(BF16), BF16)
            att = _dilated_attention(qkv, slopes, seq_tokens)
            x = _projection_ln(x, att, attn_w_o[j].astype(BF16), zero_bias_dm, ln_g[i, 1], ln_b[i, 1])
        else:
            u = _projection_short_conv(x, hy_w_in[j].astype(BF16), hy_b_in[j], hy_conv_w[j], hy_conv_b[j],
                                       seq_tokens)
            filt = (hy_f_w1[j], hy_f_b1[j], hy_f_w2[j], hy_f_b2[j], hy_f_freq[j], hy_f_w3[j])
            z = _hyena_mixer(u, filt, hy_skip[j], seq_tokens, (bp, bs))
            x = _projection_ln(x, z, hy_w_out[j].astype(BF16), hy_b_out[j], ln_g[i, 1], ln_b[i, 1])
        ffn = (ffn_w_in[i, 1].astype(BF16), ffn_w_out[i, 1].astype(BF16), ln_g[i, 2], ln_b[i, 2])
        if i < DEPTH - 1:
            x = _ffn_layer(x, *ffn)

    y_prompt = _ffn_layer(x, *ffn, n_rows=n_prompt)
    y_sample = _ffn_layer(x, *ffn, x_row0=n_prompt, n_rows=n_sample)
    return (y_prompt.reshape(bp, sp, dm), y_sample.reshape(bs, ss, dm))
```

```python
import functools
import math

import jax
import jax.numpy as jnp
import ml_dtypes
import numpy as np
from jax import lax
from jax.experimental import pallas as pl
from jax.experimental.pallas import tpu as pltpu

D_MODEL = 2048
DEPTH = 4
HEAD_DIM = 128
N_SLOTS = D_MODEL // HEAD_DIM
DILATED_GROUPS = ((128, 1), (512, 4), (2048, 16))
N_GROUPS = len(DILATED_GROUPS)
ATTN_WIDTH = N_SLOTS * HEAD_DIM
QKV_WIDTH = 3 * N_GROUPS * ATTN_WIDTH
HALF_WINDOW = 64
NEG_INF = -1e30
D_FF = 5632
HYENA_ORDER = 2
SHORT_CONV = 3
FILTER_BANDS = 16
DECAY_TARGET = 1e-2
DECAY_FAST = 0.3
DECAY_SLOW = 1.5
MAX_DECAY = math.log(DECAY_TARGET) / DECAY_FAST
MIN_DECAY = math.log(DECAY_TARGET) / DECAY_SLOW
DN_ALPHA = (2.0 * DEPTH) ** 0.25
LN_EPS = 1e-5

LANES = 128
FFN_TOKEN_TILE = 512
FFN_HIDDEN_TILE = 512
PROJ_TOKEN_TILE = 1024
PROJ_COLUMN_TILE = 1024
OUT_PROJ_TOKEN_TILE = 512
ATTN_QUERY_TILE = 1024
ATTN_HEADS_PER_STEP = 4
ATTN_BLOCKS = 4
FFT_BLOCKS = 4
COPY_UNROLL = 8
VMEM_LIMIT_BYTES = 48 * 1024 * 1024
BF16 = jnp.bfloat16
F32 = jnp.float32


def _layer_norm_rows(y, g, b):
    mu = jnp.mean(y, axis=-1, keepdims=True)
    c = y - mu
    var = jnp.mean(c * c, axis=-1, keepdims=True)
    return c * lax.rsqrt(var + LN_EPS) * g + b


def _stacked_spec(w, block, index_map):
    _, lead = w
    return pl.BlockSpec((None,) * len(lead) + block, lambda *grid: tuple(lead) + index_map(*grid))


def _seq_position(t0, seq_tokens):
    n_prompt, s_prompt, s_sample = seq_tokens
    seq_len = jnp.where(t0 < n_prompt, s_prompt, s_sample)
    return lax.rem(t0, seq_len), seq_len


def _ffn_kernel(x_ref, wg_ref, wu_ref, wo_ref, g_ref, b_ref, o_ref, xb_ref, *, nk):
    k = pl.program_id(1)

    @pl.when(k == 0)
    def _():
        xb_ref[...] = x_ref[...].astype(BF16)
        o_ref[...] = jnp.zeros_like(o_ref)

    xb = xb_ref[...]
    gate = jnp.dot(xb, wg_ref[...], preferred_element_type=F32)
    up = jnp.dot(xb, wu_ref[...], preferred_element_type=F32)
    act = (gate * jax.nn.sigmoid(gate) * up).astype(BF16)
    o_ref[...] += jnp.dot(act, wo_ref[...], preferred_element_type=F32)

    @pl.when(k == nk - 1)
    def _():
        y = DN_ALPHA * x_ref[...] + 0.5 * o_ref[...]
        o_ref[...] = _layer_norm_rows(y, g_ref[...], b_ref[...])


def _ffn_kernel_aliased(buf_ref, *refs, nk):
    _ffn_kernel(*refs, nk=nk)


def _ffn_layer(x, w_in, w_out, g, b, *, x_row0=0, n_rows=None, out_rows=None, out_row0=0, out_buf=None,
               tm=FFN_TOKEN_TILE, tf=FFN_HIDDEN_TILE):
    dm = x.shape[1]
    n_rows = x.shape[0] if n_rows is None else n_rows
    out_rows = n_rows if out_rows is None else out_rows
    nk = w_out[0].shape[-2] // tf
    x_blk0, o_blk0 = x_row0 // tm, out_row0 // tm
    in_specs = [
        pl.BlockSpec((tm, dm), lambda i, k: (x_blk0 + i, 0)),
        _stacked_spec(w_in, (dm, tf), lambda i, k: (0, k)),
        _stacked_spec(w_in, (dm, tf), lambda i, k: (0, k + nk)),
        _stacked_spec(w_out, (tf, dm), lambda i, k: (k, 0)),
        pl.BlockSpec((1, dm), lambda i, k: (0, 0)),
        pl.BlockSpec((1, dm), lambda i, k: (0, 0)),
    ]
    args = [x, w_in[0], w_in[0], w_out[0], g.reshape(1, dm), b.reshape(1, dm)]
    body, aliases = _ffn_kernel, {}
    if out_buf is not None:
        in_specs = [pl.BlockSpec(memory_space=pl.ANY)] + in_specs
        args = [out_buf] + args
        body, aliases = _ffn_kernel_aliased, {0: 0}
    return pl.pallas_call(
        functools.partial(body, nk=nk),
        grid=(n_rows // tm, nk),
        in_specs=in_specs,
        out_specs=pl.BlockSpec((tm, dm), lambda i, k: (o_blk0 + i, 0)),
        out_shape=jax.ShapeDtypeStruct((out_rows, dm), F32),
        scratch_shapes=[pltpu.VMEM((tm, dm), BF16)],
        input_output_aliases=aliases,
        compiler_params=pltpu.CompilerParams(
            dimension_semantics=("parallel", "arbitrary"),
            vmem_limit_bytes=VMEM_LIMIT_BYTES),
        name="ffn_ln",
    )(*args)


def _proj_kernel(x_ref, w_ref, o_ref, xb_ref):
    @pl.when(pl.program_id(1) == 0)
    def _():
        xb_ref[...] = x_ref[...].astype(BF16)

    o_ref[...] = jnp.dot(xb_ref[...], w_ref[...], preferred_element_type=F32).astype(o_ref.dtype)


def _projection(x, w, out_dtype, *, tm=PROJ_TOKEN_TILE, tn=PROJ_COLUMN_TILE):
    t, dm = x.shape
    n = w[0].shape[-1]
    return pl.pallas_call(
        _proj_kernel,
        grid=(t // tm, n // tn),
        in_specs=[
            pl.BlockSpec((tm, dm), lambda i, j: (i, 0)),
            _stacked_spec(w, (dm, tn), lambda i, j: (0, j)),
        ],
        out_specs=pl.BlockSpec((tm, tn), lambda i, j: (i, j)),
        out_shape=jax.ShapeDtypeStruct((t, n), out_dtype),
        scratch_shapes=[pltpu.VMEM((tm, dm), BF16)],
        compiler_params=pltpu.CompilerParams(
            dimension_semantics=("parallel", "arbitrary"),
            vmem_limit_bytes=VMEM_LIMIT_BYTES),
        name="projection",
    )(x, w[0])


HALO_ROWS = 16


def _proj_conv_kernel(xp_ref, x_ref, xn_ref, w_ref, b_ref, cw_ref, cb_ref, o_ref, xb_ref, *, tm, seq_tokens):
    @pl.when(pl.program_id(1) == 0)
    def _():
        xb_ref[0:HALO_ROWS, :] = xp_ref[...].astype(BF16)
        xb_ref[HALO_ROWS:HALO_ROWS + tm, :] = x_ref[...].astype(BF16)
        xb_ref[HALO_ROWS + tm:, :] = xn_ref[...].astype(BF16)

    rows = tm + 2 * HALO_ROWS
    acc = jnp.dot(xb_ref[...], w_ref[...], preferred_element_type=F32) + b_ref[...]
    pos0, seq_len = _seq_position(pl.program_id(0) * tm, seq_tokens)
    tile_row = lax.broadcasted_iota(jnp.int32, (tm, 1), 0)
    at_seq_start = (tile_row == 0) & (pos0 == 0)
    at_seq_end = (tile_row == tm - 1) & (pos0 + tm == seq_len)
    cur = acc[HALO_ROWS:HALO_ROWS + tm]
    prev = pltpu.roll(acc, 1, axis=0)[HALO_ROWS:HALO_ROWS + tm]
    nxt = pltpu.roll(acc, rows - 1, axis=0)[HALO_ROWS:HALO_ROWS + tm]
    prev = jnp.where(at_seq_start, 0.0, prev)
    nxt = jnp.where(at_seq_end, 0.0, nxt)
    out = prev * cw_ref[0:1, :] + cur * cw_ref[1:2, :] + nxt * cw_ref[2:3, :] + cb_ref[...]
    for s in range(o_ref.shape[0]):
        o_ref[s] = out[:, s * LANES:(s + 1) * LANES]


def _projection_short_conv(x, w, bias, conv_w, conv_b, seq_tokens, *, tm=PROJ_TOKEN_TILE, tn=PROJ_COLUMN_TILE):
    t, dm = x.shape
    n = w[0].shape[-1]
    ratio = tm // HALO_ROWS
    n_halo_blocks = t // HALO_ROWS
    return pl.pallas_call(
        functools.partial(_proj_conv_kernel, tm=tm, seq_tokens=seq_tokens),
        grid=(t // tm, n // tn),
        in_specs=[
            pl.BlockSpec((HALO_ROWS, dm), lambda i, j: (jnp.maximum(i * ratio - 1, 0), 0)),
            pl.BlockSpec((tm, dm), lambda i, j: (i, 0)),
            pl.BlockSpec((HALO_ROWS, dm), lambda i, j: (jnp.minimum((i + 1) * ratio, n_halo_blocks - 1), 0)),
            _stacked_spec(w, (dm, tn), lambda i, j: (0, j)),
            pl.BlockSpec((1, tn), lambda i, j: (0, j)),
            pl.BlockSpec((SHORT_CONV, tn), lambda i, j: (0, j)),
            pl.BlockSpec((1, tn), lambda i, j: (0, j)),
        ],
        out_specs=pl.BlockSpec((tn // LANES, tm, LANES), lambda i, j: (j, i, 0)),
        out_shape=jax.ShapeDtypeStruct((n // LANES, t, LANES), F32),
        scratch_shapes=[pltpu.VMEM((tm + 2 * HALO_ROWS, dm), BF16)],
        compiler_params=pltpu.CompilerParams(
            dimension_semantics=("parallel", "arbitrary"),
            vmem_limit_bytes=VMEM_LIMIT_BYTES),
        name="projection_short_conv",
    )(x, x, x, w[0], bias.reshape(1, n), conv_w, conv_b.reshape(1, n))


def _proj_ln_kernel(x_ref, z_ref, w_ref, bias_ref, g_ref, b_ref, o_ref):
    if len(z_ref.shape) == 3:
        z = jnp.concatenate([z_ref[s].astype(BF16) for s in range(z_ref.shape[0])], axis=1)
    else:
        z = z_ref[...].astype(BF16)
    mix = jnp.dot(z, w_ref[...], preferred_element_type=F32)
    y = DN_ALPHA * x_ref[...] + (mix + bias_ref[...])
    o_ref[...] = _layer_norm_rows(y, g_ref[...], b_ref[...])


def _projection_ln(x, z, w, bias, g, b, *, tm=OUT_PROJ_TOKEN_TILE):
    t, dm = x.shape
    if z.ndim == 3:
        kdim = z.shape[0] * LANES
        z_spec = pl.BlockSpec((z.shape[0], tm, LANES), lambda i: (0, i, 0))
    else:
        kdim = z.shape[1]
        z_spec = pl.BlockSpec((tm, kdim), lambda i: (i, 0))
    return pl.pallas_call(
        _proj_ln_kernel,
        grid=(t // tm,),
        in_specs=[
            pl.BlockSpec((tm, dm), lambda i: (i, 0)),
            z_spec,
            _stacked_spec(w, (kdim, dm), lambda i: (0, 0)),
            pl.BlockSpec((1, dm), lambda i: (0, 0)),
            pl.BlockSpec((1, dm), lambda i: (0, 0)),
            pl.BlockSpec((1, dm), lambda i: (0, 0)),
        ],
        out_specs=pl.BlockSpec((tm, dm), lambda i: (i, 0)),
        out_shape=jax.ShapeDtypeStruct((t, dm), F32),
        compiler_params=pltpu.CompilerParams(
            dimension_semantics=("parallel",),
            vmem_limit_bytes=VMEM_LIMIT_BYTES),
        name="projection_ln",
    )(x, z, w[0], bias.reshape(1, dm), g.reshape(1, dm), b.reshape(1, dm))


def _attn_kernel(slopes_ref, *refs, tq, hg, seq_tokens):
    n_in = 7 * N_GROUPS
    in_refs = refs[:n_in]
    o_ref = refs[n_in]
    qf_ref, kf_ref, vf_ref, og_ref, lg_ref = refs[n_in + 1:]

    pos0, seq_len = _seq_position(pl.program_id(0) * tq, seq_tokens)
    scale = HEAD_DIM ** -0.5

    for hh in range(hg):
        cols = slice(hh * HEAD_DIM, (hh + 1) * HEAD_DIM)
        slope = slopes_ref[pl.program_id(1) * hg + hh]
        for g, (_, d) in enumerate(DILATED_GROUPS):
            q_ref, kp_ref, kc_ref, kn_ref, vp_ref, vc_ref, vn_ref = in_refs[7 * g:7 * g + 7]
            halo = HALF_WINDOW * d
            nq = tq // d
            sb = min(nq, 128)
            nkeys = sb + 2 * HALF_WINDOW
            chain_len = seq_len // d
            cpos0 = pos0 // d

            qf_ref[0:tq, :] = q_ref[:, cols].astype(F32)
            kf_ref[0:halo, :] = kp_ref[:, cols].astype(F32)
            kf_ref[halo:halo + tq, :] = kc_ref[:, cols].astype(F32)
            kf_ref[halo + tq:2 * halo + tq, :] = kn_ref[:, cols].astype(F32)
            vf_ref[0:halo, :] = vp_ref[:, cols].astype(F32)
            vf_ref[halo:halo + tq, :] = vc_ref[:, cols].astype(F32)
            vf_ref[halo + tq:2 * halo + tq, :] = vn_ref[:, cols].astype(F32)

            qi = lax.broadcasted_iota(jnp.int32, (sb, nkeys), 0)
            ki = lax.broadcasted_iota(jnp.int32, (sb, nkeys), 1)
            rel = ki - HALF_WINDOW - qi
            band_bias = jnp.where(jnp.abs(rel) <= HALF_WINDOW,
                                  (-slope) * (jnp.abs(rel) * d).astype(F32), NEG_INF)
            key_col = lax.broadcasted_iota(jnp.int32, (1, nkeys), 1)

            def chain_blocks(it, carry, d=d, sb=sb, nkeys=nkeys, nq=nq, g=g,
                             band_bias=band_bias, key_col=key_col,
                             chain_len=chain_len, cpos0=cpos0):
                rows, scores = [], []
                for j in range(ATTN_BLOCKS):
                    idx = it * ATTN_BLOCKS + j
                    r = idx // (nq // sb)
                    s = idx % (nq // sb)
                    start = r + s * (sb * d)
                    if d == 1:
                        q_rows, k_rows = pl.ds(start, sb), pl.ds(start, nkeys)
                    else:
                        q_rows, k_rows = pl.ds(start, sb, stride=d), pl.ds(start, nkeys, stride=d)
                    qc = qf_ref[q_rows, :].astype(BF16)
                    kc = kf_ref[k_rows, :].astype(BF16)
                    sc = lax.dot_general(qc, kc, (((1,), (1,)), ((), ())), preferred_element_type=F32)
                    rows.append((q_rows, k_rows, s))
                    scores.append(sc)
                probs = []
                for (q_rows, k_rows, s), sc in zip(rows, scores):
                    kpos0 = cpos0 + s * sb - HALF_WINDOW
                    in_seq = (key_col >= -kpos0) & (key_col < chain_len - kpos0)
                    sc = sc * scale + band_bias + jnp.where(in_seq, 0.0, NEG_INF)
                    mx = jnp.max(sc, axis=-1, keepdims=True)
                    p = jnp.exp(sc - mx)
                    den = jnp.sum(p, axis=-1, keepdims=True)
                    probs.append((p.astype(BF16), mx, den))
                outs = [jnp.dot(p, vf_ref[k_rows, :].astype(BF16), preferred_element_type=F32)
                        for (p, _, _), (_, k_rows, _) in zip(probs, rows)]
                for (q_rows, _, _), (_, mx, den), o in zip(rows, probs, outs):
                    og_ref[g, q_rows, :] = o / den
                    lg_ref[g, q_rows, :] = jnp.broadcast_to(mx + jnp.log(den), (sb, HEAD_DIM))
                return carry

            lax.fori_loop(0, d * (nq // sb) // ATTN_BLOCKS, chain_blocks, 0)

        l0, l1, l2 = lg_ref[0], lg_ref[1], lg_ref[2]
        lmax = jnp.maximum(jnp.maximum(l0, l1), l2)
        w0, w1, w2 = jnp.exp(l0 - lmax), jnp.exp(l1 - lmax), jnp.exp(l2 - lmax)
        mixed = (w0 * og_ref[0] + w1 * og_ref[1] + w2 * og_ref[2]) / (w0 + w1 + w2)
        o_ref[:, cols] = mixed.astype(o_ref.dtype)


def _dilated_attention(qkv, slopes, seq_tokens, *, tq=ATTN_QUERY_TILE, hg=ATTN_HEADS_PER_STEP):
    t = qkv.shape[0]
    wcols = hg * HEAD_DIM
    blocks_per_part = ATTN_WIDTH // wcols
    in_specs = []
    args = []
    max_halo = HALF_WINDOW * max(d for _, d in DILATED_GROUPS)
    for g, (_, d) in enumerate(DILATED_GROUPS):
        halo = HALF_WINDOW * d
        ratio = tq // halo
        n_halo_blocks = t // halo

        def col(part, g=g):
            return lambda i, h, *_: (part * N_GROUPS + g) * blocks_per_part + h

        def cur(part, g=g):
            c = col(part, g)
            return pl.BlockSpec((tq, wcols), lambda i, h, *_: (i, c(i, h)))

        def prev(part, g=g, ratio=ratio):
            c = col(part, g)
            return pl.BlockSpec((halo, wcols),
                                lambda i, h, *_: (jnp.maximum(i * ratio - 1, 0), c(i, h)))

        def nxt(part, g=g, ratio=ratio, n_halo_blocks=n_halo_blocks):
            c = col(part, g)
            return pl.BlockSpec((halo, wcols),
                                lambda i, h, *_: (jnp.minimum((i + 1) * ratio, n_halo_blocks - 1), c(i, h)))

        in_specs += [cur(0), prev(1), cur(1), nxt(1), prev(2), cur(2), nxt(2)]
        args += [qkv] * 7

    grid_spec = pltpu.PrefetchScalarGridSpec(
        num_scalar_prefetch=1,
        grid=(t // tq, ATTN_WIDTH // wcols),
        in_specs=in_specs,
        out_specs=pl.BlockSpec((tq, wcols), lambda i, h, *_: (i, h)),
        scratch_shapes=[
            pltpu.VMEM((tq, HEAD_DIM), F32),
            pltpu.VMEM((tq + 2 * max_halo, HEAD_DIM), F32),
            pltpu.VMEM((tq + 2 * max_halo, HEAD_DIM), F32),
            pltpu.VMEM((N_GROUPS, tq, HEAD_DIM), F32),
            pltpu.VMEM((N_GROUPS, tq, HEAD_DIM), F32),
        ],
    )
    return pl.pallas_call(
        functools.partial(_attn_kernel, tq=tq, hg=hg, seq_tokens=seq_tokens),
        grid_spec=grid_spec,
        out_shape=jax.ShapeDtypeStruct((t, ATTN_WIDTH), BF16),
        compiler_params=pltpu.CompilerParams(
            dimension_semantics=("parallel", "arbitrary"),
            vmem_limit_bytes=VMEM_LIMIT_BYTES),
        name="dilated_attention",
    )(slopes, *args)


TWIDDLE_RADIX = 16
PITCH_PAD = 8


def _split_hi_lo_np(m):
    hi = m.astype(ml_dtypes.bfloat16)
    lo = (m - hi.astype(np.float64)).astype(ml_dtypes.bfloat16)
    return np.stack([hi, lo])


@functools.lru_cache(maxsize=None)
def _dft_constants(r):
    h = r // 2
    idx = np.arange(r, dtype=np.float64)
    ang = 2.0 * np.pi * np.outer(idx, idx) / r
    c, s = np.cos(ang), np.sin(ang)
    f1 = np.block([[c[:, :h], s[:, :h]], [-s[:, :h], c[:, :h]]])
    f1_real = np.concatenate([c, -s], axis=0)
    f2 = np.block([[c, s], [-s, c]])
    f2_inv = np.block([[c, -s], [s, c]])
    f1_inv = np.block([[c[:h, :], -s[:h, :]], [s[:h, :], c[:h, :]]])
    n = r * r
    coarse = np.arange(r // TWIDDLE_RADIX, dtype=np.float64)[:, None] * TWIDDLE_RADIX
    fine = np.arange(TWIDDLE_RADIX, dtype=np.float64)[:, None]

    def table(mult):
        a = 2.0 * np.pi * mult * idx[None, :] / n
        t = np.stack([np.cos(a), np.sin(a)], axis=1)
        return np.ascontiguousarray(np.broadcast_to(t[..., None], t.shape + (LANES,))).astype(np.float32)

    return dict(f1=_split_hi_lo_np(f1), f1_real=_split_hi_lo_np(f1_real), f2=_split_hi_lo_np(f2),
                f2_inv=_split_hi_lo_np(f2_inv), f1_inv=_split_hi_lo_np(f1_inv),
                t1=table(coarse), t2=table(fine))


def _split_hi_lo(data):
    d_hi = data.astype(BF16)
    return d_hi, (data - d_hi.astype(F32)).astype(BF16)


def _dot3(m_ref, d_hi, d_lo):
    m_hi, m_lo = m_ref[0], m_ref[1]
    return (jnp.dot(m_hi, d_hi, preferred_element_type=F32)
            + jnp.dot(m_hi, d_lo, preferred_element_type=F32)
            + jnp.dot(m_lo, d_hi, preferred_element_type=F32))


def _staged_loop(n_items, stages):
    def body(it, carry):
        state = [it * FFT_BLOCKS + j for j in range(FFT_BLOCKS)]
        for stage in stages:
            state = [stage(s) for s in state]
        return carry

    lax.fori_loop(0, n_items // FFT_BLOCKS, body, 0)


def _twiddle(t1_ref, t2_ref, idx):
    a = idx // TWIDDLE_RADIX
    b = idx % TWIDDLE_RADIX
    c1, s1 = t1_ref[a, 0], t1_ref[a, 1]
    c2, s2 = t2_ref[b, 0], t2_ref[b, 1]
    return c1 * c2 - s1 * s2, s1 * c2 + c1 * s2


def _lanes(x, half):
    return x[:, half * LANES:(half + 1) * LANES]


def _slab_rows(r, chunk):
    return r * (chunk + PITCH_PAD)


def _blocks_to_rows(src_ref, dst_ref, r):
    chunk = src_ref.shape[1]
    pitch = chunk + PITCH_PAD

    def body(g, carry):
        dst_ref[pl.ds(pl.multiple_of(g * pitch, 8), chunk), :] = src_ref[g]
        return carry

    lax.fori_loop(0, r, body, 0, unroll=COPY_UNROLL)


def _rows_to_blocks(src_ref, dst_ref, r):
    chunk = dst_ref.shape[1]
    pitch = chunk + PITCH_PAD

    def body(g, carry):
        dst_ref[g] = src_ref[pl.ds(pl.multiple_of(g * pitch, 8), chunk), :]
        return carry

    lax.fori_loop(0, r, body, 0, unroll=COPY_UNROLL)


def _forward_n1_store(out, first_n2, n2l, t1_ref, t2_ref, sre_ref, sim_ref, r):
    pitch = r + PITCH_PAD
    for half in range(2):
        re, im = _lanes(out[:r], half), _lanes(out[r:], half)
        n2 = first_n2 + n2l + half
        c, s = _twiddle(t1_ref, t2_ref, n2)
        rows = pl.ds(n2, r, stride=pitch)
        sre_ref[rows, :] = re * c + im * s
        sim_ref[rows, :] = im * c - re * s


def _stage_a(x_ref, x2d, chunk, first_n2, f1_ref, t1_ref, t2_ref, sre_ref, sim_ref, r):
    _blocks_to_rows(x_ref, x2d, r)

    def gather(jj):
        n2l = 2 * jj
        rhs = jnp.concatenate([x2d[pl.ds(n2l, r, stride=chunk + PITCH_PAD), :],
                               x2d[pl.ds(n2l + 1, r, stride=chunk + PITCH_PAD), :]], axis=1)
        return n2l, _split_hi_lo(rhs)

    def transform(state):
        n2l, (d_hi, d_lo) = state
        return n2l, _dot3(f1_ref, d_hi, d_lo)

    def twiddle_store(state):
        n2l, out = state
        _forward_n1_store(out, first_n2, n2l, t1_ref, t2_ref, sre_ref, sim_ref, r)

    _staged_loop(chunk // 2, [gather, transform, twiddle_store])


def _load_k1_pair(sre_ref, sim_ref, k1, r):
    pitch = r + PITCH_PAD
    r0 = pl.multiple_of(k1 * pitch, 8)
    r1 = pl.multiple_of((k1 + 1) * pitch, 8)
    rows = (pl.ds(r0, r), pl.ds(r1, r))
    rhs = jnp.concatenate([
        jnp.concatenate([sre_ref[rows[0], :], sre_ref[rows[1], :]], axis=1),
        jnp.concatenate([sim_ref[rows[0], :], sim_ref[rows[1], :]], axis=1)], axis=0)
    return rhs, rows


def _load_n2_pair(sre_ref, sim_ref, n2, r):
    pitch = r + PITCH_PAD
    rows0 = pl.ds(n2, r, stride=pitch)
    rows1 = pl.ds(n2 + 1, r, stride=pitch)
    return jnp.concatenate([
        jnp.concatenate([sre_ref[rows0, :], sre_ref[rows1, :]], axis=1),
        jnp.concatenate([sim_ref[rows0, :], sim_ref[rows1, :]], axis=1)], axis=0)


def _stage_bc(first_k1, chunk, cs_ref, f2_ref, f2i_ref, t1_ref, t2_ref, sre_ref, sim_ref, r):
    def load(jj):
        k1l = 2 * jj
        rhs, rows = _load_k1_pair(sre_ref, sim_ref, first_k1 + k1l, r)
        return k1l, rows, _split_hi_lo(rhs)

    def forward(state):
        k1l, rows, (d_hi, d_lo) = state
        return k1l, rows, _dot3(f2_ref, d_hi, d_lo)

    def filter_multiply(state):
        k1l, rows, x = state
        p_re, p_im = [], []
        for half in range(2):
            xr, xi = _lanes(x[:r], half), _lanes(x[r:], half)
            crow = pl.ds(pl.multiple_of((k1l + half) * r, 8), r)
            cr, ci = cs_ref[0, crow, :], cs_ref[1, crow, :]
            p_re.append(xr * cr - xi * ci)
            p_im.append(xr * ci + xi * cr)
        prod = jnp.concatenate([jnp.concatenate(p_re, axis=1), jnp.concatenate(p_im, axis=1)], axis=0)
        return k1l, rows, _split_hi_lo(prod)

    def inverse(state):
        k1l, rows, (d_hi, d_lo) = state
        return k1l, rows, _dot3(f2i_ref, d_hi, d_lo)

    def twiddle_store(state):
        k1l, rows, y = state
        for half in range(2):
            re, im = _lanes(y[:r], half), _lanes(y[r:], half)
            c, s = _twiddle(t1_ref, t2_ref, first_k1 + k1l + half)
            sre_ref[rows[half], :] = re * c - im * s
            sim_ref[rows[half], :] = im * c + re * s

    _staged_loop(chunk // 2, [load, forward, filter_multiply, inverse, twiddle_store])


def _hyena_conv_kernel(x_ref, gate_ref, cs_ref, f1_ref, f1i_ref, f2_ref, f2i_ref, t1_ref, t2_ref,
                       o_ref, sre_ref, sim_ref, x2d, g2d, *, r, ca, cb):
    n_a, n_bc = r // ca, r // cb
    p_bc1, p_da, p_bc2, p_d2 = n_a, n_a + n_bc, 2 * n_a + n_bc, 2 * n_a + 2 * n_bc
    step = pl.program_id(2)
    slab_stride = ca + PITCH_PAD

    def inverse_n1_stages(first_n2):
        def gather(jj):
            n2l = 2 * jj
            return n2l, _split_hi_lo(_load_n2_pair(sre_ref, sim_ref, first_n2 + n2l, r))

        def inverse_n1(state):
            n2l, (d_hi, d_lo) = state
            return n2l, _dot3(f1i_ref, d_hi, d_lo)

        return [gather, inverse_n1]

    @pl.when(step < p_bc1)
    def _():
        _stage_a(x_ref, x2d, ca, step * ca, f1_ref, t1_ref, t2_ref, sre_ref, sim_ref, r)

    @pl.when((step >= p_bc1) & (step < p_da))
    def _():
        _stage_bc((step - p_bc1) * cb, cb, cs_ref, f2_ref, f2i_ref, t1_ref, t2_ref, sre_ref, sim_ref, r)

    @pl.when((step >= p_da) & (step < p_bc2))
    def _():
        first_n2 = (step - p_da) * ca
        _blocks_to_rows(gate_ref, g2d, r)

        def gate(state):
            n2l, y = state
            z = jnp.concatenate([g2d[pl.ds(n2l + half, r, stride=slab_stride), :] * _lanes(y, half)
                                 for half in range(2)], axis=1)
            return n2l, _split_hi_lo(z)

        def forward_n1(state):
            n2l, (d_hi, d_lo) = state
            return n2l, _dot3(f1_ref, d_hi, d_lo)

        def twiddle_store(state):
            n2l, out = state
            _forward_n1_store(out, first_n2, n2l, t1_ref, t2_ref, sre_ref, sim_ref, r)

        _staged_loop(ca // 2, inverse_n1_stages(first_n2) + [gate, forward_n1, twiddle_store])

    @pl.when((step >= p_bc2) & (step < p_d2))
    def _():
        _stage_bc((step - p_bc2) * cb, cb, cs_ref, f2_ref, f2i_ref, t1_ref, t2_ref, sre_ref, sim_ref, r)

    @pl.when(step >= p_d2)
    def _():
        first_n2 = (step - p_d2) * ca
        _blocks_to_rows(gate_ref, g2d, r)

        def gate_store(state):
            n2l, y = state
            for half in range(2):
                tok = pl.ds(n2l + half, r, stride=slab_stride)
                x2d[tok, :] = g2d[tok, :] * _lanes(y, half)

        _staged_loop(ca // 2, inverse_n1_stages(first_n2) + [gate_store])
        _rows_to_blocks(x2d, o_ref, r)


def _hyena_conv_kernel_aliased(*refs, **kw):
    _hyena_conv_kernel(*refs[1:], **kw)


def _hyena_long_conv(u, row0, spectrum, r, n_pairs, out_buf=None, *, ca, cb=16):
    n_a, n_bc = r // ca, r // cb
    p_bc1, p_da, p_bc2, p_d2 = n_a, n_a + n_bc, 2 * n_a + n_bc, 2 * n_a + 2 * n_bc
    n_slabs = u.shape[0] // 3
    consts = _dft_constants(r)
    t = u.shape[1]
    uv = u.reshape(u.shape[0], t // r, r, LANES)
    blk0 = row0 // (r * r)

    def gate_index(s, p, st):
        second = st >= p_bc2
        chunk = jnp.where(second, st - p_d2, st - p_da)
        return (jnp.where(second, 2 * n_slabs, n_slabs) + s, blk0 + p, jnp.clip(chunk, 0, n_a - 1), 0)

    def spectrum_index(s, p, st):
        second = st >= p_bc2
        chunk = jnp.where(second, st - p_bc2, st - p_bc1)
        return (jnp.where(second, n_slabs, 0) + s, 0, jnp.clip(chunk, 0, n_bc - 1), 0)

    def full(a):
        return pl.BlockSpec(a.shape, lambda s, p, st, nd=a.ndim: (0,) * nd)

    const_args = [consts["f1"], consts["f1_inv"], consts["f2"], consts["f2_inv"], consts["t1"], consts["t2"]]
    in_specs = [
        pl.BlockSpec((None, r, ca, LANES), lambda s, p, st: (s, blk0 + p, jnp.minimum(st, n_a - 1), 0)),
        pl.BlockSpec((None, r, ca, LANES), gate_index),
        pl.BlockSpec((None, 2, cb * r, LANES), spectrum_index),
    ] + [full(a) for a in const_args]
    args = [uv, uv, spectrum, *const_args]
    body = _hyena_conv_kernel
    aliases = {}
    if out_buf is not None:
        in_specs = [pl.BlockSpec(memory_space=pl.ANY)] + in_specs
        args = [out_buf.reshape(n_slabs, t // r, r, LANES)] + args
        body = _hyena_conv_kernel_aliased
        aliases = {0: 0}
    out = pl.pallas_call(
        functools.partial(body, r=r, ca=ca, cb=cb),
        grid=(n_slabs, n_pairs, 3 * n_a + 2 * n_bc),
        in_specs=in_specs,
        out_specs=pl.BlockSpec((None, r, ca, LANES),
                               lambda s, p, st: (s, blk0 + p, jnp.clip(st - p_d2, 0, n_a - 1), 0)),
        out_shape=jax.ShapeDtypeStruct((n_slabs, t // r, r, LANES), F32),
        scratch_shapes=[pltpu.VMEM((r * (r + PITCH_PAD), LANES), F32),
                        pltpu.VMEM((r * (r + PITCH_PAD), LANES), F32),
                        pltpu.VMEM((_slab_rows(r, ca), LANES), F32),
                        pltpu.VMEM((_slab_rows(r, ca), LANES), F32)],
        input_output_aliases=aliases,
        compiler_params=pltpu.CompilerParams(
            dimension_semantics=("parallel", "arbitrary", "arbitrary"),
            vmem_limit_bytes=VMEM_LIMIT_BYTES),
        name="hyena_long_conv",
    )(*args)
    return out.reshape(n_slabs, t, LANES)


def _fft_spectrum_kernel(c_ref, scale_ref, shift_ref, f1_ref, f2_ref, t1_ref, t2_ref, o_ref, sre_ref, sim_ref, c2d,
                         *, r, ca, cb):
    n_a = r // ca
    step = pl.program_id(2)

    @pl.when(step < n_a)
    def _():
        _stage_a(c_ref, c2d, ca, step * ca, f1_ref, t1_ref, t2_ref, sre_ref, sim_ref, r)

    @pl.when(step >= n_a)
    def _():
        first_k1 = (step - n_a) * cb

        def load(jj):
            k1l = 2 * jj
            rhs, _ = _load_k1_pair(sre_ref, sim_ref, first_k1 + k1l, r)
            return k1l, _split_hi_lo(rhs)

        def forward(state):
            k1l, (d_hi, d_lo) = state
            return k1l, _dot3(f2_ref, d_hi, d_lo)

        def store(state):
            k1l, x = state
            for half in range(2):
                orow = pl.ds(pl.multiple_of((k1l + half) * r, 8), r)
                o_ref[0, orow, :] = _lanes(x[:r], half) * scale_ref[...] + shift_ref[...]
                o_ref[1, orow, :] = _lanes(x[r:], half) * scale_ref[...]

        _staged_loop(cb // 2, [load, forward, store])


def _fft_spectrum(c, scale, shift, r, *, ca=32, cb=16):
    n_slabs, n, _ = c.shape
    width = n_slabs * LANES
    n_a, n_bc = r // ca, r // cb
    consts = _dft_constants(r)
    cv = c.reshape(n_slabs, r, r, LANES)

    def full(a):
        return pl.BlockSpec(a.shape, lambda s, o, st, nd=a.ndim: (0,) * nd)

    const_args = [consts["f1_real"], consts["f2"], consts["t1"], consts["t2"]]
    return pl.pallas_call(
        functools.partial(_fft_spectrum_kernel, r=r, ca=ca, cb=cb),
        grid=(n_slabs, 1, n_a + n_bc),
        in_specs=[pl.BlockSpec((None, r, ca, LANES), lambda s, o, st: (s, 0, jnp.minimum(st, n_a - 1), 0)),
                  pl.BlockSpec((1, LANES), lambda s, o, st: (0, s)),
                  pl.BlockSpec((1, LANES), lambda s, o, st: (0, s))]
        + [full(a) for a in const_args],
        out_specs=pl.BlockSpec((None, 2, cb * r, LANES), lambda s, o, st: (s, 0, jnp.maximum(st - n_a, 0), 0)),
        out_shape=jax.ShapeDtypeStruct((n_slabs, 2, n, LANES), F32),
        scratch_shapes=[pltpu.VMEM((r * (r + PITCH_PAD), LANES), F32),
                        pltpu.VMEM((r * (r + PITCH_PAD), LANES), F32),
                        pltpu.VMEM((_slab_rows(r, ca), LANES), F32)],
        compiler_params=pltpu.CompilerParams(
            dimension_semantics=("parallel", "arbitrary", "arbitrary"),
            vmem_limit_bytes=VMEM_LIMIT_BYTES),
        name="fft_spectrum",
    )(cv, scale.reshape(1, width), shift.reshape(1, width), *const_args)


FILTER_EMB = 2 * FILTER_BANDS + 1


def _dot_f32(a, b):
    a_hi, a_lo = _split_hi_lo(a)
    b_hi, b_lo = _split_hi_lo(b)
    return (jnp.dot(a_hi, b_hi, preferred_element_type=F32)
            + jnp.dot(a_hi, b_lo, preferred_element_type=F32)
            + jnp.dot(a_lo, b_hi, preferred_element_type=F32))


def _filter_kernel(bands_ref, w1_ref, b1_ref, w2_ref, b2_ref, freq_ref, w3f_ref, w3b_ref, delta_ref,
                   c_ref, norm_ref, hid_ref, *, length, tr):
    j = pl.program_id(0)
    i = pl.program_id(1)
    row = i * tr + lax.broadcasted_iota(jnp.int32, (tr, 1), 0)
    pos_idx = jnp.where(row < length, row, 2 * length - row)
    mf = pos_idx.astype(F32)
    t = mf / (length - 1)
    rows = pl.ds(pl.multiple_of(i * tr, 8), tr)

    @pl.when(j == 0)
    def _():
        w = 2.0 * math.pi * mf / length
        arg = w * bands_ref[...]
        lane = lax.broadcasted_iota(jnp.int32, (1, LANES), 1)
        pos = jnp.where(lane == 0, t,
                        jnp.where(lane <= FILTER_BANDS, jnp.cos(arg),
                                  jnp.where(lane < FILTER_EMB, -jnp.sin(arg), 0.0)))
        h = jnp.sin(freq_ref[...] * (_dot_f32(pos, w1_ref[...]) + b1_ref[...]))
        h = jnp.sin(freq_ref[...] * (_dot_f32(h, w2_ref[...]) + b2_ref[...]))
        hid_ref[rows, :] = h

    h = hid_ref[rows, :]
    decay = jnp.exp(-t * delta_ref[...])
    fwd = _dot_f32(h, w3f_ref[...]) * decay
    bwd = _dot_f32(h, w3b_ref[...]) * decay
    main = jnp.where(row < length, fwd, jnp.where(row == length, 0.0, bwd))
    extra = jnp.where(row == 0, bwd, 0.0)
    taps = main + extra
    for s in range(c_ref.shape[0]):
        c_ref[s] = taps[:, s * LANES:(s + 1) * LANES]
    contrib = jnp.abs(main) + jnp.abs(extra)
    part = contrib[0:8]
    for k in range(1, tr // 8):
        part = part + contrib[8 * k:8 * (k + 1)]

    @pl.when(i == 0)
    def _():
        norm_ref[...] = part

    @pl.when(i > 0)
    def _():
        norm_ref[...] += part


def _hyena_filter_kernel(length, f_w1, f_b1, f_w2, f_b2, f_freq, f_w3, *, tr=512, tc=1024):
    hidden = f_w1.shape[1]
    dm = f_w3.shape[1] // (2 * HYENA_ORDER)
    width = HYENA_ORDER * dm
    tr, tc = min(tr, 2 * length), min(tc, width)
    bands = jnp.linspace(1e-4, FILTER_BANDS - 1, FILTER_BANDS, dtype=F32)
    bands_row = jnp.zeros((1, LANES), F32).at[0, 1:FILTER_BANDS + 1].set(bands)
    bands_row = bands_row.at[0, FILTER_BANDS + 1:FILTER_EMB].set(bands)

    def pad_to(a, rows, cols):
        return jnp.zeros((rows, cols), F32).at[:a.shape[0], :a.shape[1]].set(a)

    w3 = f_w3.reshape(hidden, HYENA_ORDER, 2, dm)
    w3f = pad_to(w3[:, :, 0].reshape(hidden, width), LANES, width)
    w3b = pad_to(w3[:, :, 1].reshape(hidden, width), LANES, width)
    deltas = jnp.abs(jnp.linspace(MIN_DECAY, MAX_DECAY, dm, dtype=F32))
    delta_row = jnp.tile(deltas, HYENA_ORDER).reshape(1, width)
    small = [bands_row, pad_to(f_w1, LANES, LANES), pad_to(f_b1[None], 1, LANES), pad_to(f_w2, LANES, LANES),
             pad_to(f_b2[None], 1, LANES), pad_to(f_freq[None], 1, LANES)]
    c, norm = pl.pallas_call(
        functools.partial(_filter_kernel, length=length, tr=tr),
        grid=(width // tc, 2 * length // tr),
        in_specs=[pl.BlockSpec(a.shape, lambda j, i: (0, 0)) for a in small] + [
            pl.BlockSpec((LANES, tc), lambda j, i: (0, j)),
            pl.BlockSpec((LANES, tc), lambda j, i: (0, j)),
            pl.BlockSpec((1, tc), lambda j, i: (0, j)),
        ],
        out_specs=[pl.BlockSpec((tc // LANES, tr, LANES), lambda j, i: (j, i, 0)),
                   pl.BlockSpec((8, tc), lambda j, i: (0, j))],
        out_shape=[jax.ShapeDtypeStruct((width // LANES, 2 * length, LANES), F32),
                   jax.ShapeDtypeStruct((8, width), F32)],
        scratch_shapes=[pltpu.VMEM((2 * length, LANES), F32)],
        compiler_params=pltpu.CompilerParams(
            dimension_semantics=("arbitrary", "arbitrary"),
            vmem_limit_bytes=VMEM_LIMIT_BYTES),
        name="hyena_filter",
    )(*small, w3f, w3b, delta_row)
    return c, jnp.sum(norm, axis=0)


def _filter_spectra(length, r, filt_params, skip):
    c, norm = _hyena_filter_kernel(length, *filt_params)
    n = 2 * length
    return _fft_spectrum(c, 1.0 / (n * norm), skip.reshape(-1) / n, r)


def _hyena_mixer(u, filt_params, skip, seq_tokens, batches):
    n_prompt, s_prompt, s_sample = seq_tokens
    z = None
    for seq_len, n_batch, row0 in ((s_prompt, batches[0], 0), (s_sample, batches[1], n_prompt)):
        r = math.isqrt(2 * seq_len)
        assert r * r == 2 * seq_len and n_batch % 2 == 0 and row0 % (r * r) == 0
        spec = _filter_spectra(seq_len, r, filt_params, skip)
        chunk = 16 if r > 64 else 32
        z = _hyena_long_conv(u, row0, spec, r, n_batch // 2, z, ca=chunk)
    return z


def kernel(x_prompt, x_sample, ln_g, ln_b, ffn_w_in, ffn_w_out, attn_w_qkv, attn_w_o, hy_w_in, hy_b_in,
           hy_conv_w, hy_conv_b, hy_f_w1, hy_f_b1, hy_f_w2, hy_f_b2, hy_f_freq, hy_f_w3, hy_skip,
           hy_w_out, hy_b_out):
    bp, sp, dm = x_prompt.shape
    bs, ss, _ = x_sample.shape
    n_prompt = bp * sp
    n_sample = bs * ss
    seq_tokens = (n_prompt, sp, ss)
    n_tokens = n_prompt + n_sample
    slopes = jnp.exp2(-8.0 * jnp.arange(1, N_SLOTS + 1, dtype=F32) / N_SLOTS)
    zero_bias_dm = jnp.zeros((dm,), F32)
    ffn_w_in, ffn_w_out, attn_w_qkv, attn_w_o, hy_w_in, hy_w_out = (
        w.astype(BF16) for w in (ffn_w_in, ffn_w_out, attn_w_qkv, attn_w_o, hy_w_in, hy_w_out))

    x = None
    for i in range(DEPTH):
        ffn = ((ffn_w_in, (i, 0)), (ffn_w_out, (i, 0)), ln_g[i, 0], ln_b[i, 0])
        if i == 0:
            x = _ffn_layer(x_prompt.reshape(n_prompt, dm), *ffn, out_rows=n_tokens)
            x = _ffn_layer(x_sample.reshape(n_sample, dm), *ffn, out_rows=n_tokens, out_row0=n_prompt, out_buf=x)
        else:
            x = _ffn_layer(x, *ffn)
        j = i // 2
        if i % 2 == 0:
            qkv = _projection(x, (attn_w_qkv, (j,)), BF16)
            att = _dilated_attention(qkv, slopes, seq_tokens)
            x = _projection_ln(x, att, (attn_w_o, (j,)), zero_bias_dm, ln_g[i, 1], ln_b[i, 1])
        else:
            u = _projection_short_conv(x, (hy_w_in, (j,)), hy_b_in[j], hy_conv_w[j], hy_conv_b[j], seq_tokens)
            filt = (hy_f_w1[j], hy_f_b1[j], hy_f_w2[j], hy_f_b2[j], hy_f_freq[j], hy_f_w3[j])
            z = _hyena_mixer(u, filt, hy_skip[j], seq_tokens, (bp, bs))
            x = _projection_ln(x, z, (hy_w_out, (j,)), hy_b_out[j], ln_g[i, 1], ln_b[i, 1])
        ffn = ((ffn_w_in, (i, 1)), (ffn_w_out, (i, 1)), ln_g[i, 2], ln_b[i, 2])
        if i < DEPTH - 1:
            x = _ffn_layer(x, *ffn)

    y_prompt = _ffn_layer(x, *ffn, n_rows=n_prompt)
    y_sample = _ffn_layer(x, *ffn, x_row0=n_prompt, n_rows=n_sample)
    return (y_prompt.reshape(bp, sp, dm), y_sample.reshape(bs, ss, dm))
```

```python
import functools
import math

import jax
import jax.numpy as jnp
import ml_dtypes
import numpy as np
from jax import lax
from jax.experimental import pallas as pl
from jax.experimental.pallas import tpu as pltpu

D_MODEL = 2048
DEPTH = 4
HEAD_DIM = 128
N_SLOTS = D_MODEL // HEAD_DIM
DILATED_GROUPS = ((128, 1), (512, 4), (2048, 16))
N_GROUPS = len(DILATED_GROUPS)
ATTN_WIDTH = N_SLOTS * HEAD_DIM
QKV_WIDTH = 3 * N_GROUPS * ATTN_WIDTH
HALF_WINDOW = 64
NEG_INF = -1e30
D_FF = 5632
HYENA_ORDER = 2
SHORT_CONV = 3
FILTER_BANDS = 16
DECAY_TARGET = 1e-2
DECAY_FAST = 0.3
DECAY_SLOW = 1.5
MAX_DECAY = math.log(DECAY_TARGET) / DECAY_FAST
MIN_DECAY = math.log(DECAY_TARGET) / DECAY_SLOW
DN_ALPHA = (2.0 * DEPTH) ** 0.25
LN_EPS = 1e-5

LANES = 128
FFN_TOKEN_TILE = 512
FFN_HIDDEN_TILE = 512
PROJ_TOKEN_TILE = 1024
PROJ_COLUMN_TILE = 1024
OUT_PROJ_TOKEN_TILE = 512
ATTN_QUERY_TILE = 1024
ATTN_HEADS_PER_STEP = 4
ATTN_BLOCKS = 8
FFT_BLOCKS = 4
COPY_UNROLL = 8
VMEM_LIMIT_BYTES = 48 * 1024 * 1024
BF16 = jnp.bfloat16
F32 = jnp.float32


def _layer_norm_rows(y, g, b):
    mu = jnp.mean(y, axis=-1, keepdims=True)
    c = y - mu
    var = jnp.mean(c * c, axis=-1, keepdims=True)
    return c * lax.rsqrt(var + LN_EPS) * g + b


def _stacked_spec(w, block, index_map):
    _, lead = w
    return pl.BlockSpec((None,) * len(lead) + block, lambda *grid: tuple(lead) + index_map(*grid))


def _seq_position(t0, seq_tokens):
    n_prompt, s_prompt, s_sample = seq_tokens
    seq_len = jnp.where(t0 < n_prompt, s_prompt, s_sample)
    return lax.rem(t0, seq_len), seq_len


def _ffn_kernel(x_ref, wg_ref, wu_ref, wo_ref, g_ref, b_ref, o_ref, xb_ref, *, nk):
    k = pl.program_id(1)

    @pl.when(k == 0)
    def _():
        xb_ref[...] = x_ref[...].astype(BF16)
        o_ref[...] = jnp.zeros_like(o_ref)

    xb = xb_ref[...]
    gate = jnp.dot(xb, wg_ref[...], preferred_element_type=F32)
    up = jnp.dot(xb, wu_ref[...], preferred_element_type=F32)
    act = (gate * jax.nn.sigmoid(gate) * up).astype(BF16)
    o_ref[...] += jnp.dot(act, wo_ref[...], preferred_element_type=F32)

    @pl.when(k == nk - 1)
    def _():
        y = DN_ALPHA * x_ref[...] + 0.5 * o_ref[...]
        o_ref[...] = _layer_norm_rows(y, g_ref[...], b_ref[...])


def _ffn_kernel_aliased(buf_ref, *refs, nk):
    _ffn_kernel(*refs, nk=nk)


def _ffn_layer(x, w_in, w_out, g, b, *, x_row0=0, n_rows=None, out_rows=None, out_row0=0, out_buf=None,
               tm=FFN_TOKEN_TILE, tf=FFN_HIDDEN_TILE):
    dm = x.shape[1]
    n_rows = x.shape[0] if n_rows is None else n_rows
    out_rows = n_rows if out_rows is None else out_rows
    nk = w_out[0].shape[-2] // tf
    x_blk0, o_blk0 = x_row0 // tm, out_row0 // tm
    in_specs = [
        pl.BlockSpec((tm, dm), lambda i, k: (x_blk0 + i, 0)),
        _stacked_spec(w_in, (dm, tf), lambda i, k: (0, k)),
        _stacked_spec(w_in, (dm, tf), lambda i, k: (0, k + nk)),
        _stacked_spec(w_out, (tf, dm), lambda i, k: (k, 0)),
        pl.BlockSpec((1, dm), lambda i, k: (0, 0)),
        pl.BlockSpec((1, dm), lambda i, k: (0, 0)),
    ]
    args = [x, w_in[0], w_in[0], w_out[0], g.reshape(1, dm), b.reshape(1, dm)]
    body, aliases = _ffn_kernel, {}
    if out_buf is not None:
        in_specs = [pl.BlockSpec(memory_space=pl.ANY)] + in_specs
        args = [out_buf] + args
        body, aliases = _ffn_kernel_aliased, {0: 0}
    return pl.pallas_call(
        functools.partial(body, nk=nk),
        grid=(n_rows // tm, nk),
        in_specs=in_specs,
        out_specs=pl.BlockSpec((tm, dm), lambda i, k: (o_blk0 + i, 0)),
        out_shape=jax.ShapeDtypeStruct((out_rows, dm), F32),
        scratch_shapes=[pltpu.VMEM((tm, dm), BF16)],
        input_output_aliases=aliases,
        compiler_params=pltpu.CompilerParams(
            dimension_semantics=("parallel", "arbitrary"),
            vmem_limit_bytes=VMEM_LIMIT_BYTES),
        name="ffn_ln",
    )(*args)


def _proj_kernel(x_ref, w_ref, o_ref, xb_ref):
    @pl.when(pl.program_id(1) == 0)
    def _():
        xb_ref[...] = x_ref[...].astype(BF16)

    o_ref[...] = jnp.dot(xb_ref[...], w_ref[...], preferred_element_type=F32).astype(o_ref.dtype)


def _projection(x, w, out_dtype, *, tm=PROJ_TOKEN_TILE, tn=PROJ_COLUMN_TILE):
    t, dm = x.shape
    n = w[0].shape[-1]
    return pl.pallas_call(
        _proj_kernel,
        grid=(t // tm, n // tn),
        in_specs=[
            pl.BlockSpec((tm, dm), lambda i, j: (i, 0)),
            _stacked_spec(w, (dm, tn), lambda i, j: (0, j)),
        ],
        out_specs=pl.BlockSpec((tm, tn), lambda i, j: (i, j)),
        out_shape=jax.ShapeDtypeStruct((t, n), out_dtype),
        scratch_shapes=[pltpu.VMEM((tm, dm), BF16)],
        compiler_params=pltpu.CompilerParams(
            dimension_semantics=("parallel", "arbitrary"),
            vmem_limit_bytes=VMEM_LIMIT_BYTES),
        name="projection",
    )(x, w[0])


HALO_ROWS = 16


def _proj_conv_kernel(xp_ref, x_ref, xn_ref, w_ref, b_ref, cw_ref, cb_ref, o_ref, xb_ref, *, tm, seq_tokens):
    @pl.when(pl.program_id(1) == 0)
    def _():
        xb_ref[0:HALO_ROWS, :] = xp_ref[...].astype(BF16)
        xb_ref[HALO_ROWS:HALO_ROWS + tm, :] = x_ref[...].astype(BF16)
        xb_ref[HALO_ROWS + tm:, :] = xn_ref[...].astype(BF16)

    rows = tm + 2 * HALO_ROWS
    acc = jnp.dot(xb_ref[...], w_ref[...], preferred_element_type=F32) + b_ref[...]
    pos0, seq_len = _seq_position(pl.program_id(0) * tm, seq_tokens)
    tile_row = lax.broadcasted_iota(jnp.int32, (tm, 1), 0)
    at_seq_start = (tile_row == 0) & (pos0 == 0)
    at_seq_end = (tile_row == tm - 1) & (pos0 + tm == seq_len)
    cur = acc[HALO_ROWS:HALO_ROWS + tm]
    prev = pltpu.roll(acc, 1, axis=0)[HALO_ROWS:HALO_ROWS + tm]
    nxt = pltpu.roll(acc, rows - 1, axis=0)[HALO_ROWS:HALO_ROWS + tm]
    prev = jnp.where(at_seq_start, 0.0, prev)
    nxt = jnp.where(at_seq_end, 0.0, nxt)
    out = prev * cw_ref[0:1, :] + cur * cw_ref[1:2, :] + nxt * cw_ref[2:3, :] + cb_ref[...]
    for s in range(o_ref.shape[0]):
        o_ref[s] = out[:, s * LANES:(s + 1) * LANES]


def _projection_short_conv(x, w, bias, conv_w, conv_b, seq_tokens, *, tm=PROJ_TOKEN_TILE, tn=PROJ_COLUMN_TILE):
    t, dm = x.shape
    n = w[0].shape[-1]
    ratio = tm // HALO_ROWS
    n_halo_blocks = t // HALO_ROWS
    return pl.pallas_call(
        functools.partial(_proj_conv_kernel, tm=tm, seq_tokens=seq_tokens),
        grid=(t // tm, n // tn),
        in_specs=[
            pl.BlockSpec((HALO_ROWS, dm), lambda i, j: (jnp.maximum(i * ratio - 1, 0), 0)),
            pl.BlockSpec((tm, dm), lambda i, j: (i, 0)),
            pl.BlockSpec((HALO_ROWS, dm), lambda i, j: (jnp.minimum((i + 1) * ratio, n_halo_blocks - 1), 0)),
            _stacked_spec(w, (dm, tn), lambda i, j: (0, j)),
            pl.BlockSpec((1, tn), lambda i, j: (0, j)),
            pl.BlockSpec((SHORT_CONV, tn), lambda i, j: (0, j)),
            pl.BlockSpec((1, tn), lambda i, j: (0, j)),
        ],
        out_specs=pl.BlockSpec((tn // LANES, tm, LANES), lambda i, j: (j, i, 0)),
        out_shape=jax.ShapeDtypeStruct((n // LANES, t, LANES), F32),
        scratch_shapes=[pltpu.VMEM((tm + 2 * HALO_ROWS, dm), BF16)],
        compiler_params=pltpu.CompilerParams(
            dimension_semantics=("parallel", "arbitrary"),
            vmem_limit_bytes=VMEM_LIMIT_BYTES),
        name="projection_short_conv",
    )(x, x, x, w[0], bias.reshape(1, n), conv_w, conv_b.reshape(1, n))


def _proj_ln_kernel(x_ref, z_ref, w_ref, bias_ref, g_ref, b_ref, o_ref):
    if len(z_ref.shape) == 3:
        z = jnp.concatenate([z_ref[s].astype(BF16) for s in range(z_ref.shape[0])], axis=1)
    else:
        z = z_ref[...].astype(BF16)
    mix = jnp.dot(z, w_ref[...], preferred_element_type=F32)
    y = DN_ALPHA * x_ref[...] + (mix + bias_ref[...])
    o_ref[...] = _layer_norm_rows(y, g_ref[...], b_ref[...])


def _projection_ln(x, z, w, bias, g, b, *, tm=OUT_PROJ_TOKEN_TILE):
    t, dm = x.shape
    if z.ndim == 3:
        kdim = z.shape[0] * LANES
        z_spec = pl.BlockSpec((z.shape[0], tm, LANES), lambda i: (0, i, 0))
    else:
        kdim = z.shape[1]
        z_spec = pl.BlockSpec((tm, kdim), lambda i: (i, 0))
    return pl.pallas_call(
        _proj_ln_kernel,
        grid=(t // tm,),
        in_specs=[
            pl.BlockSpec((tm, dm), lambda i: (i, 0)),
            z_spec,
            _stacked_spec(w, (kdim, dm), lambda i: (0, 0)),
            pl.BlockSpec((1, dm), lambda i: (0, 0)),
            pl.BlockSpec((1, dm), lambda i: (0, 0)),
            pl.BlockSpec((1, dm), lambda i: (0, 0)),
        ],
        out_specs=pl.BlockSpec((tm, dm), lambda i: (i, 0)),
        out_shape=jax.ShapeDtypeStruct((t, dm), F32),
        compiler_params=pltpu.CompilerParams(
            dimension_semantics=("parallel",),
            vmem_limit_bytes=VMEM_LIMIT_BYTES),
        name="projection_ln",
    )(x, z, w[0], bias.reshape(1, dm), g.reshape(1, dm), b.reshape(1, dm))


def _attn_kernel(slopes_ref, *refs, tq, hg, seq_tokens):
    n_in = 7 * N_GROUPS
    in_refs = refs[:n_in]
    o_ref = refs[n_in]
    qf_ref, kf_ref, vf_ref, og_ref, lg_ref = refs[n_in + 1:]

    pos0, seq_len = _seq_position(pl.program_id(0) * tq, seq_tokens)
    scale = HEAD_DIM ** -0.5

    for hh in range(hg):
        cols = slice(hh * HEAD_DIM, (hh + 1) * HEAD_DIM)
        slope = slopes_ref[pl.program_id(1) * hg + hh]
        for g, (_, d) in enumerate(DILATED_GROUPS):
            q_ref, kp_ref, kc_ref, kn_ref, vp_ref, vc_ref, vn_ref = in_refs[7 * g:7 * g + 7]
            halo = HALF_WINDOW * d
            nq = tq // d
            sb = min(nq, 128)
            nkeys = sb + 2 * HALF_WINDOW
            chain_len = seq_len // d
            cpos0 = pos0 // d

            qf_ref[0:tq, :] = q_ref[:, cols].astype(F32)
            kf_ref[0:halo, :] = kp_ref[:, cols].astype(F32)
            kf_ref[halo:halo + tq, :] = kc_ref[:, cols].astype(F32)
            kf_ref[halo + tq:2 * halo + tq, :] = kn_ref[:, cols].astype(F32)
            vf_ref[0:halo, :] = vp_ref[:, cols].astype(F32)
            vf_ref[halo:halo + tq, :] = vc_ref[:, cols].astype(F32)
            vf_ref[halo + tq:2 * halo + tq, :] = vn_ref[:, cols].astype(F32)

            qi = lax.broadcasted_iota(jnp.int32, (sb, nkeys), 0)
            ki = lax.broadcasted_iota(jnp.int32, (sb, nkeys), 1)
            rel = ki - HALF_WINDOW - qi
            band_bias = jnp.where(jnp.abs(rel) <= HALF_WINDOW,
                                  (-slope) * (jnp.abs(rel) * d).astype(F32), NEG_INF)
            key_col = lax.broadcasted_iota(jnp.int32, (1, nkeys), 1)

            def chain_blocks(it, carry, d=d, sb=sb, nkeys=nkeys, nq=nq, g=g,
                             band_bias=band_bias, key_col=key_col,
                             chain_len=chain_len, cpos0=cpos0):
                rows, scores = [], []
                for j in range(ATTN_BLOCKS):
                    idx = it * ATTN_BLOCKS + j
                    r = idx // (nq // sb)
                    s = idx % (nq // sb)
                    start = r + s * (sb * d)
                    if d == 1:
                        q_rows, k_rows = pl.ds(start, sb), pl.ds(start, nkeys)
                    else:
                        q_rows, k_rows = pl.ds(start, sb, stride=d), pl.ds(start, nkeys, stride=d)
                    qc = qf_ref[q_rows, :].astype(BF16)
                    kc = kf_ref[k_rows, :].astype(BF16)
                    sc = lax.dot_general(qc, kc, (((1,), (1,)), ((), ())), preferred_element_type=F32)
                    rows.append((q_rows, k_rows, s))
                    scores.append(sc)
                probs = []
                for (q_rows, k_rows, s), sc in zip(rows, scores):
                    kpos0 = cpos0 + s * sb - HALF_WINDOW
                    in_seq = (key_col >= -kpos0) & (key_col < chain_len - kpos0)
                    sc = sc * scale + band_bias + jnp.where(in_seq, 0.0, NEG_INF)
                    mx = jnp.max(sc, axis=-1, keepdims=True)
                    p = jnp.exp(sc - mx)
                    den = jnp.sum(p, axis=-1, keepdims=True)
                    probs.append((p.astype(BF16), mx, den))
                outs = [jnp.dot(p, vf_ref[k_rows, :].astype(BF16), preferred_element_type=F32)
                        for (p, _, _), (_, k_rows, _) in zip(probs, rows)]
                for (q_rows, _, _), (_, mx, den), o in zip(rows, probs, outs):
                    og_ref[g, q_rows, :] = o / den
                    lg_ref[g, q_rows, :] = jnp.broadcast_to(mx + jnp.log(den), (sb, HEAD_DIM))
                return carry

            lax.fori_loop(0, d * (nq // sb) // ATTN_BLOCKS, chain_blocks, 0)

        l0, l1, l2 = lg_ref[0], lg_ref[1], lg_ref[2]
        lmax = jnp.maximum(jnp.maximum(l0, l1), l2)
        w0, w1, w2 = jnp.exp(l0 - lmax), jnp.exp(l1 - lmax), jnp.exp(l2 - lmax)
        mixed = (w0 * og_ref[0] + w1 * og_ref[1] + w2 * og_ref[2]) / (w0 + w1 + w2)
        o_ref[:, cols] = mixed.astype(o_ref.dtype)


def _dilated_attention(qkv, slopes, seq_tokens, *, tq=ATTN_QUERY_TILE, hg=ATTN_HEADS_PER_STEP):
    t = qkv.shape[0]
    wcols = hg * HEAD_DIM
    blocks_per_part = ATTN_WIDTH // wcols
    in_specs = []
    args = []
    max_halo = HALF_WINDOW * max(d for _, d in DILATED_GROUPS)
    for g, (_, d) in enumerate(DILATED_GROUPS):
        halo = HALF_WINDOW * d
        ratio = tq // halo
        n_halo_blocks = t // halo

        def col(part, g=g):
            return lambda i, h, *_: (part * N_GROUPS + g) * blocks_per_part + h

        def cur(part, g=g):
            c = col(part, g)
            return pl.BlockSpec((tq, wcols), lambda i, h, *_: (i, c(i, h)))

        def prev(part, g=g, ratio=ratio):
            c = col(part, g)
            return pl.BlockSpec((halo, wcols),
                                lambda i, h, *_: (jnp.maximum(i * ratio - 1, 0), c(i, h)))

        def nxt(part, g=g, ratio=ratio, n_halo_blocks=n_halo_blocks):
            c = col(part, g)
            return pl.BlockSpec((halo, wcols),
                                lambda i, h, *_: (jnp.minimum((i + 1) * ratio, n_halo_blocks - 1), c(i, h)))

        in_specs += [cur(0), prev(1), cur(1), nxt(1), prev(2), cur(2), nxt(2)]
        args += [qkv] * 7

    grid_spec = pltpu.PrefetchScalarGridSpec(
        num_scalar_prefetch=1,
        grid=(t // tq, ATTN_WIDTH // wcols),
        in_specs=in_specs,
        out_specs=pl.BlockSpec((tq, wcols), lambda i, h, *_: (i, h)),
        scratch_shapes=[
            pltpu.VMEM((tq, HEAD_DIM), F32),
            pltpu.VMEM((tq + 2 * max_halo, HEAD_DIM), F32),
            pltpu.VMEM((tq + 2 * max_halo, HEAD_DIM), F32),
            pltpu.VMEM((N_GROUPS, tq, HEAD_DIM), F32),
            pltpu.VMEM((N_GROUPS, tq, HEAD_DIM), F32),
        ],
    )
    return pl.pallas_call(
        functools.partial(_attn_kernel, tq=tq, hg=hg, seq_tokens=seq_tokens),
        grid_spec=grid_spec,
        out_shape=jax.ShapeDtypeStruct((t, ATTN_WIDTH), BF16),
        compiler_params=pltpu.CompilerParams(
            dimension_semantics=("parallel", "arbitrary"),
            vmem_limit_bytes=VMEM_LIMIT_BYTES),
        name="dilated_attention",
    )(slopes, *args)


TWIDDLE_RADIX = 16
PITCH_PAD = 8


def _split_hi_lo_np(m):
    hi = m.astype(ml_dtypes.bfloat16)
    lo = (m - hi.astype(np.float64)).astype(ml_dtypes.bfloat16)
    return np.stack([hi, lo])


@functools.lru_cache(maxsize=None)
def _dft_constants(r):
    h = r // 2
    idx = np.arange(r, dtype=np.float64)
    ang = 2.0 * np.pi * np.outer(idx, idx) / r
    c, s = np.cos(ang), np.sin(ang)
    f1 = np.block([[c[:, :h], s[:, :h]], [-s[:, :h], c[:, :h]]])
    f1_real = np.concatenate([c, -s], axis=0)
    f2 = np.block([[c, s], [-s, c]])
    f2_inv = np.block([[c, -s], [s, c]])
    f1_inv = np.block([[c[:h, :], -s[:h, :]], [s[:h, :], c[:h, :]]])
    n = r * r
    coarse = np.arange(r // TWIDDLE_RADIX, dtype=np.float64)[:, None] * TWIDDLE_RADIX
    fine = np.arange(TWIDDLE_RADIX, dtype=np.float64)[:, None]

    def table(mult):
        a = 2.0 * np.pi * mult * idx[None, :] / n
        t = np.stack([np.cos(a), np.sin(a)], axis=1)
        return np.ascontiguousarray(np.broadcast_to(t[..., None], t.shape + (LANES,))).astype(np.float32)

    return dict(f1=_split_hi_lo_np(f1), f1_real=_split_hi_lo_np(f1_real), f2=_split_hi_lo_np(f2),
                f2_inv=_split_hi_lo_np(f2_inv), f1_inv=_split_hi_lo_np(f1_inv),
                t1=table(coarse), t2=table(fine))


def _split_hi_lo(data):
    d_hi = data.astype(BF16)
    return d_hi, (data - d_hi.astype(F32)).astype(BF16)


def _dot3(m_ref, d_hi, d_lo):
    m_hi, m_lo = m_ref[0], m_ref[1]
    return (jnp.dot(m_hi, d_hi, preferred_element_type=F32)
            + jnp.dot(m_hi, d_lo, preferred_element_type=F32)
            + jnp.dot(m_lo, d_hi, preferred_element_type=F32))


def _staged_loop(n_items, stages):
    def body(it, carry):
        state = [it * FFT_BLOCKS + j for j in range(FFT_BLOCKS)]
        for stage in stages:
            state = [stage(s) for s in state]
        return carry

    lax.fori_loop(0, n_items // FFT_BLOCKS, body, 0)


def _twiddle(t1_ref, t2_ref, idx):
    a = idx // TWIDDLE_RADIX
    b = idx % TWIDDLE_RADIX
    c1, s1 = t1_ref[a, 0], t1_ref[a, 1]
    c2, s2 = t2_ref[b, 0], t2_ref[b, 1]
    return c1 * c2 - s1 * s2, s1 * c2 + c1 * s2


def _lanes(x, half):
    return x[:, half * LANES:(half + 1) * LANES]


def _slab_rows(r, chunk):
    return r * (chunk + PITCH_PAD)


def _blocks_to_rows(src_ref, dst_ref, r):
    chunk = src_ref.shape[1]
    pitch = chunk + PITCH_PAD

    def body(g, carry):
        dst_ref[pl.ds(pl.multiple_of(g * pitch, 8), chunk), :] = src_ref[g]
        return carry

    lax.fori_loop(0, r, body, 0, unroll=COPY_UNROLL)


def _rows_to_blocks(src_ref, dst_ref, r):
    chunk = dst_ref.shape[1]
    pitch = chunk + PITCH_PAD

    def body(g, carry):
        dst_ref[g] = src_ref[pl.ds(pl.multiple_of(g * pitch, 8), chunk), :]
        return carry

    lax.fori_loop(0, r, body, 0, unroll=COPY_UNROLL)


def _forward_n1_store(out, first_n2, n2l, t1_ref, t2_ref, sre_ref, sim_ref, r):
    pitch = r + PITCH_PAD
    for half in range(2):
        re, im = _lanes(out[:r], half), _lanes(out[r:], half)
        n2 = first_n2 + n2l + half
        c, s = _twiddle(t1_ref, t2_ref, n2)
        rows = pl.ds(n2, r, stride=pitch)
        sre_ref[rows, :] = re * c + im * s
        sim_ref[rows, :] = im * c - re * s


def _stage_a(x_ref, x2d, chunk, first_n2, f1_ref, t1_ref, t2_ref, sre_ref, sim_ref, r):
    _blocks_to_rows(x_ref, x2d, r)

    def gather(jj):
        n2l = 2 * jj
        rhs = jnp.concatenate([x2d[pl.ds(n2l, r, stride=chunk + PITCH_PAD), :],
                               x2d[pl.ds(n2l + 1, r, stride=chunk + PITCH_PAD), :]], axis=1)
        return n2l, _split_hi_lo(rhs)

    def transform(state):
        n2l, (d_hi, d_lo) = state
        return n2l, _dot3(f1_ref, d_hi, d_lo)

    def twiddle_store(state):
        n2l, out = state
        _forward_n1_store(out, first_n2, n2l, t1_ref, t2_ref, sre_ref, sim_ref, r)

    _staged_loop(chunk // 2, [gather, transform, twiddle_store])


def _load_k1_pair(sre_ref, sim_ref, k1, r):
    pitch = r + PITCH_PAD
    r0 = pl.multiple_of(k1 * pitch, 8)
    r1 = pl.multiple_of((k1 + 1) * pitch, 8)
    rows = (pl.ds(r0, r), pl.ds(r1, r))
    rhs = jnp.concatenate([
        jnp.concatenate([sre_ref[rows[0], :], sre_ref[rows[1], :]], axis=1),
        jnp.concatenate([sim_ref[rows[0], :], sim_ref[rows[1], :]], axis=1)], axis=0)
    return rhs, rows


def _load_n2_pair(sre_ref, sim_ref, n2, r):
    pitch = r + PITCH_PAD
    rows0 = pl.ds(n2, r, stride=pitch)
    rows1 = pl.ds(n2 + 1, r, stride=pitch)
    return jnp.concatenate([
        jnp.concatenate([sre_ref[rows0, :], sre_ref[rows1, :]], axis=1),
        jnp.concatenate([sim_ref[rows0, :], sim_ref[rows1, :]], axis=1)], axis=0)


def _stage_bc(first_k1, chunk, cs_ref, f2_ref, f2i_ref, t1_ref, t2_ref, sre_ref, sim_ref, r):
    def load(jj):
        k1l = 2 * jj
        rhs, rows = _load_k1_pair(sre_ref, sim_ref, first_k1 + k1l, r)
        return k1l, rows, _split_hi_lo(rhs)

    def forward(state):
        k1l, rows, (d_hi, d_lo) = state
        return k1l, rows, _dot3(f2_ref, d_hi, d_lo)

    def filter_multiply(state):
        k1l, rows, x = state
        p_re, p_im = [], []
        for half in range(2):
            xr, xi = _lanes(x[:r], half), _lanes(x[r:], half)
            crow = pl.ds(pl.multiple_of((k1l + half) * r, 8), r)
            cr, ci = cs_ref[0, crow, :], cs_ref[1, crow, :]
            p_re.append(xr * cr - xi * ci)
            p_im.append(xr * ci + xi * cr)
        prod = jnp.concatenate([jnp.concatenate(p_re, axis=1), jnp.concatenate(p_im, axis=1)], axis=0)
        return k1l, rows, _split_hi_lo(prod)

    def inverse(state):
        k1l, rows, (d_hi, d_lo) = state
        return k1l, rows, _dot3(f2i_ref, d_hi, d_lo)

    def twiddle_store(state):
        k1l, rows, y = state
        for half in range(2):
            re, im = _lanes(y[:r], half), _lanes(y[r:], half)
            c, s = _twiddle(t1_ref, t2_ref, first_k1 + k1l + half)
            sre_ref[rows[half], :] = re * c - im * s
            sim_ref[rows[half], :] = im * c + re * s

    _staged_loop(chunk // 2, [load, forward, filter_multiply, inverse, twiddle_store])


def _hyena_conv_kernel(x_ref, gate_ref, cs_ref, f1_ref, f1i_ref, f2_ref, f2i_ref, t1_ref, t2_ref,
                       o_ref, sre_ref, sim_ref, x2d, g2d, *, r, ca, cb):
    n_a, n_bc = r // ca, r // cb
    p_bc1, p_da, p_bc2, p_d2 = n_a, n_a + n_bc, 2 * n_a + n_bc, 2 * n_a + 2 * n_bc
    step = pl.program_id(2)
    slab_stride = ca + PITCH_PAD

    def inverse_n1_stages(first_n2):
        def gather(jj):
            n2l = 2 * jj
            return n2l, _split_hi_lo(_load_n2_pair(sre_ref, sim_ref, first_n2 + n2l, r))

        def inverse_n1(state):
            n2l, (d_hi, d_lo) = state
            return n2l, _dot3(f1i_ref, d_hi, d_lo)

        return [gather, inverse_n1]

    @pl.when(step < p_bc1)
    def _():
        _stage_a(x_ref, x2d, ca, step * ca, f1_ref, t1_ref, t2_ref, sre_ref, sim_ref, r)

    @pl.when((step >= p_bc1) & (step < p_da))
    def _():
        _stage_bc((step - p_bc1) * cb, cb, cs_ref, f2_ref, f2i_ref, t1_ref, t2_ref, sre_ref, sim_ref, r)

    @pl.when((step >= p_da) & (step < p_bc2))
    def _():
        first_n2 = (step - p_da) * ca
        _blocks_to_rows(gate_ref, g2d, r)

        def gate(state):
            n2l, y = state
            z = jnp.concatenate([g2d[pl.ds(n2l + half, r, stride=slab_stride), :] * _lanes(y, half)
                                 for half in range(2)], axis=1)
            return n2l, _split_hi_lo(z)

        def forward_n1(state):
            n2l, (d_hi, d_lo) = state
            return n2l, _dot3(f1_ref, d_hi, d_lo)

        def twiddle_store(state):
            n2l, out = state
            _forward_n1_store(out, first_n2, n2l, t1_ref, t2_ref, sre_ref, sim_ref, r)

        _staged_loop(ca // 2, inverse_n1_stages(first_n2) + [gate, forward_n1, twiddle_store])

    @pl.when((step >= p_bc2) & (step < p_d2))
    def _():
        _stage_bc((step - p_bc2) * cb, cb, cs_ref, f2_ref, f2i_ref, t1_ref, t2_ref, sre_ref, sim_ref, r)

    @pl.when(step >= p_d2)
    def _():
        first_n2 = (step - p_d2) * ca
        _blocks_to_rows(gate_ref, g2d, r)

        def gate_store(state):
            n2l, y = state
            for half in range(2):
                tok = pl.ds(n2l + half, r, stride=slab_stride)
                x2d[tok, :] = g2d[tok, :] * _lanes(y, half)

        _staged_loop(ca // 2, inverse_n1_stages(first_n2) + [gate_store])
        _rows_to_blocks(x2d, o_ref, r)


def _hyena_conv_kernel_aliased(*refs, **kw):
    _hyena_conv_kernel(*refs[1:], **kw)


def _hyena_long_conv(u, row0, spectrum, r, n_pairs, out_buf=None, *, ca, cb=16):
    n_a, n_bc = r // ca, r // cb
    p_bc1, p_da, p_bc2, p_d2 = n_a, n_a + n_bc, 2 * n_a + n_bc, 2 * n_a + 2 * n_bc
    n_slabs = u.shape[0] // 3
    consts = _dft_constants(r)
    t = u.shape[1]
    uv = u.reshape(u.shape[0], t // r, r, LANES)
    blk0 = row0 // (r * r)

    def gate_index(s, p, st):
        second = st >= p_bc2
        chunk = jnp.where(second, st - p_d2, st - p_da)
        return (jnp.where(second, 2 * n_slabs, n_slabs) + s, blk0 + p, jnp.clip(chunk, 0, n_a - 1), 0)

    def spectrum_index(s, p, st):
        second = st >= p_bc2
        chunk = jnp.where(second, st - p_bc2, st - p_bc1)
        return (jnp.where(second, n_slabs, 0) + s, 0, jnp.clip(chunk, 0, n_bc - 1), 0)

    def full(a):
        return pl.BlockSpec(a.shape, lambda s, p, st, nd=a.ndim: (0,) * nd)

    const_args = [consts["f1"], consts["f1_inv"], consts["f2"], consts["f2_inv"], consts["t1"], consts["t2"]]
    in_specs = [
        pl.BlockSpec((None, r, ca, LANES), lambda s, p, st: (s, blk0 + p, jnp.minimum(st, n_a - 1), 0)),
        pl.BlockSpec((None, r, ca, LANES), gate_index),
        pl.BlockSpec((None, 2, cb * r, LANES), spectrum_index),
    ] + [full(a) for a in const_args]
    args = [uv, uv, spectrum, *const_args]
    body = _hyena_conv_kernel
    aliases = {}
    if out_buf is not None:
        in_specs = [pl.BlockSpec(memory_space=pl.ANY)] + in_specs
        args = [out_buf.reshape(n_slabs, t // r, r, LANES)] + args
        body = _hyena_conv_kernel_aliased
        aliases = {0: 0}
    out = pl.pallas_call(
        functools.partial(body, r=r, ca=ca, cb=cb),
        grid=(n_slabs, n_pairs, 3 * n_a + 2 * n_bc),
        in_specs=in_specs,
        out_specs=pl.BlockSpec((None, r, ca, LANES),
                               lambda s, p, st: (s, blk0 + p, jnp.clip(st - p_d2, 0, n_a - 1), 0)),
        out_shape=jax.ShapeDtypeStruct((n_slabs, t // r, r, LANES), F32),
        scratch_shapes=[pltpu.VMEM((r * (r + PITCH_PAD), LANES), F32),
                        pltpu.VMEM((r * (r + PITCH_PAD), LANES), F32),
                        pltpu.VMEM((_slab_rows(r, ca), LANES), F32),
                        pltpu.VMEM((_slab_rows(r, ca), LANES), F32)],
        input_output_aliases=aliases,
        compiler_params=pltpu.CompilerParams(
            dimension_semantics=("parallel", "arbitrary", "arbitrary"),
            vmem_limit_bytes=VMEM_LIMIT_BYTES),
        name="hyena_long_conv",
    )(*args)
    return out.reshape(n_slabs, t, LANES)


def _fft_spectrum_kernel(c_ref, scale_ref, shift_ref, f1_ref, f2_ref, t1_ref, t2_ref, o_ref, sre_ref, sim_ref, c2d,
                         *, r, ca, cb):
    n_a = r // ca
    step = pl.program_id(2)

    @pl.when(step < n_a)
    def _():
        _stage_a(c_ref, c2d, ca, step * ca, f1_ref, t1_ref, t2_ref, sre_ref, sim_ref, r)

    @pl.when(step >= n_a)
    def _():
        first_k1 = (step - n_a) * cb

        def load(jj):
            k1l = 2 * jj
            rhs, _ = _load_k1_pair(sre_ref, sim_ref, first_k1 + k1l, r)
            return k1l, _split_hi_lo(rhs)

        def forward(state):
            k1l, (d_hi, d_lo) = state
            return k1l, _dot3(f2_ref, d_hi, d_lo)

        def store(state):
            k1l, x = state
            for half in range(2):
                orow = pl.ds(pl.multiple_of((k1l + half) * r, 8), r)
                o_ref[0, orow, :] = _lanes(x[:r], half) * scale_ref[...] + shift_ref[...]
                o_ref[1, orow, :] = _lanes(x[r:], half) * scale_ref[...]

        _staged_loop(cb // 2, [load, forward, store])


def _fft_spectrum(c, scale, shift, r, *, ca=32, cb=16):
    n_slabs, n, _ = c.shape
    width = n_slabs * LANES
    n_a, n_bc = r // ca, r // cb
    consts = _dft_constants(r)
    cv = c.reshape(n_slabs, r, r, LANES)

    def full(a):
        return pl.BlockSpec(a.shape, lambda s, o, st, nd=a.ndim: (0,) * nd)

    const_args = [consts["f1_real"], consts["f2"], consts["t1"], consts["t2"]]
    return pl.pallas_call(
        functools.partial(_fft_spectrum_kernel, r=r, ca=ca, cb=cb),
        grid=(n_slabs, 1, n_a + n_bc),
        in_specs=[pl.BlockSpec((None, r, ca, LANES), lambda s, o, st: (s, 0, jnp.minimum(st, n_a - 1), 0)),
                  pl.BlockSpec((1, LANES), lambda s, o, st: (0, s)),
                  pl.BlockSpec((1, LANES), lambda s, o, st: (0, s))]
        + [full(a) for a in const_args],
        out_specs=pl.BlockSpec((None, 2, cb * r, LANES), lambda s, o, st: (s, 0, jnp.maximum(st - n_a, 0), 0)),
        out_shape=jax.ShapeDtypeStruct((n_slabs, 2, n, LANES), F32),
        scratch_shapes=[pltpu.VMEM((r * (r + PITCH_PAD), LANES), F32),
                        pltpu.VMEM((r * (r + PITCH_PAD), LANES), F32),
                        pltpu.VMEM((_slab_rows(r, ca), LANES), F32)],
        compiler_params=pltpu.CompilerParams(
            dimension_semantics=("parallel", "arbitrary", "arbitrary"),
            vmem_limit_bytes=VMEM_LIMIT_BYTES),
        name="fft_spectrum",
    )(cv, scale.reshape(1, width), shift.reshape(1, width), *const_args)


FILTER_EMB = 2 * FILTER_BANDS + 1


def _dot_f32(a, b):
    a_hi, a_lo = _split_hi_lo(a)
    b_hi, b_lo = _split_hi_lo(b)
    return (jnp.dot(a_hi, b_hi, preferred_element_type=F32)
            + jnp.dot(a_hi, b_lo, preferred_element_type=F32)
            + jnp.dot(a_lo, b_hi, preferred_element_type=F32))


def _filter_kernel(bands_ref, w1_ref, b1_ref, w2_ref, b2_ref, freq_ref, w3f_ref, w3b_ref, delta_ref,
                   c_ref, norm_ref, hid_ref, *, length, tr):
    j = pl.program_id(0)
    i = pl.program_id(1)
    row = i * tr + lax.broadcasted_iota(jnp.int32, (tr, 1), 0)
    pos_idx = jnp.where(row < length, row, 2 * length - row)
    mf = pos_idx.astype(F32)
    t = mf / (length - 1)
    rows = pl.ds(pl.multiple_of(i * tr, 8), tr)

    @pl.when(j == 0)
    def _():
        w = 2.0 * math.pi * mf / length
        arg = w * bands_ref[...]
        lane = lax.broadcasted_iota(jnp.int32, (1, LANES), 1)
        pos = jnp.where(lane == 0, t,
                        jnp.where(lane <= FILTER_BANDS, jnp.cos(arg),
                                  jnp.where(lane < FILTER_EMB, -jnp.sin(arg), 0.0)))
        h = jnp.sin(freq_ref[...] * (_dot_f32(pos, w1_ref[...]) + b1_ref[...]))
        h = jnp.sin(freq_ref[...] * (_dot_f32(h, w2_ref[...]) + b2_ref[...]))
        hid_ref[rows, :] = h

    h = hid_ref[rows, :]
    decay = jnp.exp(-t * delta_ref[...])
    tile_is_fwd = (i + 1) * tr <= length
    main = _dot_f32(h, jnp.where(tile_is_fwd, w3f_ref[...], w3b_ref[...])) * decay
    main = jnp.where(row == length, 0.0, main)
    extra = jnp.where(row[0:8] == 0, _dot_f32(h[0:8], w3b_ref[...]) * decay[0:8], 0.0)
    for s in range(c_ref.shape[0]):
        slab = slice(s * LANES, (s + 1) * LANES)
        c_ref[s] = main[:, slab]
        c_ref[s, 0:8, :] = main[0:8, slab] + extra[:, slab]
    part = jnp.abs(main[0:8]) + jnp.abs(extra)
    for k in range(1, tr // 8):
        part = part + jnp.abs(main[8 * k:8 * (k + 1)])

    @pl.when(i == 0)
    def _():
        norm_ref[...] = part

    @pl.when(i > 0)
    def _():
        norm_ref[...] += part


def _hyena_filter_kernel(length, f_w1, f_b1, f_w2, f_b2, f_freq, f_w3, *, tr=512, tc=1024):
    hidden = f_w1.shape[1]
    dm = f_w3.shape[1] // (2 * HYENA_ORDER)
    width = HYENA_ORDER * dm
    tr, tc = min(tr, length), min(tc, width)
    assert length % tr == 0 and width % tc == 0
    bands = jnp.linspace(1e-4, FILTER_BANDS - 1, FILTER_BANDS, dtype=F32)
    bands_row = jnp.zeros((1, LANES), F32).at[0, 1:FILTER_BANDS + 1].set(bands)
    bands_row = bands_row.at[0, FILTER_BANDS + 1:FILTER_EMB].set(bands)

    def pad_to(a, rows, cols):
        return jnp.zeros((rows, cols), F32).at[:a.shape[0], :a.shape[1]].set(a)

    w3 = f_w3.reshape(hidden, HYENA_ORDER, 2, dm)
    w3f = pad_to(w3[:, :, 0].reshape(hidden, width), LANES, width)
    w3b = pad_to(w3[:, :, 1].reshape(hidden, width), LANES, width)
    deltas = jnp.abs(jnp.linspace(MIN_DECAY, MAX_DECAY, dm, dtype=F32))
    delta_row = jnp.tile(deltas, HYENA_ORDER).reshape(1, width)
    small = [bands_row, pad_to(f_w1, LANES, LANES), pad_to(f_b1[None], 1, LANES), pad_to(f_w2, LANES, LANES),
             pad_to(f_b2[None], 1, LANES), pad_to(f_freq[None], 1, LANES)]
    c, norm = pl.pallas_call(
        functools.partial(_filter_kernel, length=length, tr=tr),
        grid=(width // tc, 2 * length // tr),
        in_specs=[pl.BlockSpec(a.shape, lambda j, i: (0, 0)) for a in small] + [
            pl.BlockSpec((LANES, tc), lambda j, i: (0, j)),
            pl.BlockSpec((LANES, tc), lambda j, i: (0, j)),
            pl.BlockSpec((1, tc), lambda j, i: (0, j)),
        ],
        out_specs=[pl.BlockSpec((tc // LANES, tr, LANES), lambda j, i: (j, i, 0)),
                   pl.BlockSpec((8, tc), lambda j, i: (0, j))],
        out_shape=[jax.ShapeDtypeStruct((width // LANES, 2 * length, LANES), F32),
                   jax.ShapeDtypeStruct((8, width), F32)],
        scratch_shapes=[pltpu.VMEM((2 * length, LANES), F32)],
        compiler_params=pltpu.CompilerParams(
            dimension_semantics=("arbitrary", "arbitrary"),
            vmem_limit_bytes=VMEM_LIMIT_BYTES),
        name="hyena_filter",
    )(*small, w3f, w3b, delta_row)
    return c, jnp.sum(norm, axis=0)


def _filter_spectra(length, r, filt_params, skip):
    c, norm = _hyena_filter_kernel(length, *filt_params)
    n = 2 * length
    return _fft_spectrum(c, 1.0 / (n * norm), skip.reshape(-1) / n, r)


def _hyena_mixer(u, filt_params, skip, seq_tokens, batches):
    n_prompt, s_prompt, s_sample = seq_tokens
    z = None
    for seq_len, n_batch, row0 in ((s_prompt, batches[0], 0), (s_sample, batches[1], n_prompt)):
        r = math.isqrt(2 * seq_len)
        assert r * r == 2 * seq_len and n_batch % 2 == 0 and row0 % (r * r) == 0
        spec = _filter_spectra(seq_len, r, filt_params, skip)
        chunk = 16 if r > 64 else 32
        z = _hyena_long_conv(u, row0, spec, r, n_batch // 2, z, ca=chunk)
    return z


def kernel(x_prompt, x_sample, ln_g, ln_b, ffn_w_in, ffn_w_out, attn_w_qkv, attn_w_o, hy_w_in, hy_b_in,
           hy_conv_w, hy_conv_b, hy_f_w1, hy_f_b1, hy_f_w2, hy_f_b2, hy_f_freq, hy_f_w3, hy_skip,
           hy_w_out, hy_b_out):
    bp, sp, dm = x_prompt.shape
    bs, ss, _ = x_sample.shape
    n_prompt = bp * sp
    n_sample = bs * ss
    seq_tokens = (n_prompt, sp, ss)
    n_tokens = n_prompt + n_sample
    slopes = jnp.exp2(-8.0 * jnp.arange(1, N_SLOTS + 1, dtype=F32) / N_SLOTS)
    zero_bias_dm = jnp.zeros((dm,), F32)
    ffn_w_in, ffn_w_out, attn_w_qkv, attn_w_o, hy_w_in, hy_w_out = (
        w.astype(BF16) for w in (ffn_w_in, ffn_w_out, attn_w_qkv, attn_w_o, hy_w_in, hy_w_out))

    x = None
    for i in range(DEPTH):
        ffn = ((ffn_w_in, (i, 0)), (ffn_w_out, (i, 0)), ln_g[i, 0], ln_b[i, 0])
        if i == 0:
            x = _ffn_layer(x_prompt.reshape(n_prompt, dm), *ffn, out_rows=n_tokens)
            x = _ffn_layer(x_sample.reshape(n_sample, dm), *ffn, out_rows=n_tokens, out_row0=n_prompt, out_buf=x)
        else:
            x = _ffn_layer(x, *ffn)
        j = i // 2
        if i % 2 == 0:
            qkv = _projection(x, (attn_w_qkv, (j,)), BF16)
            att = _dilated_attention(qkv, slopes, seq_tokens)
            x = _projection_ln(x, att, (attn_w_o, (j,)), zero_bias_dm, ln_g[i, 1], ln_b[i, 1])
        else:
            u = _projection_short_conv(x, (hy_w_in, (j,)), hy_b_in[j], hy_conv_w[j], hy_conv_b[j], seq_tokens)
            filt = (hy_f_w1[j], hy_f_b1[j], hy_f_w2[j], hy_f_b2[j], hy_f_freq[j], hy_f_w3[j])
            z = _hyena_mixer(u, filt, hy_skip[j], seq_tokens, (bp, bs))
            x = _projection_ln(x, z, (hy_w_out, (j,)), hy_b_out[j], ln_g[i, 1], ln_b[i, 1])
        ffn = ((ffn_w_in, (i, 1)), (ffn_w_out, (i, 1)), ln_g[i, 2], ln_b[i, 2])
        if i < DEPTH - 1:
            x = _ffn_layer(x, *ffn)

    y_prompt = _ffn_layer(x, *ffn, n_rows=n_prompt)
    y_sample = _ffn_layer(x, *ffn, x_row0=n_prompt, n_rows=n_sample)
    return (y_prompt.reshape(bp, sp, dm), y_sample.reshape(bs, ss, dm))
```

```python
import functools
import math

import jax
import jax.numpy as jnp
import ml_dtypes
import numpy as np
from jax import lax
from jax.experimental import pallas as pl
from jax.experimental.pallas import tpu as pltpu

D_MODEL = 2048
DEPTH = 4
HEAD_DIM = 128
N_SLOTS = D_MODEL // HEAD_DIM
DILATED_GROUPS = ((128, 1), (512, 4), (2048, 16))
N_GROUPS = len(DILATED_GROUPS)
ATTN_WIDTH = N_SLOTS * HEAD_DIM
QKV_WIDTH = 3 * N_GROUPS * ATTN_WIDTH
HALF_WINDOW = 64
NEG_INF = -1e30
D_FF = 5632
HYENA_ORDER = 2
SHORT_CONV = 3
FILTER_BANDS = 16
DECAY_TARGET = 1e-2
DECAY_FAST = 0.3
DECAY_SLOW = 1.5
MAX_DECAY = math.log(DECAY_TARGET) / DECAY_FAST
MIN_DECAY = math.log(DECAY_TARGET) / DECAY_SLOW
DN_ALPHA = (2.0 * DEPTH) ** 0.25
LN_EPS = 1e-5

LANES = 128
FFN_TOKEN_TILE = 512
FFN_HIDDEN_TILE = 512
PROJ_TOKEN_TILE = 1024
PROJ_COLUMN_TILE = 1024
OUT_PROJ_TOKEN_TILE = 512
ATTN_QUERY_TILE = 1024
ATTN_HEADS_PER_STEP = 4
ATTN_BLOCKS = 8
FFT_BLOCKS = 8
COPY_UNROLL = 8
VMEM_LIMIT_BYTES = 48 * 1024 * 1024
BF16 = jnp.bfloat16
F32 = jnp.float32


def _layer_norm_rows(y, g, b):
    mu = jnp.mean(y, axis=-1, keepdims=True)
    c = y - mu
    var = jnp.mean(c * c, axis=-1, keepdims=True)
    return c * lax.rsqrt(var + LN_EPS) * g + b


def _stacked_spec(w, block, index_map):
    _, lead = w
    return pl.BlockSpec((None,) * len(lead) + block, lambda *grid: tuple(lead) + index_map(*grid))


def _seq_position(t0, seq_tokens):
    n_prompt, s_prompt, s_sample = seq_tokens
    seq_len = jnp.where(t0 < n_prompt, s_prompt, s_sample)
    return lax.rem(t0, seq_len), seq_len


def _ffn_kernel(x_ref, wg_ref, wu_ref, wo_ref, g_ref, b_ref, o_ref, xb_ref, *, nk):
    k = pl.program_id(1)

    @pl.when(k == 0)
    def _():
        xb_ref[...] = x_ref[...].astype(BF16)
        o_ref[...] = jnp.zeros_like(o_ref)

    xb = xb_ref[...]
    gate = jnp.dot(xb, wg_ref[...], preferred_element_type=F32)
    up = jnp.dot(xb, wu_ref[...], preferred_element_type=F32)
    act = (gate * jax.nn.sigmoid(gate) * up).astype(BF16)
    o_ref[...] += jnp.dot(act, wo_ref[...], preferred_element_type=F32)

    @pl.when(k == nk - 1)
    def _():
        y = DN_ALPHA * x_ref[...] + 0.5 * o_ref[...]
        o_ref[...] = _layer_norm_rows(y, g_ref[...], b_ref[...])


def _ffn_kernel_aliased(buf_ref, *refs, nk):
    _ffn_kernel(*refs, nk=nk)


def _ffn_layer(x, w_in, w_out, g, b, *, x_row0=0, n_rows=None, out_rows=None, out_row0=0, out_buf=None,
               tm=FFN_TOKEN_TILE, tf=FFN_HIDDEN_TILE):
    dm = x.shape[1]
    n_rows = x.shape[0] if n_rows is None else n_rows
    out_rows = n_rows if out_rows is None else out_rows
    nk = w_out[0].shape[-2] // tf
    x_blk0, o_blk0 = x_row0 // tm, out_row0 // tm
    in_specs = [
        pl.BlockSpec((tm, dm), lambda i, k: (x_blk0 + i, 0)),
        _stacked_spec(w_in, (dm, tf), lambda i, k: (0, k)),
        _stacked_spec(w_in, (dm, tf), lambda i, k: (0, k + nk)),
        _stacked_spec(w_out, (tf, dm), lambda i, k: (k, 0)),
        pl.BlockSpec((1, dm), lambda i, k: (0, 0)),
        pl.BlockSpec((1, dm), lambda i, k: (0, 0)),
    ]
    args = [x, w_in[0], w_in[0], w_out[0], g.reshape(1, dm), b.reshape(1, dm)]
    body, aliases = _ffn_kernel, {}
    if out_buf is not None:
        in_specs = [pl.BlockSpec(memory_space=pl.ANY)] + in_specs
        args = [out_buf] + args
        body, aliases = _ffn_kernel_aliased, {0: 0}
    return pl.pallas_call(
        functools.partial(body, nk=nk),
        grid=(n_rows // tm, nk),
        in_specs=in_specs,
        out_specs=pl.BlockSpec((tm, dm), lambda i, k: (o_blk0 + i, 0)),
        out_shape=jax.ShapeDtypeStruct((out_rows, dm), F32),
        scratch_shapes=[pltpu.VMEM((tm, dm), BF16)],
        input_output_aliases=aliases,
        compiler_params=pltpu.CompilerParams(
            dimension_semantics=("parallel", "arbitrary"),
            vmem_limit_bytes=VMEM_LIMIT_BYTES),
        name="ffn_ln",
    )(*args)


def _proj_kernel(x_ref, w_ref, o_ref, xb_ref):
    @pl.when(pl.program_id(1) == 0)
    def _():
        xb_ref[...] = x_ref[...].astype(BF16)

    o_ref[...] = jnp.dot(xb_ref[...], w_ref[...], preferred_element_type=F32).astype(o_ref.dtype)


def _projection(x, w, out_dtype, *, tm=PROJ_TOKEN_TILE, tn=PROJ_COLUMN_TILE):
    t, dm = x.shape
    n = w[0].shape[-1]
    return pl.pallas_call(
        _proj_kernel,
        grid=(t // tm, n // tn),
        in_specs=[
            pl.BlockSpec((tm, dm), lambda i, j: (i, 0)),
            _stacked_spec(w, (dm, tn), lambda i, j: (0, j)),
        ],
        out_specs=pl.BlockSpec((tm, tn), lambda i, j: (i, j)),
        out_shape=jax.ShapeDtypeStruct((t, n), out_dtype),
        scratch_shapes=[pltpu.VMEM((tm, dm), BF16)],
        compiler_params=pltpu.CompilerParams(
            dimension_semantics=("parallel", "arbitrary"),
            vmem_limit_bytes=VMEM_LIMIT_BYTES),
        name="projection",
    )(x, w[0])


HALO_ROWS = 16


def _proj_conv_kernel(xp_ref, x_ref, xn_ref, w_ref, b_ref, cw_ref, cb_ref, o_ref, xb_ref, *, tm, seq_tokens):
    @pl.when(pl.program_id(1) == 0)
    def _():
        xb_ref[0:HALO_ROWS, :] = xp_ref[...].astype(BF16)
        xb_ref[HALO_ROWS:HALO_ROWS + tm, :] = x_ref[...].astype(BF16)
        xb_ref[HALO_ROWS + tm:, :] = xn_ref[...].astype(BF16)

    rows = tm + 2 * HALO_ROWS
    acc = jnp.dot(xb_ref[...], w_ref[...], preferred_element_type=F32) + b_ref[...]
    pos0, seq_len = _seq_position(pl.program_id(0) * tm, seq_tokens)
    tile_row = lax.broadcasted_iota(jnp.int32, (tm, 1), 0)
    at_seq_start = (tile_row == 0) & (pos0 == 0)
    at_seq_end = (tile_row == tm - 1) & (pos0 + tm == seq_len)
    cur = acc[HALO_ROWS:HALO_ROWS + tm]
    prev = pltpu.roll(acc, 1, axis=0)[HALO_ROWS:HALO_ROWS + tm]
    nxt = pltpu.roll(acc, rows - 1, axis=0)[HALO_ROWS:HALO_ROWS + tm]
    prev = jnp.where(at_seq_start, 0.0, prev)
    nxt = jnp.where(at_seq_end, 0.0, nxt)
    out = prev * cw_ref[0:1, :] + cur * cw_ref[1:2, :] + nxt * cw_ref[2:3, :] + cb_ref[...]
    for s in range(o_ref.shape[0]):
        o_ref[s] = out[:, s * LANES:(s + 1) * LANES]


def _projection_short_conv(x, w, bias, conv_w, conv_b, seq_tokens, *, tm=PROJ_TOKEN_TILE, tn=PROJ_COLUMN_TILE):
    t, dm = x.shape
    n = w[0].shape[-1]
    ratio = tm // HALO_ROWS
    n_halo_blocks = t // HALO_ROWS
    return pl.pallas_call(
        functools.partial(_proj_conv_kernel, tm=tm, seq_tokens=seq_tokens),
        grid=(t // tm, n // tn),
        in_specs=[
            pl.BlockSpec((HALO_ROWS, dm), lambda i, j: (jnp.maximum(i * ratio - 1, 0), 0)),
            pl.BlockSpec((tm, dm), lambda i, j: (i, 0)),
            pl.BlockSpec((HALO_ROWS, dm), lambda i, j: (jnp.minimum((i + 1) * ratio, n_halo_blocks - 1), 0)),
            _stacked_spec(w, (dm, tn), lambda i, j: (0, j)),
            pl.BlockSpec((1, tn), lambda i, j: (0, j)),
            pl.BlockSpec((SHORT_CONV, tn), lambda i, j: (0, j)),
            pl.BlockSpec((1, tn), lambda i, j: (0, j)),
        ],
        out_specs=pl.BlockSpec((tn // LANES, tm, LANES), lambda i, j: (j, i, 0)),
        out_shape=jax.ShapeDtypeStruct((n // LANES, t, LANES), F32),
        scratch_shapes=[pltpu.VMEM((tm + 2 * HALO_ROWS, dm), BF16)],
        compiler_params=pltpu.CompilerParams(
            dimension_semantics=("parallel", "arbitrary"),
            vmem_limit_bytes=VMEM_LIMIT_BYTES),
        name="projection_short_conv",
    )(x, x, x, w[0], bias.reshape(1, n), conv_w, conv_b.reshape(1, n))


def _proj_ln_kernel(x_ref, z_ref, w_ref, bias_ref, g_ref, b_ref, o_ref):
    if len(z_ref.shape) == 3:
        z = jnp.concatenate([z_ref[s].astype(BF16) for s in range(z_ref.shape[0])], axis=1)
    else:
        z = z_ref[...].astype(BF16)
    mix = jnp.dot(z, w_ref[...], preferred_element_type=F32)
    y = DN_ALPHA * x_ref[...] + (mix + bias_ref[...])
    o_ref[...] = _layer_norm_rows(y, g_ref[...], b_ref[...])


def _projection_ln(x, z, w, bias, g, b, *, tm=OUT_PROJ_TOKEN_TILE):
    t, dm = x.shape
    if z.ndim == 3:
        kdim = z.shape[0] * LANES
        z_spec = pl.BlockSpec((z.shape[0], tm, LANES), lambda i: (0, i, 0))
    else:
        kdim = z.shape[1]
        z_spec = pl.BlockSpec((tm, kdim), lambda i: (i, 0))
    return pl.pallas_call(
        _proj_ln_kernel,
        grid=(t // tm,),
        in_specs=[
            pl.BlockSpec((tm, dm), lambda i: (i, 0)),
            z_spec,
            _stacked_spec(w, (kdim, dm), lambda i: (0, 0)),
            pl.BlockSpec((1, dm), lambda i: (0, 0)),
            pl.BlockSpec((1, dm), lambda i: (0, 0)),
            pl.BlockSpec((1, dm), lambda i: (0, 0)),
        ],
        out_specs=pl.BlockSpec((tm, dm), lambda i: (i, 0)),
        out_shape=jax.ShapeDtypeStruct((t, dm), F32),
        compiler_params=pltpu.CompilerParams(
            dimension_semantics=("parallel",),
            vmem_limit_bytes=VMEM_LIMIT_BYTES),
        name="projection_ln",
    )(x, z, w[0], bias.reshape(1, dm), g.reshape(1, dm), b.reshape(1, dm))


def _attn_kernel(slopes_ref, *refs, tq, hg, seq_tokens):
    n_in = 7 * N_GROUPS
    in_refs = refs[:n_in]
    o_ref = refs[n_in]
    qf_ref, kf_ref, vf_ref, og_ref, lg_ref = refs[n_in + 1:]

    pos0, seq_len = _seq_position(pl.program_id(0) * tq, seq_tokens)
    scale = HEAD_DIM ** -0.5

    for hh in range(hg):
        cols = slice(hh * HEAD_DIM, (hh + 1) * HEAD_DIM)
        slope = slopes_ref[pl.program_id(1) * hg + hh]
        for g, (_, d) in enumerate(DILATED_GROUPS):
            q_ref, kp_ref, kc_ref, kn_ref, vp_ref, vc_ref, vn_ref = in_refs[7 * g:7 * g + 7]
            halo = HALF_WINDOW * d
            nq = tq // d
            sb = min(nq, 128)
            nkeys = sb + 2 * HALF_WINDOW
            chain_len = seq_len // d
            cpos0 = pos0 // d

            qf_ref[0:tq, :] = q_ref[:, cols].astype(F32)
            kf_ref[0:halo, :] = kp_ref[:, cols].astype(F32)
            kf_ref[halo:halo + tq, :] = kc_ref[:, cols].astype(F32)
            kf_ref[halo + tq:2 * halo + tq, :] = kn_ref[:, cols].astype(F32)
            vf_ref[0:halo, :] = vp_ref[:, cols].astype(F32)
            vf_ref[halo:halo + tq, :] = vc_ref[:, cols].astype(F32)
            vf_ref[halo + tq:2 * halo + tq, :] = vn_ref[:, cols].astype(F32)

            qi = lax.broadcasted_iota(jnp.int32, (sb, nkeys), 0)
            ki = lax.broadcasted_iota(jnp.int32, (sb, nkeys), 1)
            rel = ki - HALF_WINDOW - qi
            band_bias = jnp.where(jnp.abs(rel) <= HALF_WINDOW,
                                  (-slope) * (jnp.abs(rel) * d).astype(F32), NEG_INF)
            key_col = lax.broadcasted_iota(jnp.int32, (1, nkeys), 1)

            n_blocks = d * (nq // sb)
            per_iter = min(ATTN_BLOCKS, n_blocks)

            def chain_blocks(it, carry, d=d, sb=sb, nkeys=nkeys, nq=nq, g=g, per_iter=per_iter,
                             band_bias=band_bias, key_col=key_col,
                             chain_len=chain_len, cpos0=cpos0):
                rows, scores = [], []
                for j in range(per_iter):
                    idx = it * per_iter + j
                    r = idx // (nq // sb)
                    s = idx % (nq // sb)
                    start = r + s * (sb * d)
                    if d == 1:
                        q_rows, k_rows = pl.ds(start, sb), pl.ds(start, nkeys)
                    else:
                        q_rows, k_rows = pl.ds(start, sb, stride=d), pl.ds(start, nkeys, stride=d)
                    qc = qf_ref[q_rows, :].astype(BF16)
                    kc = kf_ref[k_rows, :].astype(BF16)
                    sc = lax.dot_general(qc, kc, (((1,), (1,)), ((), ())), preferred_element_type=F32)
                    rows.append((q_rows, k_rows, s))
                    scores.append(sc)
                probs = []
                for (q_rows, k_rows, s), sc in zip(rows, scores):
                    kpos0 = cpos0 + s * sb - HALF_WINDOW
                    in_seq = (key_col >= -kpos0) & (key_col < chain_len - kpos0)
                    sc = sc * scale + band_bias + jnp.where(in_seq, 0.0, NEG_INF)
                    mx = jnp.max(sc, axis=-1, keepdims=True)
                    p = jnp.exp(sc - mx)
                    den = jnp.sum(p, axis=-1, keepdims=True)
                    probs.append((p.astype(BF16), mx, den))
                outs = [jnp.dot(p, vf_ref[k_rows, :].astype(BF16), preferred_element_type=F32)
                        for (p, _, _), (_, k_rows, _) in zip(probs, rows)]
                for (q_rows, _, _), (_, mx, den), o in zip(rows, probs, outs):
                    og_ref[g, q_rows, :] = o / den
                    lg_ref[g, q_rows, :] = jnp.broadcast_to(mx + jnp.log(den), (sb, HEAD_DIM))
                return carry

            lax.fori_loop(0, n_blocks // per_iter, chain_blocks, 0)

        l0, l1, l2 = lg_ref[0], lg_ref[1], lg_ref[2]
        lmax = jnp.maximum(jnp.maximum(l0, l1), l2)
        w0, w1, w2 = jnp.exp(l0 - lmax), jnp.exp(l1 - lmax), jnp.exp(l2 - lmax)
        mixed = (w0 * og_ref[0] + w1 * og_ref[1] + w2 * og_ref[2]) / (w0 + w1 + w2)
        o_ref[:, cols] = mixed.astype(o_ref.dtype)


def _dilated_attention(qkv, slopes, seq_tokens, *, tq=ATTN_QUERY_TILE, hg=ATTN_HEADS_PER_STEP):
    t = qkv.shape[0]
    wcols = hg * HEAD_DIM
    blocks_per_part = ATTN_WIDTH // wcols
    in_specs = []
    args = []
    max_halo = HALF_WINDOW * max(d for _, d in DILATED_GROUPS)
    for g, (_, d) in enumerate(DILATED_GROUPS):
        halo = HALF_WINDOW * d
        ratio = tq // halo
        n_halo_blocks = t // halo

        def col(part, g=g):
            return lambda i, h, *_: (part * N_GROUPS + g) * blocks_per_part + h

        def cur(part, g=g):
            c = col(part, g)
            return pl.BlockSpec((tq, wcols), lambda i, h, *_: (i, c(i, h)))

        def prev(part, g=g, ratio=ratio):
            c = col(part, g)
            return pl.BlockSpec((halo, wcols),
                                lambda i, h, *_: (jnp.maximum(i * ratio - 1, 0), c(i, h)))

        def nxt(part, g=g, ratio=ratio, n_halo_blocks=n_halo_blocks):
            c = col(part, g)
            return pl.BlockSpec((halo, wcols),
                                lambda i, h, *_: (jnp.minimum((i + 1) * ratio, n_halo_blocks - 1), c(i, h)))

        in_specs += [cur(0), prev(1), cur(1), nxt(1), prev(2), cur(2), nxt(2)]
        args += [qkv] * 7

    grid_spec = pltpu.PrefetchScalarGridSpec(
        num_scalar_prefetch=1,
        grid=(t // tq, ATTN_WIDTH // wcols),
        in_specs=in_specs,
        out_specs=pl.BlockSpec((tq, wcols), lambda i, h, *_: (i, h)),
        scratch_shapes=[
            pltpu.VMEM((tq, HEAD_DIM), F32),
            pltpu.VMEM((tq + 2 * max_halo, HEAD_DIM), F32),
            pltpu.VMEM((tq + 2 * max_halo, HEAD_DIM), F32),
            pltpu.VMEM((N_GROUPS, tq, HEAD_DIM), F32),
            pltpu.VMEM((N_GROUPS, tq, HEAD_DIM), F32),
        ],
    )
    return pl.pallas_call(
        functools.partial(_attn_kernel, tq=tq, hg=hg, seq_tokens=seq_tokens),
        grid_spec=grid_spec,
        out_shape=jax.ShapeDtypeStruct((t, ATTN_WIDTH), BF16),
        compiler_params=pltpu.CompilerParams(
            dimension_semantics=("parallel", "arbitrary"),
            vmem_limit_bytes=VMEM_LIMIT_BYTES),
        name="dilated_attention",
    )(slopes, *args)


TWIDDLE_RADIX = 16
PITCH_PAD = 8


def _split_hi_lo_np(m):
    hi = m.astype(ml_dtypes.bfloat16)
    lo = (m - hi.astype(np.float64)).astype(ml_dtypes.bfloat16)
    return np.stack([hi, lo])


@functools.lru_cache(maxsize=None)
def _dft_constants(r):
    h = r // 2
    idx = np.arange(r, dtype=np.float64)
    ang = 2.0 * np.pi * np.outer(idx, idx) / r
    c, s = np.cos(ang), np.sin(ang)
    f1 = np.block([[c[:, :h], s[:, :h]], [-s[:, :h], c[:, :h]]])
    f1_real = np.concatenate([c, -s], axis=0)
    f2 = np.block([[c, s], [-s, c]])
    f2_inv = np.block([[c, -s], [s, c]])
    f1_inv = np.block([[c[:h, :], -s[:h, :]], [s[:h, :], c[:h, :]]])
    n = r * r
    coarse = np.arange(r // TWIDDLE_RADIX, dtype=np.float64)[:, None] * TWIDDLE_RADIX
    fine = np.arange(TWIDDLE_RADIX, dtype=np.float64)[:, None]

    def table(mult):
        a = 2.0 * np.pi * mult * idx[None, :] / n
        t = np.stack([np.cos(a), np.sin(a)], axis=1)
        return np.ascontiguousarray(np.broadcast_to(t[..., None], t.shape + (LANES,))).astype(np.float32)

    return dict(f1=_split_hi_lo_np(f1), f1_real=_split_hi_lo_np(f1_real), f2=_split_hi_lo_np(f2),
                f2_inv=_split_hi_lo_np(f2_inv), f1_inv=_split_hi_lo_np(f1_inv),
                t1=table(coarse), t2=table(fine))


def _split_hi_lo(data):
    d_hi = data.astype(BF16)
    return d_hi, (data - d_hi.astype(F32)).astype(BF16)


def _dot3(m_ref, d_hi, d_lo):
    m_hi, m_lo = m_ref[0], m_ref[1]
    return (jnp.dot(m_hi, d_hi, preferred_element_type=F32)
            + jnp.dot(m_hi, d_lo, preferred_element_type=F32)
            + jnp.dot(m_lo, d_hi, preferred_element_type=F32))


def _staged_loop(n_items, stages):
    def body(it, carry):
        state = [it * FFT_BLOCKS + j for j in range(FFT_BLOCKS)]
        for stage in stages:
            state = [stage(s) for s in state]
        return carry

    lax.fori_loop(0, n_items // FFT_BLOCKS, body, 0)


def _twiddle(t1_ref, t2_ref, idx):
    a = idx // TWIDDLE_RADIX
    b = idx % TWIDDLE_RADIX
    c1, s1 = t1_ref[a, 0], t1_ref[a, 1]
    c2, s2 = t2_ref[b, 0], t2_ref[b, 1]
    return c1 * c2 - s1 * s2, s1 * c2 + c1 * s2


def _lanes(x, half):
    return x[:, half * LANES:(half + 1) * LANES]


def _slab_rows(r, chunk):
    return r * (chunk + PITCH_PAD)


def _blocks_to_rows(src_ref, dst_ref, r):
    chunk = src_ref.shape[1]
    pitch = chunk + PITCH_PAD

    def body(g, carry):
        dst_ref[pl.ds(pl.multiple_of(g * pitch, 8), chunk), :] = src_ref[g]
        return carry

    lax.fori_loop(0, r, body, 0, unroll=COPY_UNROLL)


def _rows_to_blocks(src_ref, dst_ref, r):
    chunk = dst_ref.shape[1]
    pitch = chunk + PITCH_PAD

    def body(g, carry):
        dst_ref[g] = src_ref[pl.ds(pl.multiple_of(g * pitch, 8), chunk), :]
        return carry

    lax.fori_loop(0, r, body, 0, unroll=COPY_UNROLL)


def _forward_n1_store(out, first_n2, n2l, t1_ref, t2_ref, sre_ref, sim_ref, r):
    pitch = r + PITCH_PAD
    for half in range(2):
        re, im = _lanes(out[:r], half), _lanes(out[r:], half)
        n2 = first_n2 + n2l + half
        c, s = _twiddle(t1_ref, t2_ref, n2)
        rows = pl.ds(n2, r, stride=pitch)
        sre_ref[rows, :] = re * c + im * s
        sim_ref[rows, :] = im * c - re * s


def _stage_a(x_ref, x2d, chunk, first_n2, f1_ref, t1_ref, t2_ref, sre_ref, sim_ref, r):
    _blocks_to_rows(x_ref, x2d, r)

    def gather(jj):
        n2l = 2 * jj
        rhs = jnp.concatenate([x2d[pl.ds(n2l, r, stride=chunk + PITCH_PAD), :],
                               x2d[pl.ds(n2l + 1, r, stride=chunk + PITCH_PAD), :]], axis=1)
        return n2l, _split_hi_lo(rhs)

    def transform(state):
        n2l, (d_hi, d_lo) = state
        return n2l, _dot3(f1_ref, d_hi, d_lo)

    def twiddle_store(state):
        n2l, out = state
        _forward_n1_store(out, first_n2, n2l, t1_ref, t2_ref, sre_ref, sim_ref, r)

    _staged_loop(chunk // 2, [gather, transform, twiddle_store])


def _load_k1_pair(sre_ref, sim_ref, k1, r):
    pitch = r + PITCH_PAD
    r0 = pl.multiple_of(k1 * pitch, 8)
    r1 = pl.multiple_of((k1 + 1) * pitch, 8)
    rows = (pl.ds(r0, r), pl.ds(r1, r))
    rhs = jnp.concatenate([
        jnp.concatenate([sre_ref[rows[0], :], sre_ref[rows[1], :]], axis=1),
        jnp.concatenate([sim_ref[rows[0], :], sim_ref[rows[1], :]], axis=1)], axis=0)
    return rhs, rows


def _load_n2_pair(sre_ref, sim_ref, n2, r):
    pitch = r + PITCH_PAD
    rows0 = pl.ds(n2, r, stride=pitch)
    rows1 = pl.ds(n2 + 1, r, stride=pitch)
    return jnp.concatenate([
        jnp.concatenate([sre_ref[rows0, :], sre_ref[rows1, :]], axis=1),
        jnp.concatenate([sim_ref[rows0, :], sim_ref[rows1, :]], axis=1)], axis=0)


def _stage_bc(first_k1, chunk, cs_ref, f2_ref, f2i_ref, t1_ref, t2_ref, sre_ref, sim_ref, r):
    def load(jj):
        k1l = 2 * jj
        rhs, rows = _load_k1_pair(sre_ref, sim_ref, first_k1 + k1l, r)
        return k1l, rows, _split_hi_lo(rhs)

    def forward(state):
        k1l, rows, (d_hi, d_lo) = state
        return k1l, rows, _dot3(f2_ref, d_hi, d_lo)

    def filter_multiply(state):
        k1l, rows, x = state
        p_re, p_im = [], []
        for half in range(2):
            xr, xi = _lanes(x[:r], half), _lanes(x[r:], half)
            crow = pl.ds(pl.multiple_of((k1l + half) * r, 8), r)
            cr, ci = cs_ref[0, crow, :], cs_ref[1, crow, :]
            p_re.append(xr * cr - xi * ci)
            p_im.append(xr * ci + xi * cr)
        prod = jnp.concatenate([jnp.concatenate(p_re, axis=1), jnp.concatenate(p_im, axis=1)], axis=0)
        return k1l, rows, _split_hi_lo(prod)

    def inverse(state):
        k1l, rows, (d_hi, d_lo) = state
        return k1l, rows, _dot3(f2i_ref, d_hi, d_lo)

    def twiddle_store(state):
        k1l, rows, y = state
        for half in range(2):
            re, im = _lanes(y[:r], half), _lanes(y[r:], half)
            c, s = _twiddle(t1_ref, t2_ref, first_k1 + k1l + half)
            sre_ref[rows[half], :] = re * c - im * s
            sim_ref[rows[half], :] = im * c + re * s

    _staged_loop(chunk // 2, [load, forward, filter_multiply, inverse, twiddle_store])


def _hyena_conv_kernel(x_ref, gate_ref, cs_ref, f1_ref, f1i_ref, f2_ref, f2i_ref, t1_ref, t2_ref,
                       o_ref, sre_ref, sim_ref, x2d, g2d, *, r, ca, cb):
    n_a, n_bc = r // ca, r // cb
    p_bc1, p_da, p_bc2, p_d2 = n_a, n_a + n_bc, 2 * n_a + n_bc, 2 * n_a + 2 * n_bc
    step = pl.program_id(2)
    slab_stride = ca + PITCH_PAD

    def inverse_n1_stages(first_n2):
        def gather(jj):
            n2l = 2 * jj
            return n2l, _split_hi_lo(_load_n2_pair(sre_ref, sim_ref, first_n2 + n2l, r))

        def inverse_n1(state):
            n2l, (d_hi, d_lo) = state
            return n2l, _dot3(f1i_ref, d_hi, d_lo)

        return [gather, inverse_n1]

    @pl.when(step < p_bc1)
    def _():
        _stage_a(x_ref, x2d, ca, step * ca, f1_ref, t1_ref, t2_ref, sre_ref, sim_ref, r)

    @pl.when((step >= p_bc1) & (step < p_da))
    def _():
        _stage_bc((step - p_bc1) * cb, cb, cs_ref, f2_ref, f2i_ref, t1_ref, t2_ref, sre_ref, sim_ref, r)

    @pl.when((step >= p_da) & (step < p_bc2))
    def _():
        first_n2 = (step - p_da) * ca
        _blocks_to_rows(gate_ref, g2d, r)

        def gate(state):
            n2l, y = state
            z = jnp.concatenate([g2d[pl.ds(n2l + half, r, stride=slab_stride), :] * _lanes(y, half)
                                 for half in range(2)], axis=1)
            return n2l, _split_hi_lo(z)

        def forward_n1(state):
            n2l, (d_hi, d_lo) = state
            return n2l, _dot3(f1_ref, d_hi, d_lo)

        def twiddle_store(state):
            n2l, out = state
            _forward_n1_store(out, first_n2, n2l, t1_ref, t2_ref, sre_ref, sim_ref, r)

        _staged_loop(ca // 2, inverse_n1_stages(first_n2) + [gate, forward_n1, twiddle_store])

    @pl.when((step >= p_bc2) & (step < p_d2))
    def _():
        _stage_bc((step - p_bc2) * cb, cb, cs_ref, f2_ref, f2i_ref, t1_ref, t2_ref, sre_ref, sim_ref, r)

    @pl.when(step >= p_d2)
    def _():
        first_n2 = (step - p_d2) * ca
        _blocks_to_rows(gate_ref, g2d, r)

        def gate_store(state):
            n2l, y = state
            for half in range(2):
                tok = pl.ds(n2l + half, r, stride=slab_stride)
                x2d[tok, :] = g2d[tok, :] * _lanes(y, half)

        _staged_loop(ca // 2, inverse_n1_stages(first_n2) + [gate_store])
        _rows_to_blocks(x2d, o_ref, r)


def _hyena_conv_kernel_aliased(*refs, **kw):
    _hyena_conv_kernel(*refs[1:], **kw)


def _hyena_long_conv(u, row0, spectrum, r, n_pairs, out_buf=None, *, ca, cb=16):
    n_a, n_bc = r // ca, r // cb
    p_bc1, p_da, p_bc2, p_d2 = n_a, n_a + n_bc, 2 * n_a + n_bc, 2 * n_a + 2 * n_bc
    n_slabs = u.shape[0] // 3
    consts = _dft_constants(r)
    t = u.shape[1]
    uv = u.reshape(u.shape[0], t // r, r, LANES)
    blk0 = row0 // (r * r)

    def gate_index(s, p, st):
        second = st >= p_bc2
        chunk = jnp.where(second, st - p_d2, st - p_da)
        return (jnp.where(second, 2 * n_slabs, n_slabs) + s, blk0 + p, jnp.clip(chunk, 0, n_a - 1), 0)

    def spectrum_index(s, p, st):
        second = st >= p_bc2
        chunk = jnp.where(second, st - p_bc2, st - p_bc1)
        return (jnp.where(second, n_slabs, 0) + s, 0, jnp.clip(chunk, 0, n_bc - 1), 0)

    def full(a):
        return pl.BlockSpec(a.shape, lambda s, p, st, nd=a.ndim: (0,) * nd)

    const_args = [consts["f1"], consts["f1_inv"], consts["f2"], consts["f2_inv"], consts["t1"], consts["t2"]]
    in_specs = [
        pl.BlockSpec((None, r, ca, LANES), lambda s, p, st: (s, blk0 + p, jnp.minimum(st, n_a - 1), 0)),
        pl.BlockSpec((None, r, ca, LANES), gate_index),
        pl.BlockSpec((None, 2, cb * r, LANES), spectrum_index),
    ] + [full(a) for a in const_args]
    args = [uv, uv, spectrum, *const_args]
    body = _hyena_conv_kernel
    aliases = {}
    if out_buf is not None:
        in_specs = [pl.BlockSpec(memory_space=pl.ANY)] + in_specs
        args = [out_buf.reshape(n_slabs, t // r, r, LANES)] + args
        body = _hyena_conv_kernel_aliased
        aliases = {0: 0}
    out = pl.pallas_call(
        functools.partial(body, r=r, ca=ca, cb=cb),
        grid=(n_slabs, n_pairs, 3 * n_a + 2 * n_bc),
        in_specs=in_specs,
        out_specs=pl.BlockSpec((None, r, ca, LANES),
                               lambda s, p, st: (s, blk0 + p, jnp.clip(st - p_d2, 0, n_a - 1), 0)),
        out_shape=jax.ShapeDtypeStruct((n_slabs, t // r, r, LANES), F32),
        scratch_shapes=[pltpu.VMEM((r * (r + PITCH_PAD), LANES), F32),
                        pltpu.VMEM((r * (r + PITCH_PAD), LANES), F32),
                        pltpu.VMEM((_slab_rows(r, ca), LANES), F32),
                        pltpu.VMEM((_slab_rows(r, ca), LANES), F32)],
        input_output_aliases=aliases,
        compiler_params=pltpu.CompilerParams(
            dimension_semantics=("parallel", "arbitrary", "arbitrary"),
            vmem_limit_bytes=VMEM_LIMIT_BYTES),
        name="hyena_long_conv",
    )(*args)
    return out.reshape(n_slabs, t, LANES)


def _fft_spectrum_kernel(c_ref, scale_ref, shift_ref, f1_ref, f2_ref, t1_ref, t2_ref, o_ref, sre_ref, sim_ref, c2d,
                         *, r, ca, cb):
    n_a = r // ca
    step = pl.program_id(2)

    @pl.when(step < n_a)
    def _():
        _stage_a(c_ref, c2d, ca, step * ca, f1_ref, t1_ref, t2_ref, sre_ref, sim_ref, r)

    @pl.when(step >= n_a)
    def _():
        first_k1 = (step - n_a) * cb

        def load(jj):
            k1l = 2 * jj
            rhs, _ = _load_k1_pair(sre_ref, sim_ref, first_k1 + k1l, r)
            return k1l, _split_hi_lo(rhs)

        def forward(state):
            k1l, (d_hi, d_lo) = state
            return k1l, _dot3(f2_ref, d_hi, d_lo)

        def store(state):
            k1l, x = state
            for half in range(2):
                orow = pl.ds(pl.multiple_of((k1l + half) * r, 8), r)
                o_ref[0, orow, :] = _lanes(x[:r], half) * scale_ref[...] + shift_ref[...]
                o_ref[1, orow, :] = _lanes(x[r:], half) * scale_ref[...]

        _staged_loop(cb // 2, [load, forward, store])


def _fft_spectrum(c, scale, shift, r, *, ca=32, cb=16):
    n_slabs, n, _ = c.shape
    width = n_slabs * LANES
    n_a, n_bc = r // ca, r // cb
    consts = _dft_constants(r)
    cv = c.reshape(n_slabs, r, r, LANES)

    def full(a):
        return pl.BlockSpec(a.shape, lambda s, o, st, nd=a.ndim: (0,) * nd)

    const_args = [consts["f1_real"], consts["f2"], consts["t1"], consts["t2"]]
    return pl.pallas_call(
        functools.partial(_fft_spectrum_kernel, r=r, ca=ca, cb=cb),
        grid=(n_slabs, 1, n_a + n_bc),
        in_specs=[pl.BlockSpec((None, r, ca, LANES), lambda s, o, st: (s, 0, jnp.minimum(st, n_a - 1), 0)),
                  pl.BlockSpec((1, LANES), lambda s, o, st: (0, s)),
                  pl.BlockSpec((1, LANES), lambda s, o, st: (0, s))]
        + [full(a) for a in const_args],
        out_specs=pl.BlockSpec((None, 2, cb * r, LANES), lambda s, o, st: (s, 0, jnp.maximum(st - n_a, 0), 0)),
        out_shape=jax.ShapeDtypeStruct((n_slabs, 2, n, LANES), F32),
        scratch_shapes=[pltpu.VMEM((r * (r + PITCH_PAD), LANES), F32),
                        pltpu.VMEM((r * (r + PITCH_PAD), LANES), F32),
                        pltpu.VMEM((_slab_rows(r, ca), LANES), F32)],
        compiler_params=pltpu.CompilerParams(
            dimension_semantics=("parallel", "arbitrary", "arbitrary"),
            vmem_limit_bytes=VMEM_LIMIT_BYTES),
        name="fft_spectrum",
    )(cv, scale.reshape(1, width), shift.reshape(1, width), *const_args)


FILTER_EMB = 2 * FILTER_BANDS + 1


def _dot_f32(a, b):
    a_hi, a_lo = _split_hi_lo(a)
    b_hi, b_lo = _split_hi_lo(b)
    return (jnp.dot(a_hi, b_hi, preferred_element_type=F32)
            + jnp.dot(a_hi, b_lo, preferred_element_type=F32)
            + jnp.dot(a_lo, b_hi, preferred_element_type=F32))


def _filter_kernel(bands_ref, w1_ref, b1_ref, w2_ref, b2_ref, freq_ref, w3f_ref, w3b_ref, delta_ref,
                   c_ref, norm_ref, hid_ref, *, length, tr):
    j = pl.program_id(0)
    i = pl.program_id(1)
    row = i * tr + lax.broadcasted_iota(jnp.int32, (tr, 1), 0)
    pos_idx = jnp.where(row < length, row, 2 * length - row)
    mf = pos_idx.astype(F32)
    t = mf / (length - 1)
    rows = pl.ds(pl.multiple_of(i * tr, 8), tr)

    @pl.when(j == 0)
    def _():
        w = 2.0 * math.pi * mf / length
        arg = w * bands_ref[...]
        lane = lax.broadcasted_iota(jnp.int32, (1, LANES), 1)
        pos = jnp.where(lane == 0, t,
                        jnp.where(lane <= FILTER_BANDS, jnp.cos(arg),
                                  jnp.where(lane < FILTER_EMB, -jnp.sin(arg), 0.0)))
        h = jnp.sin(freq_ref[...] * (_dot_f32(pos, w1_ref[...]) + b1_ref[...]))
        h = jnp.sin(freq_ref[...] * (_dot_f32(h, w2_ref[...]) + b2_ref[...]))
        hid_ref[rows, :] = h

    h = hid_ref[rows, :]
    decay = jnp.exp(-t * delta_ref[...])
    tile_is_fwd = (i + 1) * tr <= length
    main = _dot_f32(h, jnp.where(tile_is_fwd, w3f_ref[...], w3b_ref[...])) * decay
    main = jnp.where(row == length, 0.0, main)
    extra = jnp.where(row[0:8] == 0, _dot_f32(h[0:8], w3b_ref[...]) * decay[0:8], 0.0)
    for s in range(c_ref.shape[0]):
        slab = slice(s * LANES, (s + 1) * LANES)
        c_ref[s] = main[:, slab]
        c_ref[s, 0:8, :] = main[0:8, slab] + extra[:, slab]
    part = jnp.abs(main[0:8]) + jnp.abs(extra)
    for k in range(1, tr // 8):
        part = part + jnp.abs(main[8 * k:8 * (k + 1)])

    @pl.when(i == 0)
    def _():
        norm_ref[...] = part

    @pl.when(i > 0)
    def _():
        norm_ref[...] += part


def _hyena_filter_kernel(length, f_w1, f_b1, f_w2, f_b2, f_freq, f_w3, *, tr=512, tc=1024):
    hidden = f_w1.shape[1]
    dm = f_w3.shape[1] // (2 * HYENA_ORDER)
    width = HYENA_ORDER * dm
    tr, tc = min(tr, length), min(tc, width)
    assert length % tr == 0 and width % tc == 0
    bands = jnp.linspace(1e-4, FILTER_BANDS - 1, FILTER_BANDS, dtype=F32)
    bands_row = jnp.zeros((1, LANES), F32).at[0, 1:FILTER_BANDS + 1].set(bands)
    bands_row = bands_row.at[0, FILTER_BANDS + 1:FILTER_EMB].set(bands)

    def pad_to(a, rows, cols):
        return jnp.zeros((rows, cols), F32).at[:a.shape[0], :a.shape[1]].set(a)

    w3 = f_w3.reshape(hidden, HYENA_ORDER, 2, dm)
    w3f = pad_to(w3[:, :, 0].reshape(hidden, width), LANES, width)
    w3b = pad_to(w3[:, :, 1].reshape(hidden, width), LANES, width)
    deltas = jnp.abs(jnp.linspace(MIN_DECAY, MAX_DECAY, dm, dtype=F32))
    delta_row = jnp.tile(deltas, HYENA_ORDER).reshape(1, width)
    small = [bands_row, pad_to(f_w1, LANES, LANES), pad_to(f_b1[None], 1, LANES), pad_to(f_w2, LANES, LANES),
             pad_to(f_b2[None], 1, LANES), pad_to(f_freq[None], 1, LANES)]
    c, norm = pl.pallas_call(
        functools.partial(_filter_kernel, length=length, tr=tr),
        grid=(width // tc, 2 * length // tr),
        in_specs=[pl.BlockSpec(a.shape, lambda j, i: (0, 0)) for a in small] + [
            pl.BlockSpec((LANES, tc), lambda j, i: (0, j)),
            pl.BlockSpec((LANES, tc), lambda j, i: (0, j)),
            pl.BlockSpec((1, tc), lambda j, i: (0, j)),
        ],
        out_specs=[pl.BlockSpec((tc // LANES, tr, LANES), lambda j, i: (j, i, 0)),
                   pl.BlockSpec((8, tc), lambda j, i: (0, j))],
        out_shape=[jax.ShapeDtypeStruct((width // LANES, 2 * length, LANES), F32),
                   jax.ShapeDtypeStruct((8, width), F32)],
        scratch_shapes=[pltpu.VMEM((2 * length, LANES), F32)],
        compiler_params=pltpu.CompilerParams(
            dimension_semantics=("arbitrary", "arbitrary"),
            vmem_limit_bytes=VMEM_LIMIT_BYTES),
        name="hyena_filter",
    )(*small, w3f, w3b, delta_row)
    return c, jnp.sum(norm, axis=0)


def _filter_spectra(length, r, filt_params, skip):
    c, norm = _hyena_filter_kernel(length, *filt_params)
    n = 2 * length
    return _fft_spectrum(c, 1.0 / (n * norm), skip.reshape(-1) / n, r)


def _hyena_mixer(u, filt_params, skip, seq_tokens, batches):
    n_prompt, s_prompt, s_sample = seq_tokens
    z = None
    for seq_len, n_batch, row0 in ((s_prompt, batches[0], 0), (s_sample, batches[1], n_prompt)):
        r = math.isqrt(2 * seq_len)
        assert r * r == 2 * seq_len and n_batch % 2 == 0 and row0 % (r * r) == 0
        spec = _filter_spectra(seq_len, r, filt_params, skip)
        chunk = 16 if r > 64 else 32
        z = _hyena_long_conv(u, row0, spec, r, n_batch // 2, z, ca=chunk)
    return z


def kernel(x_prompt, x_sample, ln_g, ln_b, ffn_w_in, ffn_w_out, attn_w_qkv, attn_w_o, hy_w_in, hy_b_in,
           hy_conv_w, hy_conv_b, hy_f_w1, hy_f_b1, hy_f_w2, hy_f_b2, hy_f_freq, hy_f_w3, hy_skip,
           hy_w_out, hy_b_out):
    bp, sp, dm = x_prompt.shape
    bs, ss, _ = x_sample.shape
    n_prompt = bp * sp
    n_sample = bs * ss
    seq_tokens = (n_prompt, sp, ss)
    n_tokens = n_prompt + n_sample
    slopes = jnp.exp2(-8.0 * jnp.arange(1, N_SLOTS + 1, dtype=F32) / N_SLOTS)
    zero_bias_dm = jnp.zeros((dm,), F32)
    ffn_w_in, ffn_w_out, attn_w_qkv, attn_w_o, hy_w_in, hy_w_out = (
        w.astype(BF16) for w in (ffn_w_in, ffn_w_out, attn_w_qkv, attn_w_o, hy_w_in, hy_w_out))

    x = None
    for i in range(DEPTH):
        ffn = ((ffn_w_in, (i, 0)), (ffn_w_out, (i, 0)), ln_g[i, 0], ln_b[i, 0])
        if i == 0:
            x = _ffn_layer(x_prompt.reshape(n_prompt, dm), *ffn, out_rows=n_tokens)
            x = _ffn_layer(x_sample.reshape(n_sample, dm), *ffn, out_rows=n_tokens, out_row0=n_prompt, out_buf=x)
        else:
            x = _ffn_layer(x, *ffn)
        j = i // 2
        if i % 2 == 0:
            qkv = _projection(x, (attn_w_qkv, (j,)), BF16)
            att = _dilated_attention(qkv, slopes, seq_tokens)
            x = _projection_ln(x, att, (attn_w_o, (j,)), zero_bias_dm, ln_g[i, 1], ln_b[i, 1])
        else:
            u = _projection_short_conv(x, (hy_w_in, (j,)), hy_b_in[j], hy_conv_w[j], hy_conv_b[j], seq_tokens)
            filt = (hy_f_w1[j], hy_f_b1[j], hy_f_w2[j], hy_f_b2[j], hy_f_freq[j], hy_f_w3[j])
            z = _hyena_mixer(u, filt, hy_skip[j], seq_tokens, (bp, bs))
            x = _projection_ln(x, z, (hy_w_out, (j,)), hy_b_out[j], ln_g[i, 1], ln_b[i, 1])
        ffn = ((ffn_w_in, (i, 1)), (ffn_w_out, (i, 1)), ln_g[i, 2], ln_b[i, 2])
        if i < DEPTH - 1:
            x = _ffn_layer(x, *ffn)

    y_prompt = _ffn_layer(x, *ffn, n_rows=n_prompt)
    y_sample = _ffn_layer(x, *ffn, x_row0=n_prompt, n_rows=n_sample)
    return (y_prompt.reshape(bp, sp, dm), y_sample.reshape(bs, ss, dm))
```
